```python
import jax
import jax.numpy as jnp
from jax import lax
import numpy as np

D_MODEL = 1024
BATCH = 8
SEQ = 2048
DEPTH = 1

MIX_WIDTH = D_MODEL
HEAD_DIM = 64
NSA_WIDTH = MIX_WIDTH // 2
CONV_WIDTH = MIX_WIDTH - NSA_WIDTH
NSA_HEADS = NSA_WIDTH // HEAD_DIM
NSA_KV_HEADS = 2
GQA_REP = NSA_HEADS // NSA_KV_HEADS
N_NSA_BRANCH = 3
CMP_LEN = 32
CMP_STRIDE = 16
SEL_LEN = 64
SEL_TOPN = 16
WINDOW = 512
Q_BLOCK = 64
FORCE_BONUS = 1.0e4
CONV_TAPS = 31
PEER_HEADS = 8
PEER_NKEYS = 128
PEER_EXPERTS = PEER_NKEYS * PEER_NKEYS
PEER_QDIM = 256
PEER_TOPK = 16
PEER_TOK_BLOCK = 128
NORM_EPS = 1e-6
NEG_INF = -1e30

N_Q_COLS = NSA_HEADS * HEAD_DIM
N_KV_COLS = 2 * N_NSA_BRANCH * NSA_KV_HEADS * HEAD_DIM
N_GATE_COLS = NSA_HEADS * N_NSA_BRANCH
N_GLU_COLS = 2 * CONV_WIDTH
IN_COLS = N_Q_COLS + N_KV_COLS + N_GATE_COLS + N_GLU_COLS

kernel_name = 'hybrid_nsa_conformer_peer_layer'


def rmsnorm(x, g):
    xf = x.astype(jnp.float32)
    y = xf * lax.rsqrt(jnp.mean(xf * xf, axis=-1, keepdims=True) + NORM_EPS)
    return (y * g).astype(x.dtype)


def modulate(h, shift, scale):
    return h * (1.0 + scale[:, None, :]) + shift[:, None, :]


def masked_softmax(s, mask):
    p = jax.nn.softmax(jnp.where(mask, s, NEG_INF), axis=-1)
    return jnp.where(mask, p, 0.0)


def alibi_slopes():
    h = np.arange(1, NSA_HEADS + 1, dtype=np.float32)
    return (2.0 ** (-8.0 * h / NSA_HEADS)).astype(np.float32).reshape(NSA_KV_HEADS, GQA_REP)


def nsa_mixer(q, kv, gate_logits, cmp_pe_k, cmp_pe_v, w_cmp_k, w_cmp_v, qk_norm_g):
    B, S = q.shape[0], q.shape[1]
    G, R, hd = NSA_KV_HEADS, GQA_REP, HEAD_DIM
    slopes = jnp.asarray(alibi_slopes())
    t_pos = np.arange(S)
    q = rmsnorm(q, qk_norm_g[0]) * (hd ** -0.5)
    q = q.reshape(B, S, G, R, hd).transpose(0, 2, 3, 1, 4)
    kv = kv.transpose(2, 0, 3, 1, 4)
    k_cmp, v_cmp, k_slc, v_slc, k_win, v_win = kv[0], kv[1], kv[2], kv[3], kv[4], kv[5]

    n_cmp = (S - CMP_LEN) // CMP_STRIDE + 1
    blk_tok = np.arange(n_cmp)[:, None] * CMP_STRIDE + np.arange(CMP_LEN)[None, :]
    kc = jnp.einsum('bgnld,lde->bgne', k_cmp[:, :, blk_tok] + cmp_pe_k, w_cmp_k)
    vc = jnp.einsum('bgnld,lde->bgne', v_cmp[:, :, blk_tok] + cmp_pe_v, w_cmp_v)
    kc = rmsnorm(kc, qk_norm_g[1])
    blk_end = blk_tok[:, -1]
    cdist = (t_pos[:, None] - blk_end[None, :]).astype(np.float32)
    s_cmp = jnp.einsum('bgrtd,bgnd->bgrtn', q, kc).astype(jnp.float32) - slopes[:, :, None, None] * cdist
    p_cmp = masked_softmax(s_cmp, cdist >= 0)
    o_cmp = jnp.einsum('bgrtn,bgnd->bgrtd', p_cmp.astype(vc.dtype), vc)

    n_sel = S // SEL_LEN
    top_n = min(SEL_TOPN, n_sel)
    sel_start = np.arange(n_sel) * SEL_LEN
    overlap = np.clip(np.minimum(blk_tok[:, -1:] + 1, sel_start[None, :] + SEL_LEN)
                      - np.maximum(blk_tok[:, :1], sel_start[None, :]), 0, None).astype(np.float32) / CMP_LEN
    imp = jnp.einsum('bgrtn,nj->bgtj', p_cmp, overlap)
    t_blk = t_pos // SEL_LEN
    j = np.arange(n_sel)
    valid = j[None, :] <= t_blk[:, None]
    forced = ((j[None, :] == 0) | (j[None, :] == t_blk[:, None]) | (j[None, :] == t_blk[:, None] - 1)).astype(np.float32)
    imp = jnp.where(valid, imp + FORCE_BONUS * forced, -1.0)
    _, sel_idx = lax.top_k(imp, top_n)

    ks_blocks = rmsnorm(k_slc, qk_norm_g[2]).reshape(B, G, n_sel, SEL_LEN, hd)
    vs_blocks = v_slc.reshape(B, G, n_sel, SEL_LEN, hd)
    kw_pad = jnp.pad(rmsnorm(k_win, qk_norm_g[3]), ((0, 0), (0, 0), (WINDOW, 0), (0, 0)))
    vw_pad = jnp.pad(v_win, ((0, 0), (0, 0), (WINDOW, 0), (0, 0)))
    n_q = S // Q_BLOCK
    q_chunks = q.reshape(B, G, R, n_q, Q_BLOCK, hd).transpose(3, 0, 1, 2, 4, 5)
    idx_chunks = sel_idx.reshape(B, G, n_q, Q_BLOCK, top_n).transpose(2, 0, 1, 3, 4)
    b_ix = jnp.arange(B)[:, None, None, None]
    g_ix = jnp.arange(G)[None, :, None, None]
    m_sel = top_n * SEL_LEN

    def block_fn(args):
        ci, qc, ic = args
        t = ci * Q_BLOCK + jnp.arange(Q_BLOCK)
        ks = ks_blocks[b_ix, g_ix, ic]
        vs = vs_blocks[b_ix, g_ix, ic]
        spos = ic[..., None] * SEL_LEN + jnp.arange(SEL_LEN)
        sd = (t[:, None, None] - spos).astype(jnp.float32)
        s = jnp.einsum('bgrqd,bgqnld->bgrqnl', qc, ks).astype(jnp.float32) \
            - slopes[None, :, :, None, None, None] * sd[:, :, None]
        smask = (sd >= 0)[:, :, None].reshape(B, G, 1, Q_BLOCK, m_sel)
        p = masked_softmax(s.reshape(B, G, R, Q_BLOCK, m_sel), smask)
        o_s = jnp.einsum('bgrqm,bgqmd->bgrqd', p.astype(vs.dtype), vs.reshape(B, G, Q_BLOCK, m_sel, hd))
        kw = lax.dynamic_slice_in_dim(kw_pad, ci * Q_BLOCK, Q_BLOCK + WINDOW, axis=2)
        vw = lax.dynamic_slice_in_dim(vw_pad, ci * Q_BLOCK, Q_BLOCK + WINDOW, axis=2)
        wpos = ci * Q_BLOCK - WINDOW + jnp.arange(Q_BLOCK + WINDOW)
        wd = t[:, None] - wpos[None, :]
        wmask = (wd >= 0) & (wd < WINDOW) & (wpos[None, :] >= 0)
        sw = jnp.einsum('bgrqd,bgmd->bgrqm', qc, kw).astype(jnp.float32) \
            - slopes[None, :, :, None, None] * wd.astype(jnp.float32)
        pw = masked_softmax(sw, wmask)
        o_w = jnp.einsum('bgrqm,bgmd->bgrqd', pw.astype(vw.dtype), vw)
        return o_s, o_w

    o_slc, o_win = lax.map(block_fn, (jnp.arange(n_q), q_chunks, idx_chunks))
    o_slc = o_slc.transpose(1, 2, 3, 0, 4, 5).reshape(B, G, R, S, hd)
    o_win = o_win.transpose(1, 2, 3, 0, 4, 5).reshape(B, G, R, S, hd)

    gates = jax.nn.sigmoid(gate_logits.astype(jnp.float32))
    gates = gates.reshape(B, S, G, R, N_NSA_BRANCH).transpose(0, 2, 3, 1, 4)
    o = gates[..., 0:1] * o_cmp + gates[..., 1:2] * o_slc + gates[..., 2:3] * o_win
    return o.transpose(0, 3, 1, 2, 4).reshape(B, S, NSA_WIDTH).astype(q.dtype)


def conv_mixer(u_glu, dw_w, dw_b, ln_g, ln_b):
    a, b = jnp.split(u_glu, 2, axis=-1)
    u = a * jax.nn.sigmoid(b)
    u = jnp.pad(u, ((0, 0), (CONV_TAPS - 1, 0), (0, 0)))
    y = lax.conv_general_dilated(u, dw_w[:, None, :], window_strides=(1,), padding='VALID',
                                 dimension_numbers=('NWC', 'WIO', 'NWC'),
                                 feature_group_count=CONV_WIDTH) + dw_b
    yf = y.astype(jnp.float32)
    mu = jnp.mean(yf, axis=-1, keepdims=True)
    var = jnp.mean(jnp.square(yf - mu), axis=-1, keepdims=True)
    yn = (yf - mu) * lax.rsqrt(var + NORM_EPS) * ln_g + ln_b
    return jax.nn.silu(yn).astype(u_glu.dtype)


def peer_ffn(h, w_q, sub_keys, u_tab, v_tab):
    B, S, D = h.shape
    T = B * S
    hf = h.reshape(T, D)
    q = (hf @ w_q).reshape(T, PEER_HEADS, 2, PEER_QDIM // 2)
    s = jnp.einsum('thcd,hcnd->thcn', q, sub_keys).astype(jnp.float32)
    s1, i1 = lax.top_k(s[:, :, 0], PEER_TOPK)
    s2, i2 = lax.top_k(s[:, :, 1], PEER_TOPK)
    cand = (s1[..., :, None] + s2[..., None, :]).reshape(T, PEER_HEADS, PEER_TOPK * PEER_TOPK)
    cand_idx = (i1[..., :, None] * PEER_NKEYS + i2[..., None, :]).reshape(T, PEER_HEADS, PEER_TOPK * PEER_TOPK)
    top_s, pos = lax.top_k(cand, PEER_TOPK)
    e_idx = jnp.take_along_axis(cand_idx, pos, axis=-1)
    gw = jax.nn.softmax(top_s, axis=-1).astype(h.dtype)
    n_blk = T // PEER_TOK_BLOCK

    def blk(args):
        hc, ec, gc = args
        ue = u_tab[ec]
        ve = v_tab[ec]
        act = jax.nn.gelu(jnp.einsum('td,thkd->thk', hc, ue), approximate=False)
        return jnp.einsum('thk,thkd->td', gc * act, ve)

    out = lax.map(blk, (hf.reshape(n_blk, PEER_TOK_BLOCK, D),
                        e_idx.reshape(n_blk, PEER_TOK_BLOCK, PEER_HEADS, PEER_TOPK),
                        gw.reshape(n_blk, PEER_TOK_BLOCK, PEER_HEADS, PEER_TOPK)))
    return out.reshape(B, S, D)


def setup_inputs(seed: int = 0) -> dict:
    key = jax.random.key(seed)
    ks = jax.random.split(key, 22)
    L = DEPTH

    def nrm(k, shape, scale):
        return jax.random.normal(k, shape, jnp.float32) * scale

    return {
        'x': nrm(ks[0], (BATCH, SEQ, D_MODEL), 1.0),
        'c': nrm(ks[1], (BATCH, D_MODEL), 1.0),
        'w_ada': nrm(ks[2], (L, D_MODEL, 6 * D_MODEL), 0.5 * D_MODEL ** -0.5),
        'b_ada': nrm(ks[3], (L, 6 * D_MODEL), 0.01),
        'norm_g': 1.0 + nrm(ks[4], (L, 2, D_MODEL), 0.02),
        'w_in': nrm(ks[5], (L, D_MODEL, IN_COLS), D_MODEL ** -0.5),
        'w_out': nrm(ks[6], (L, MIX_WIDTH, D_MODEL), MIX_WIDTH ** -0.5),
        'cmp_pe_k': nrm(ks[7], (L, CMP_LEN, HEAD_DIM), 0.1),
        'cmp_pe_v': nrm(ks[8], (L, CMP_LEN, HEAD_DIM), 0.1),
        'w_cmp_k': nrm(ks[9], (L, CMP_LEN, HEAD_DIM, HEAD_DIM), (CMP_LEN * HEAD_DIM) ** -0.5),
        'w_cmp_v': nrm(ks[10], (L, CMP_LEN, HEAD_DIM, HEAD_DIM), (CMP_LEN * HEAD_DIM) ** -0.5),
        'qk_norm_g': 1.0 + nrm(ks[11], (L, 4, HEAD_DIM), 0.02),
        'dw_w': nrm(ks[12], (L, CONV_TAPS, CONV_WIDTH), CONV_TAPS ** -0.5),
        'dw_b': nrm(ks[13], (L, CONV_WIDTH), 0.01),
        'conv_ln_g': 1.0 + nrm(ks[14], (L, CONV_WIDTH), 0.02),
        'conv_ln_b': nrm(ks[15], (L, CONV_WIDTH), 0.01),
        'peer_wq': nrm(ks[16], (L, D_MODEL, PEER_HEADS * PEER_QDIM), D_MODEL ** -0.5),
        'peer_sub_keys': nrm(ks[17], (L, PEER_HEADS, 2, PEER_NKEYS, PEER_QDIM // 2), (PEER_QDIM // 2) ** -0.5),
        'peer_u': nrm(ks[18], (L, PEER_EXPERTS, D_MODEL), D_MODEL ** -0.5),
        'peer_v': nrm(ks[19], (L, PEER_EXPERTS, D_MODEL), 0.5),
    }


def reference(x, c, w_ada, b_ada, norm_g, w_in, w_out, cmp_pe_k, cmp_pe_v, w_cmp_k, w_cmp_v,
              qk_norm_g, dw_w, dw_b, conv_ln_g, conv_ln_b, peer_wq, peer_sub_keys, peer_u, peer_v):
    B, S, _ = x.shape
    o_kv = N_Q_COLS
    o_gate = o_kv + N_KV_COLS
    o_glu = o_gate + N_GATE_COLS
    for l in range(DEPTH):
        mod = jax.nn.silu(c) @ w_ada[l] + b_ada[l]
        sh_m, sc_m, g_m, sh_f, sc_f, g_f = jnp.split(mod, 6, axis=-1)
        h = modulate(rmsnorm(x, norm_g[l, 0]), sh_m, sc_m)
        proj = h @ w_in[l]
        q = proj[..., :o_kv].reshape(B, S, NSA_HEADS, HEAD_DIM)
        kv = proj[..., o_kv:o_gate].reshape(B, S, 2 * N_NSA_BRANCH, NSA_KV_HEADS, HEAD_DIM)
        gl = proj[..., o_gate:o_glu].reshape(B, S, NSA_HEADS, N_NSA_BRANCH)
        glu = proj[..., o_glu:]
        a_out = nsa_mixer(q, kv, gl, cmp_pe_k[l], cmp_pe_v[l], w_cmp_k[l], w_cmp_v[l], qk_norm_g[l])
        c_out = conv_mixer(glu, dw_w[l], dw_b[l], conv_ln_g[l], conv_ln_b[l])
        mix = jnp.concatenate([a_out, c_out], axis=-1) @ w_out[l]
        x = x + g_m[:, None, :] * mix
        h = modulate(rmsnorm(x, norm_g[l, 1]), sh_f, sc_f)
        x = x + g_f[:, None, :] * peer_ffn(h, peer_wq[l], peer_sub_keys[l], peer_u[l], peer_v[l])
    return x
```

```python
import functools

import numpy as np
import jax
import jax.numpy as jnp
from jax import lax
from jax.experimental import pallas as pl
from jax.experimental.pallas import tpu as pltpu

F32 = jnp.float32
BF16 = jnp.bfloat16

D_MODEL = 1024
BATCH = 8
SEQ = 2048
DEPTH = 1
TOKENS = BATCH * SEQ

HEAD_DIM = 64
NSA_HEADS = 8
NSA_GROUPS = 2
GQA_REP = NSA_HEADS // NSA_GROUPS
NSA_WIDTH = NSA_HEADS * HEAD_DIM
CONV_WIDTH = 512
CMP_LEN = 32
CMP_STRIDE = 16
N_CMP = (SEQ - CMP_LEN) // CMP_STRIDE + 1
SEL_LEN = 64
N_SEL = SEQ // SEL_LEN
SEL_TOPN = 16
WINDOW = 512
FORCE_BONUS = 1.0e4
CONV_TAPS = 31
PEER_HEADS = 8
PEER_NKEYS = 128
PEER_EXPERTS = PEER_NKEYS * PEER_NKEYS
PEER_QDIM = 256
PEER_TOPK = 16
NORM_EPS = 1e-6
NEG_INF = -1e30
NEG_BIG = -3.0e38
IDX_BIG = 1.0e9

LANES = 128
VMEM_LIMIT = 48 * 1024 * 1024

TM_PROJ = 512
TQ = 128
TK = 128
WIN_KEYS = WINDOW + TQ
TS_CONV = 64
CONV_PAD = 32
TL_ROUTE = 128
TT_PEER = 256
ET_PEER = 256
G_ROW_STRIDE = PEER_NKEYS + 8

_NT = (((1,), (1,)), ((), ()))


def _cparams(sem):
    return pltpu.CompilerParams(dimension_semantics=sem, vmem_limit_bytes=VMEM_LIMIT)


def _ada_body(c_ref, w_ref, b_ref, o_ref):
    c = c_ref[...]
    sc = (c * jax.nn.sigmoid(c)).astype(BF16)
    o_ref[...] = jnp.dot(sc, w_ref[...].astype(BF16), preferred_element_type=F32) + b_ref[...]


def _ada(c, w, b):
    n = w.shape[1]
    tn = 1536
    return pl.pallas_call(
        _ada_body,
        grid=(n // tn,),
        in_specs=[pl.BlockSpec((BATCH, D_MODEL), lambda j: (0, 0)),
                  pl.BlockSpec((D_MODEL, tn), lambda j: (0, j)),
                  pl.BlockSpec((1, tn), lambda j: (0, j))],
        out_specs=pl.BlockSpec((BATCH, tn), lambda j: (0, j)),
        out_shape=jax.ShapeDtypeStruct((BATCH, n), F32),
        compiler_params=_cparams(("arbitrary",)),
        name="ada_mod",
    )(c, w, b)


def _norm_mod(x, g, shift, scale):
    ms = jnp.mean(x * x, axis=-1, keepdims=True)
    y = x * lax.rsqrt(ms + NORM_EPS) * g
    return y * (1.0 + scale) + shift


def _inproj_body(x_ref, mod_ref, g_ref, wq_ref, wkc_ref, wks_ref, wg_ref, wglu_ref,
                 q_ref, kc_ref, ks_ref, gl_ref, glu_ref):
    h = _norm_mod(x_ref[...], g_ref[...], mod_ref[0, 0:1, :], mod_ref[0, 1:2, :]).astype(BF16)
    for w_ref, o_ref in ((wq_ref, q_ref), (wkc_ref, kc_ref), (wks_ref, ks_ref),
                         (wg_ref, gl_ref), (wglu_ref, glu_ref)):
        o_ref[...] = jnp.dot(h, w_ref[...], preferred_element_type=F32)


def _inproj(xf, mod3, g, wq, wkc, wks, wg, wglu):
    tiles_per_batch = SEQ // TM_PROJ
    ws = (wq, wkc, wks, wg, wglu)
    row = lambda i: (i, 0)
    return pl.pallas_call(
        _inproj_body,
        grid=(TOKENS // TM_PROJ,),
        in_specs=[pl.BlockSpec((TM_PROJ, D_MODEL), row),
                  pl.BlockSpec((1, 6, D_MODEL), lambda i: (i // tiles_per_batch, 0, 0)),
                  pl.BlockSpec((1, D_MODEL), lambda i: (0, 0))]
                 + [pl.BlockSpec(w.shape, lambda i: (0, 0)) for w in ws],
        out_specs=[pl.BlockSpec((TM_PROJ, w.shape[1]), row) for w in ws],
        out_shape=[jax.ShapeDtypeStruct((TOKENS, w.shape[1]), F32) for w in ws],
        compiler_params=_cparams(("parallel",)),
        name="in_proj",
    )(xf, mod3, g, *ws)


def _rms_pair(x, gdup, lo):
    x2 = x * x
    s_lo = jnp.sum(jnp.where(lo, x2, 0.0), axis=-1, keepdims=True)
    s_hi = jnp.sum(jnp.where(lo, 0.0, x2), axis=-1, keepdims=True)
    rs = jnp.where(lo, lax.rsqrt(s_lo * (1.0 / HEAD_DIM) + NORM_EPS),
                   lax.rsqrt(s_hi * (1.0 / HEAD_DIM) + NORM_EPS))
    return x * rs * gdup


def _dup_halves(x, lo):
    xr = pltpu.roll(x, HEAD_DIM, 1)
    return jnp.where(lo, x, xr), jnp.where(lo, xr, x)


def _kvprep_body(r_ref, kvs_ref, wa_ref, wb_ref, pea_ref, peb_ref, g1_ref, g2_ref, g3_ref,
                 kcd_ref, vcd_ref, ksd_ref, vsd_ref, kwd_ref, vwd_ref):
    lane = lax.broadcasted_iota(jnp.int32, (1, LANES), 1)
    lo = lane < HEAD_DIM
    r = r_ref[0]
    a = jnp.dot((r + pea_ref[...]).astype(BF16), wa_ref[...], preferred_element_type=F32)
    b = jnp.dot((r + peb_ref[...]).astype(BF16), wb_ref[...], preferred_element_type=F32)
    c = a + pltpu.roll(b, SEQ // CMP_STRIDE - 1, 0)
    kc0, kc1 = _dup_halves(_rms_pair(c[:, :LANES], g1_ref[...], lo), lo)
    vc0, vc1 = _dup_halves(c[:, LANES:], lo)
    kcd_ref[0, 0] = kc0.astype(BF16)
    kcd_ref[0, 1] = kc1.astype(BF16)
    vcd_ref[0, 0] = vc0.astype(BF16)
    vcd_ref[0, 1] = vc1.astype(BF16)

    rows = 256

    def chunk(i, carry):
        r0 = pl.multiple_of(i * rows, rows)
        blk = kvs_ref[pl.ds(r0, rows), :]
        parts = (
            (ksd_ref, _rms_pair(blk[:, 0:LANES], g2_ref[...], lo)),
            (vsd_ref, blk[:, LANES:2 * LANES]),
            (kwd_ref, _rms_pair(blk[:, 2 * LANES:3 * LANES], g3_ref[...], lo)),
            (vwd_ref, blk[:, 3 * LANES:4 * LANES]),
        )
        for ref, val in parts:
            d0, d1 = _dup_halves(val, lo)
            ref[0, 0, pl.ds(r0, rows), :] = d0.astype(BF16)
            ref[0, 1, pl.ds(r0, rows), :] = d1.astype(BF16)
        return carry

    lax.fori_loop(0, SEQ // rows, chunk, 0)


def _kvprep(rmat, kvs, wa, wb, pea, peb, g1, g2, g3):
    nrow = SEQ // CMP_STRIDE
    const2 = lambda b: (0, 0)
    small = pl.BlockSpec((1, NSA_GROUPS, nrow, LANES), lambda b: (b, 0, 0, 0))
    big = pl.BlockSpec((1, NSA_GROUPS, SEQ, LANES), lambda b: (b, 0, 0, 0))
    return pl.pallas_call(
        _kvprep_body,
        grid=(BATCH,),
        in_specs=[pl.BlockSpec((1, nrow, rmat.shape[2]), lambda b: (b, 0, 0)),
                  pl.BlockSpec((SEQ, kvs.shape[1]), lambda b: (b, 0)),
                  pl.BlockSpec(wa.shape, const2), pl.BlockSpec(wb.shape, const2),
                  pl.BlockSpec(pea.shape, const2), pl.BlockSpec(peb.shape, const2),
                  pl.BlockSpec((1, LANES), const2), pl.BlockSpec((1, LANES), const2),
                  pl.BlockSpec((1, LANES), const2)],
        out_specs=[small, small, big, big, big, big],
        out_shape=[jax.ShapeDtypeStruct((BATCH, NSA_GROUPS, nrow, LANES), BF16)] * 2
                  + [jax.ShapeDtypeStruct((BATCH, NSA_GROUPS, SEQ, LANES), BF16)] * 4,
        compiler_params=_cparams(("parallel",)),
        name="kv_prep",
    )(rmat, kvs, wa, wb, pea, peb, g1, g2, g3)


def _masked_softmax(s, mask):
    s = jnp.where(mask, s, NEG_INF)
    m = jnp.max(s, axis=-1, keepdims=True)
    e = jnp.where(mask, jnp.exp(s - m), 0.0)
    l = jnp.sum(e, axis=-1, keepdims=True)
    return e / jnp.where(l > 0.0, l, 1.0)


def _split3(x):
    p1 = x.astype(BF16)
    r1 = x - p1.astype(F32)
    p2 = r1.astype(BF16)
    p3 = (r1 - p2.astype(F32)).astype(BF16)
    return p1, p2, p3


def _nsa_body(q_ref, gl_ref, kcd_ref, vcd_ref, ksd_ref, vsd_ref, kwd_ref, vwd_ref,
              g0_ref, ov_ref, e_ref, o_ref, msk_scr):
    g = pl.program_id(1)
    qi = pl.program_id(2)
    q0 = qi * TQ
    lane = lax.broadcasted_iota(jnp.int32, (1, LANES), 1)
    lo = lane < HEAD_DIM
    t_idx = q0 + lax.broadcasted_iota(jnp.int32, (TQ, 1), 0)

    qm = []
    for p in range(GQA_REP // 2):
        qn = _rms_pair(q_ref[:, p * LANES:(p + 1) * LANES], g0_ref[...], lo) * (HEAD_DIM ** -0.5)
        qm.append(jnp.where(lo, qn, 0.0).astype(BF16))
        qm.append(jnp.where(lo, 0.0, qn).astype(BF16))
    slopes = [jnp.where(g == 0, 2.0 ** -(r + 1), 2.0 ** -(r + 1 + GQA_REP)).astype(F32)
              for r in range(GQA_REP)]

    cdist = (t_idx - (lane * CMP_STRIDE + (CMP_LEN - 1))).astype(F32)
    cmask = (cdist >= 0.0) & (lane < N_CMP)
    kc = kcd_ref[0, 0]
    vc = vcd_ref[0, 0]
    psum = jnp.zeros((TQ, LANES), F32)
    o_cmp = []
    for r in range(GQA_REP):
        s = lax.dot_general(qm[r], kc, _NT, preferred_element_type=F32) - slopes[r] * cdist
        p = _masked_softmax(s, cmask)
        psum = psum + p
        o_cmp.append(jnp.dot(p.astype(BF16), vc, preferred_element_type=F32))

    ov = ov_ref[...]
    imp = jnp.zeros((TQ, LANES), F32)
    for part in _split3(psum):
        imp = imp + jnp.dot(part, ov, preferred_element_type=F32)
    tb = t_idx >> 6
    valid = lane <= tb
    forced = (lane == 0) | (lane == tb) | (lane == tb - 1)
    imp = jnp.where(valid, imp + jnp.where(forced, FORCE_BONUS, 0.0), -1.0)
    rank = jnp.zeros((TQ, LANES), F32)
    for i in range(N_SEL):
        ci = imp[:, i:i + 1]
        ahead = (ci > imp) | ((ci == imp) & (lane > i))
        rank = rank + jnp.where(ahead, 1.0, 0.0)
    sel = (rank < float(SEL_TOPN)) & (lane < N_SEL)
    msk_scr[...] = jnp.dot(jnp.where(sel, 1.0, 0.0).astype(BF16), e_ref[...],
                           preferred_element_type=F32)

    col = lax.broadcasted_iota(jnp.int32, (1, TK), 1)

    def kv_step(kb, carry):
        ms, ls, accs = carry
        k0 = pl.multiple_of(kb * TK, TK)
        kblk = ksd_ref[0, 0, pl.ds(k0, TK), :]
        vblk = vsd_ref[0, 0, pl.ds(k0, TK), :]
        dist = t_idx - (k0 + col)
        mask = (msk_scr[:, pl.ds(k0, TK)] > 0.5) & (dist >= 0)
        distf = dist.astype(F32)
        nm, nl, nacc = [], [], []
        for r in range(GQA_REP):
            s = lax.dot_general(qm[r], kblk, _NT, preferred_element_type=F32) - slopes[r] * distf
            s = jnp.where(mask, s, NEG_INF)
            m_new = jnp.maximum(ms[r], jnp.max(s, axis=-1, keepdims=True))
            alpha = jnp.exp(ms[r] - m_new)
            e = jnp.where(mask, jnp.exp(s - m_new), 0.0)
            nl.append(alpha * ls[r] + jnp.sum(e, axis=-1, keepdims=True))
            nacc.append(alpha * accs[r] + jnp.dot(e.astype(BF16), vblk, preferred_element_type=F32))
            nm.append(m_new)
        return tuple(nm), tuple(nl), tuple(nacc)

    init = (tuple(jnp.full((TQ, 1), NEG_INF, F32) for _ in range(GQA_REP)),
            tuple(jnp.zeros((TQ, 1), F32) for _ in range(GQA_REP)),
            tuple(jnp.zeros((TQ, LANES), F32) for _ in range(GQA_REP)))
    _, ls, accs = lax.fori_loop(0, qi + 1, kv_step, init)
    o_slc = [accs[r] / jnp.where(ls[r] > 0.0, ls[r], 1.0) for r in range(GQA_REP)]

    start = pl.multiple_of(jnp.maximum(q0 - WINDOW, 0), TK)
    kw = kwd_ref[0, 0, pl.ds(start, WIN_KEYS), :]
    vw = vwd_ref[0, 0, pl.ds(start, WIN_KEYS), :]
    wd = t_idx - (start + lax.broadcasted_iota(jnp.int32, (1, WIN_KEYS), 1))
    wmask = (wd >= 0) & (wd < WINDOW)
    wdf = wd.astype(F32)
    o_win = []
    for r in range(GQA_REP):
        s = lax.dot_general(qm[r], kw, _NT, preferred_element_type=F32) - slopes[r] * wdf
        p = _masked_softmax(s, wmask)
        o_win.append(jnp.dot(p.astype(BF16), vw, preferred_element_type=F32))

    sg = jax.nn.sigmoid(gl_ref[...])
    outs = []
    for r in range(GQA_REP):
        outs.append(sg[:, 3 * r:3 * r + 1] * o_cmp[r] + sg[:, 3 * r + 1:3 * r + 2] * o_slc[r]
                    + sg[:, 3 * r + 2:3 * r + 3] * o_win[r])
    for p in range(GQA_REP // 2):
        o_ref[:, p * LANES:(p + 1) * LANES] = jnp.where(lo, outs[2 * p], outs[2 * p + 1]).astype(o_ref.dtype)


def _nsa(q, gl, kcd, vcd, ksd, vsd, kwd, vwd, g0, ov, emat):
    nq = SEQ // TQ
    gw = GQA_REP * HEAD_DIM
    nrow = kcd.shape[2]
    tile = lambda b, g, i: (b * nq + i, g)
    per_bg = lambda b, g, i: (b, g, 0, 0)
    const2 = lambda b, g, i: (0, 0)
    return pl.pallas_call(
        _nsa_body,
        grid=(BATCH, NSA_GROUPS, nq),
        in_specs=[pl.BlockSpec((TQ, gw), tile),
                  pl.BlockSpec((TQ, LANES), tile),
                  pl.BlockSpec((1, 1, nrow, LANES), per_bg),
                  pl.BlockSpec((1, 1, nrow, LANES), per_bg),
                  pl.BlockSpec((1, 1, SEQ, LANES), per_bg),
                  pl.BlockSpec((1, 1, SEQ, LANES), per_bg),
                  pl.BlockSpec((1, 1, SEQ, LANES), per_bg),
                  pl.BlockSpec((1, 1, SEQ, LANES), per_bg),
                  pl.BlockSpec((1, LANES), const2),
                  pl.BlockSpec(ov.shape, const2),
                  pl.BlockSpec(emat.shape, const2)],
        out_specs=pl.BlockSpec((TQ, gw), tile),
        out_shape=jax.ShapeDtypeStruct((TOKENS, NSA_WIDTH), BF16),
        scratch_shapes=[pltpu.VMEM((TQ, SEQ), F32)],
        compiler_params=_cparams(("parallel", "parallel", "arbitrary")),
        name="nsa_attention",
    )(q, gl, kcd, vcd, ksd, vsd, kwd, vwd, g0, ov, emat)


def _conv_body(glu_ref, w_ref, b_ref, lg_ref, lb_ref, o_ref, u_scr):
    u_scr[0:CONV_PAD, :] = jnp.zeros((CONV_PAD, CONV_WIDTH), F32)
    rows = 256

    def fill(i, carry):
        r0 = pl.multiple_of(i * rows, rows)
        blk = glu_ref[pl.ds(r0, rows), :]
        u_scr[pl.ds(CONV_PAD + r0, rows), :] = blk[:, :CONV_WIDTH] * jax.nn.sigmoid(blk[:, CONV_WIDTH:])
        return carry

    lax.fori_loop(0, SEQ // rows, fill, 0)
    first = CONV_PAD - (CONV_TAPS - 1)

    def tile(i, carry):
        r0 = pl.multiple_of(i * TS_CONV, TS_CONV)
        win = u_scr[pl.ds(r0, TS_CONV + CONV_PAD), :]
        acc = jnp.zeros((TS_CONV, CONV_WIDTH), F32) + b_ref[...]
        for k in range(CONV_TAPS):
            acc = acc + win[first + k:first + k + TS_CONV, :] * w_ref[k:k + 1, :]
        mu = jnp.mean(acc, axis=-1, keepdims=True)
        d = acc - mu
        var = jnp.mean(d * d, axis=-1, keepdims=True)
        yn = d * lax.rsqrt(var + NORM_EPS) * lg_ref[...] + lb_ref[...]
        o_ref[pl.ds(r0, TS_CONV), :] = (yn * jax.nn.sigmoid(yn)).astype(o_ref.dtype)
        return carry

    lax.fori_loop(0, SEQ // TS_CONV, tile, 0)


def _conv(glu, w, b, lg, lb):
    const2 = lambda i: (0, 0)
    return pl.pallas_call(
        _conv_body,
        grid=(BATCH,),
        in_specs=[pl.BlockSpec((SEQ, 2 * CONV_WIDTH), lambda i: (i, 0)),
                  pl.BlockSpec(w.shape, const2), pl.BlockSpec(b.shape, const2),
                  pl.BlockSpec(lg.shape, const2), pl.BlockSpec(lb.shape, const2)],
        out_specs=pl.BlockSpec((SEQ, CONV_WIDTH), lambda i: (i, 0)),
        out_shape=jax.ShapeDtypeStruct((TOKENS, CONV_WIDTH), BF16),
        scratch_shapes=[pltpu.VMEM((CONV_PAD + SEQ, CONV_WIDTH), F32)],
        compiler_params=_cparams(("parallel",)),
        name="conv_mixer",
    )(glu, w, b, lg, lb)


def _outproj_body(a_ref, c_ref, x_ref, mod_ref, g_ref, wa_ref, wc_ref, x1_ref, h2_ref):
    mix = (jnp.dot(a_ref[...], wa_ref[...], preferred_element_type=F32)
           + jnp.dot(c_ref[...], wc_ref[...], preferred_element_type=F32))
    x1 = x_ref[...] + mod_ref[0, 2:3, :] * mix
    x1_ref[...] = x1
    h2_ref[...] = _norm_mod(x1, g_ref[...], mod_ref[0, 3:4, :], mod_ref[0, 4:5, :]).astype(BF16)


def _outproj(a, c, xf, mod3, g, wa, wc):
    tiles_per_batch = SEQ // TM_PROJ
    row = lambda i: (i, 0)
    const2 = lambda i: (0, 0)
    return pl.pallas_call(
        _outproj_body,
        grid=(TOKENS // TM_PROJ,),
        in_specs=[pl.BlockSpec((TM_PROJ, NSA_WIDTH), row),
                  pl.BlockSpec((TM_PROJ, CONV_WIDTH), row),
                  pl.BlockSpec((TM_PROJ, D_MODEL), row),
                  pl.BlockSpec((1, 6, D_MODEL), lambda i: (i // tiles_per_batch, 0, 0)),
                  pl.BlockSpec((1, D_MODEL), const2),
                  pl.BlockSpec(wa.shape, const2), pl.BlockSpec(wc.shape, const2)],
        out_specs=[pl.BlockSpec((TM_PROJ, D_MODEL), row), pl.BlockSpec((TM_PROJ, D_MODEL), row)],
        out_shape=[jax.ShapeDtypeStruct((TOKENS, D_MODEL), F32),
                   jax.ShapeDtypeStruct((TOKENS, D_MODEL), BF16)],
        compiler_params=_cparams(("parallel",)),
        name="out_proj",
    )(a, c, xf, mod3, g, wa, wc)


def _topk_rows(x, k):
    n, cols = x.shape
    sub = lax.broadcasted_iota(jnp.int32, (n, cols), 0).astype(F32)
    slot = lax.broadcasted_iota(jnp.int32, (k, cols), 0)
    vals = jnp.zeros((k, cols), F32)
    idxs = jnp.zeros((k, cols), F32)
    for i in range(k):
        m = jnp.max(x, axis=0, keepdims=True)
        idx = jnp.min(jnp.where(x == m, sub, IDX_BIG), axis=0, keepdims=True)
        x = jnp.where(sub == idx, NEG_BIG, x)
        vals = jnp.where(slot == i, m, vals)
        idxs = jnp.where(slot == i, idx, idxs)
    return vals, idxs


def _pair_topk(v1, v2):
    k, cols = v1.shape
    sub8 = lax.broadcasted_iota(jnp.int32, (8, cols), 0).astype(F32)
    row = lambda v, a: jnp.broadcast_to(v[a:a + 1, :], (8, cols))
    blocks = [(row(v1, 0) + v2[0:8], sub8),
              (row(v1, 0) + v2[8:16], 8.0 + sub8),
              (row(v1, 1) + v2[0:8], 16.0 + sub8)]
    for a in range(2, 8):
        nb = PEER_TOPK // (a + 1)
        blocks.append((jnp.where(sub8 < float(nb), row(v1, a) + v2[0:8], NEG_BIG), 16.0 * a + sub8))
    blocks.append((v1[8:16] + row(v2, 0), (8.0 + sub8) * 16.0))
    vals = [b[0] for b in blocks]
    flats = [b[1] for b in blocks]
    slot = lax.broadcasted_iota(jnp.int32, (k, cols), 0)
    tops = jnp.zeros((k, cols), F32)
    tflat = jnp.zeros((k, cols), F32)
    for i in range(k):
        m = functools.reduce(jnp.maximum, vals)
        m = jnp.max(m, axis=0, keepdims=True)
        f = functools.reduce(jnp.minimum, [jnp.where(v == m, fl, IDX_BIG) for v, fl in zip(vals, flats)])
        f = jnp.min(f, axis=0, keepdims=True)
        vals = [jnp.where(fl == f, NEG_BIG, v) for v, fl in zip(vals, flats)]
        tops = jnp.where(slot == i, m, tops)
        tflat = jnp.where(slot == i, f, tflat)
    return tops, tflat


def _route_body(h_ref, wq_ref, keys_ref, i1_ref, i2_ref, gw_ref, qp_scr, s1_scr, s2_scr, sw_scr):
    qp_scr[...] = jnp.dot(h_ref[...], wq_ref[...], preferred_element_type=F32)
    k = PEER_TOPK

    def head(h, carry):
        c0 = pl.multiple_of(h * PEER_QDIM, PEER_QDIM)
        tv, ti = [], []
        for c in range(2):
            qs = qp_scr[:, pl.ds(c0 + c * LANES, LANES)].astype(BF16)
            st = lax.dot_general(keys_ref[2 * h + c], qs, _NT, preferred_element_type=F32)
            v, i = _topk_rows(st, k)
            tv.append(v)
            ti.append(i)
        tops, tflat = _pair_topk(tv[0], tv[1])
        a_sel = jnp.floor(tflat * (1.0 / k))
        b_sel = tflat - a_sel * float(k)
        i1s = jnp.zeros_like(tops)
        i2s = jnp.zeros_like(tops)
        for a in range(k):
            i1s = i1s + jnp.where(a_sel == float(a), jnp.broadcast_to(ti[0][a:a + 1, :], tops.shape), 0.0)
            i2s = i2s + jnp.where(b_sel == float(a), jnp.broadcast_to(ti[1][a:a + 1, :], tops.shape), 0.0)
        e = jnp.exp(tops - jnp.max(tops, axis=0, keepdims=True))
        w = e / jnp.sum(e, axis=0, keepdims=True)
        r0 = pl.multiple_of(h * k, k)
        s1_scr[pl.ds(r0, k), :] = i1s
        s2_scr[pl.ds(r0, k), :] = i2s
        sw_scr[pl.ds(r0, k), :] = w
        return carry

    lax.fori_loop(0, PEER_HEADS, head, 0)
    i1_ref[...] = s1_scr[...].T
    i2_ref[...] = s2_scr[...].T
    gw_ref[...] = sw_scr[...].T


def _route(h2, wq, keys):
    nsel = PEER_HEADS * PEER_TOPK
    row = lambda i: (i, 0)
    return pl.pallas_call(
        _route_body,
        grid=(TOKENS // TL_ROUTE,),
        in_specs=[pl.BlockSpec((TL_ROUTE, D_MODEL), row),
                  pl.BlockSpec(wq.shape, lambda i: (0, 0)),
                  pl.BlockSpec(keys.shape, lambda i: (0, 0, 0))],
        out_specs=[pl.BlockSpec((TL_ROUTE, nsel), row)] * 3,
        out_shape=[jax.ShapeDtypeStruct((TOKENS, nsel), F32)] * 3,
        scratch_shapes=[pltpu.VMEM((TL_ROUTE, PEER_HEADS * PEER_QDIM), F32)]
                       + [pltpu.VMEM((nsel, TL_ROUTE), F32)] * 3,
        compiler_params=_cparams(("parallel",)),
        name="peer_route",
    )(h2, wq, keys)


def _peer_body(h_ref, u_ref, v_ref, i1_ref, i2_ref, gw_ref, x1_ref, mod_ref, o_ref, g_scr, acc_scr):
    e = pl.program_id(1)
    n = PEER_NKEYS

    @pl.when(e == 0)
    def _():
        acc_scr[...] = jnp.zeros(acc_scr.shape, F32)
        sub = lax.broadcasted_iota(jnp.int32, (n, n), 0).astype(F32)

        def tok(t, carry):
            w = gw_ref[pl.ds(t, 1), :]
            w_hi = w.astype(BF16).astype(F32)
            m1 = sub == i1_ref[pl.ds(t, 1), :]
            x1 = jnp.concatenate([jnp.where(m1, w_hi, 0.0), jnp.where(m1, w - w_hi, 0.0)], axis=1).astype(BF16)
            x2h = jnp.where(sub == i2_ref[pl.ds(t, 1), :], 1.0, 0.0)
            x2 = jnp.concatenate([x2h, x2h], axis=1).astype(BF16)
            g = lax.dot_general(x1, x2, _NT, preferred_element_type=F32)
            g_scr[pl.ds(pl.multiple_of(t * G_ROW_STRIDE, 8), n), :] = g
            return carry

        lax.fori_loop(0, TT_PEER, tok, 0)

    z = lax.dot_general(h_ref[...], u_ref[...], _NT, preferred_element_type=F32)
    act = 0.5 * z * (1.0 + lax.erf(z * (2.0 ** -0.5)))
    parts = []
    for c in range(ET_PEER // n):
        gc = g_scr[pl.ds(e * (ET_PEER // n) + c, TT_PEER, stride=G_ROW_STRIDE), :]
        parts.append((act[:, c * n:(c + 1) * n] * gc).astype(BF16))
    acc_scr[...] += jnp.dot(jnp.concatenate(parts, axis=1), v_ref[...], preferred_element_type=F32)

    @pl.when(e == pl.num_programs(1) - 1)
    def _():
        o_ref[...] = x1_ref[...] + mod_ref[0, 5:6, :] * acc_scr[...]


def _peer(h2, u, v, i1, i2, gw, x1, mod3):
    tiles_per_batch = SEQ // TT_PEER
    nsel = PEER_HEADS * PEER_TOPK
    tok = lambda i, e: (i, 0)
    exp = lambda i, e: (e, 0)
    return pl.pallas_call(
        _peer_body,
        grid=(TOKENS // TT_PEER, PEER_EXPERTS // ET_PEER),
        in_specs=[pl.BlockSpec((TT_PEER, D_MODEL), tok),
                  pl.BlockSpec((ET_PEER, D_MODEL), exp),
                  pl.BlockSpec((ET_PEER, D_MODEL), exp),
                  pl.BlockSpec((TT_PEER, nsel), tok),
                  pl.BlockSpec((TT_PEER, nsel), tok),
                  pl.BlockSpec((TT_PEER, nsel), tok),
                  pl.BlockSpec((TT_PEER, D_MODEL), tok),
                  pl.BlockSpec((1, 6, D_MODEL), lambda i, e: (i // tiles_per_batch, 0, 0))],
        out_specs=pl.BlockSpec((TT_PEER, D_MODEL), tok),
        out_shape=jax.ShapeDtypeStruct((TOKENS, D_MODEL), F32),
        scratch_shapes=[pltpu.VMEM((TT_PEER * G_ROW_STRIDE, PEER_NKEYS), F32),
                        pltpu.VMEM((TT_PEER, D_MODEL), F32)],
        compiler_params=_cparams(("parallel", "arbitrary")),
        name="peer_experts",
    )(h2, u, v, i1, i2, gw, x1, mod3)


def _overlap_matrix():
    start = np.arange(N_CMP)[:, None] * CMP_STRIDE
    sel = np.arange(N_SEL)[None, :] * SEL_LEN
    ov = np.clip(np.minimum(start + CMP_LEN, sel + SEL_LEN) - np.maximum(start, sel), 0, None) / CMP_LEN
    out = np.zeros((LANES, LANES), np.float32)
    out[:N_CMP, :N_SEL] = ov
    return out


def _block_expand_matrix():
    out = np.zeros((LANES, SEQ), np.float32)
    out[np.arange(SEQ) // SEL_LEN, np.arange(SEQ)] = 1.0
    return out


def _cmp_weights(wk, wv, first):
    width = 2 * NSA_GROUPS * HEAD_DIM
    out = jnp.zeros((CMP_STRIDE, width, width), F32)
    for kind, w in enumerate((wk, wv)):
        for g in range(NSA_GROUPS):
            o = (kind * NSA_GROUPS + g) * HEAD_DIM
            out = out.at[:, o:o + HEAD_DIM, o:o + HEAD_DIM].set(w[first:first + CMP_STRIDE])
    return out.reshape(CMP_STRIDE * width, width).astype(BF16)


def _dup(v):
    return jnp.concatenate([v, v])[None, :]


def kernel(x, c, w_ada, b_ada, norm_g, w_in, w_out, cmp_pe_k, cmp_pe_v, w_cmp_k, w_cmp_v, qk_norm_g,
           dw_w, dw_b, conv_ln_g, conv_ln_b, peer_wq, peer_sub_keys, peer_u, peer_v):
    assert x.shape == (BATCH, SEQ, D_MODEL) and w_ada.shape[0] == DEPTH
    ov = jnp.asarray(_overlap_matrix(), BF16)
    emat = jnp.asarray(_block_expand_matrix(), BF16)
    o_kv = NSA_WIDTH
    o_gate = o_kv + 6 * NSA_GROUPS * HEAD_DIM
    o_glu = o_gate + 3 * NSA_HEADS
    n_cmp_cols = 2 * NSA_GROUPS * HEAD_DIM
    xf = x.reshape(TOKENS, D_MODEL)
    for l in range(DEPTH):
        mod3 = _ada(c, w_ada[l], b_ada[l][None, :]).reshape(BATCH, 6, D_MODEL)
        wi = w_in[l]
        wg = jnp.zeros((D_MODEL, NSA_GROUPS * LANES), F32)
        for g in range(NSA_GROUPS):
            wg = wg.at[:, g * LANES:g * LANES + 3 * GQA_REP].set(
                wi[:, o_gate + 3 * GQA_REP * g:o_gate + 3 * GQA_REP * (g + 1)])
        q, kvc, kvs, gl, glu = _inproj(
            xf, mod3, norm_g[l, 0][None, :],
            wi[:, :o_kv].astype(BF16), wi[:, o_kv:o_kv + n_cmp_cols].astype(BF16),
            wi[:, o_kv + n_cmp_cols:o_gate].astype(BF16), wg.astype(BF16), wi[:, o_glu:].astype(BF16))
        pe = jnp.concatenate([cmp_pe_k[l], cmp_pe_k[l], cmp_pe_v[l], cmp_pe_v[l]], axis=1)
        kcd, vcd, ksd, vsd, kwd, vwd = _kvprep(
            kvc.reshape(BATCH, SEQ // CMP_STRIDE, CMP_STRIDE * n_cmp_cols), kvs,
            _cmp_weights(w_cmp_k[l], w_cmp_v[l], 0), _cmp_weights(w_cmp_k[l], w_cmp_v[l], CMP_STRIDE),
            pe[:CMP_STRIDE].reshape(1, -1), pe[CMP_STRIDE:].reshape(1, -1),
            _dup(qk_norm_g[l, 1]), _dup(qk_norm_g[l, 2]), _dup(qk_norm_g[l, 3]))
        a_out = _nsa(q, gl, kcd, vcd, ksd, vsd, kwd, vwd, _dup(qk_norm_g[l, 0]), ov, emat)
        c_out = _conv(glu, dw_w[l], dw_b[l][None, :], conv_ln_g[l][None, :], conv_ln_b[l][None, :])
        x1, h2 = _outproj(a_out, c_out, xf, mod3, norm_g[l, 1][None, :],
                          w_out[l, :NSA_WIDTH].astype(BF16), w_out[l, NSA_WIDTH:].astype(BF16))
        i1, i2, gw = _route(h2, peer_wq[l].astype(BF16),
                            peer_sub_keys[l].reshape(2 * PEER_HEADS, PEER_NKEYS, PEER_QDIM // 2).astype(BF16))
        xf = _peer(h2, peer_u[l].astype(BF16), peer_v[l].astype(BF16), i1, i2, gw, x1, mod3)
    return xf.reshape(BATCH, SEQ, D_MODEL)
```

```python
import functools

import numpy as np
import jax
import jax.numpy as jnp
from jax import lax
from jax.experimental import pallas as pl
from jax.experimental.pallas import tpu as pltpu

F32 = jnp.float32
BF16 = jnp.bfloat16

D_MODEL = 1024
BATCH = 8
SEQ = 2048
DEPTH = 1
TOKENS = BATCH * SEQ

HEAD_DIM = 64
NSA_HEADS = 8
NSA_GROUPS = 2
GQA_REP = NSA_HEADS // NSA_GROUPS
NSA_WIDTH = NSA_HEADS * HEAD_DIM
CONV_WIDTH = 512
CMP_LEN = 32
CMP_STRIDE = 16
N_CMP = (SEQ - CMP_LEN) // CMP_STRIDE + 1
SEL_LEN = 64
N_SEL = SEQ // SEL_LEN
SEL_TOPN = 16
WINDOW = 512
FORCE_BONUS = 1.0e4
CONV_TAPS = 31
PEER_HEADS = 8
PEER_NKEYS = 128
PEER_EXPERTS = PEER_NKEYS * PEER_NKEYS
PEER_QDIM = 256
PEER_TOPK = 16
NORM_EPS = 1e-6
NEG_INF = -1e30
NEG_BIG = -3.0e38
IDX_BIG = 1.0e9

LANES = 128
VMEM_LIMIT = 48 * 1024 * 1024

TM_PROJ = 512
TQ = 128
TK = 256
WIN_KEYS = WINDOW + TQ
TS_CONV = 64
CONV_PAD = 32
TL_ROUTE = 256
TT_PEER = 256
ET_PEER = 1024
ES_PEER = 256
G_ROW_STRIDE = PEER_NKEYS + 8

_NT = (((1,), (1,)), ((), ()))


def _cparams(sem):
    return pltpu.CompilerParams(dimension_semantics=sem, vmem_limit_bytes=VMEM_LIMIT)


def _ada_body(c_ref, w_ref, b_ref, o_ref):
    c = c_ref[...]
    sc = (c * jax.nn.sigmoid(c)).astype(BF16)
    o_ref[...] = jnp.dot(sc, w_ref[...].astype(BF16), preferred_element_type=F32) + b_ref[...]


def _ada(c, w, b):
    n = w.shape[1]
    tn = 1536
    return pl.pallas_call(
        _ada_body,
        grid=(n // tn,),
        in_specs=[pl.BlockSpec((BATCH, D_MODEL), lambda j: (0, 0)),
                  pl.BlockSpec((D_MODEL, tn), lambda j: (0, j)),
                  pl.BlockSpec((1, tn), lambda j: (0, j))],
        out_specs=pl.BlockSpec((BATCH, tn), lambda j: (0, j)),
        out_shape=jax.ShapeDtypeStruct((BATCH, n), F32),
        compiler_params=_cparams(("arbitrary",)),
        name="ada_mod",
    )(c, w, b)


def _norm_mod(x, g, shift, scale):
    ms = jnp.mean(x * x, axis=-1, keepdims=True)
    y = x * lax.rsqrt(ms + NORM_EPS) * g
    return y * (1.0 + scale) + shift


def _inproj_body(x_ref, mod_ref, g_ref, wq_ref, wkc_ref, wks_ref, wg_ref, wglu_ref,
                 q_ref, kc_ref, ks_ref, gl_ref, glu_ref):
    h = _norm_mod(x_ref[...], g_ref[...], mod_ref[0, 0:1, :], mod_ref[0, 1:2, :]).astype(BF16)
    for w_ref, o_ref in ((wq_ref, q_ref), (wkc_ref, kc_ref), (wks_ref, ks_ref),
                         (wg_ref, gl_ref), (wglu_ref, glu_ref)):
        o_ref[...] = jnp.dot(h, w_ref[...], preferred_element_type=F32)


def _inproj(xf, mod3, g, wq, wkc, wks, wg, wglu):
    tiles_per_batch = SEQ // TM_PROJ
    ws = (wq, wkc, wks, wg, wglu)
    row = lambda i: (i, 0)
    return pl.pallas_call(
        _inproj_body,
        grid=(TOKENS // TM_PROJ,),
        in_specs=[pl.BlockSpec((TM_PROJ, D_MODEL), row),
                  pl.BlockSpec((1, 6, D_MODEL), lambda i: (i // tiles_per_batch, 0, 0)),
                  pl.BlockSpec((1, D_MODEL), lambda i: (0, 0))]
                 + [pl.BlockSpec(w.shape, lambda i: (0, 0)) for w in ws],
        out_specs=[pl.BlockSpec((TM_PROJ, w.shape[1]), row) for w in ws],
        out_shape=[jax.ShapeDtypeStruct((TOKENS, w.shape[1]), F32) for w in ws],
        compiler_params=_cparams(("parallel",)),
        name="in_proj",
    )(xf, mod3, g, *ws)


def _rms_pair(x, gdup, lo):
    x2 = x * x
    s_lo = jnp.sum(jnp.where(lo, x2, 0.0), axis=-1, keepdims=True)
    s_hi = jnp.sum(jnp.where(lo, 0.0, x2), axis=-1, keepdims=True)
    rs = jnp.where(lo, lax.rsqrt(s_lo * (1.0 / HEAD_DIM) + NORM_EPS),
                   lax.rsqrt(s_hi * (1.0 / HEAD_DIM) + NORM_EPS))
    return x * rs * gdup


def _key_ext(kn, lo, lane, pos):
    ext = jnp.where(lane == HEAD_DIM, (pos >> 6).astype(F32),
                    jnp.where(lane == HEAD_DIM + 1, (pos & (SEL_LEN - 1)).astype(F32),
                              jnp.where(lane == HEAD_DIM + 2, 1.0, 0.0)))
    return jnp.where(lo, kn, ext), jnp.where(lo, pltpu.roll(kn, HEAD_DIM, 1), ext)


def _kvprep_body(r_ref, kvs_ref, wa_ref, wb_ref, pea_ref, peb_ref, g1_ref, g2_ref, g3_ref,
                 kce_ref, vct_ref, kse_ref, vst_ref, kwe_ref, vwt_ref):
    lane = lax.broadcasted_iota(jnp.int32, (1, LANES), 1)
    lo = lane < HEAD_DIM
    nrow = SEQ // CMP_STRIDE
    r = r_ref[0]
    a = jnp.dot((r + pea_ref[...]).astype(BF16), wa_ref[...], preferred_element_type=F32)
    b = jnp.dot((r + peb_ref[...]).astype(BF16), wb_ref[...], preferred_element_type=F32)
    c = a + pltpu.roll(b, nrow - 1, 0)
    end = lax.broadcasted_iota(jnp.int32, (nrow, 1), 0) * CMP_STRIDE + (CMP_LEN - 1)
    kc0, kc1 = _key_ext(_rms_pair(c[:, :LANES], g1_ref[...], lo), lo, lane, end)
    kce_ref[0, 0] = kc0.astype(BF16)
    kce_ref[0, 1] = kc1.astype(BF16)
    vct = c[:, LANES:].T.astype(BF16)
    vct_ref[0, 0] = vct[:HEAD_DIM]
    vct_ref[0, 1] = vct[HEAD_DIM:]

    rows = 256

    def chunk(i, carry):
        r0 = pl.multiple_of(i * rows, rows)
        blk = kvs_ref[pl.ds(r0, rows), :]
        pos = r0 + lax.broadcasted_iota(jnp.int32, (rows, 1), 0)
        for k_ref, v_ref, gain, off in ((kse_ref, vst_ref, g2_ref, 0), (kwe_ref, vwt_ref, g3_ref, 2 * LANES)):
            k0, k1 = _key_ext(_rms_pair(blk[:, off:off + LANES], gain[...], lo), lo, lane, pos)
            k_ref[0, 0, pl.ds(r0, rows), :] = k0.astype(BF16)
            k_ref[0, 1, pl.ds(r0, rows), :] = k1.astype(BF16)
            vt = blk[:, off + LANES:off + 2 * LANES].T.astype(BF16)
            v_ref[0, 0, :, pl.ds(r0, rows)] = vt[:HEAD_DIM]
            v_ref[0, 1, :, pl.ds(r0, rows)] = vt[HEAD_DIM:]
        return carry

    lax.fori_loop(0, SEQ // rows, chunk, 0)


def _kvprep(rmat, kvs, wa, wb, pea, peb, g1, g2, g3):
    nrow = SEQ // CMP_STRIDE
    const2 = lambda b: (0, 0)
    per_b = lambda b: (b, 0, 0, 0)
    shapes = [(nrow, LANES), (HEAD_DIM, nrow), (SEQ, LANES), (HEAD_DIM, SEQ), (SEQ, LANES), (HEAD_DIM, SEQ)]
    return pl.pallas_call(
        _kvprep_body,
        grid=(BATCH,),
        in_specs=[pl.BlockSpec((1, nrow, rmat.shape[2]), lambda b: (b, 0, 0)),
                  pl.BlockSpec((SEQ, kvs.shape[1]), lambda b: (b, 0)),
                  pl.BlockSpec(wa.shape, const2), pl.BlockSpec(wb.shape, const2),
                  pl.BlockSpec(pea.shape, const2), pl.BlockSpec(peb.shape, const2),
                  pl.BlockSpec((1, LANES), const2), pl.BlockSpec((1, LANES), const2),
                  pl.BlockSpec((1, LANES), const2)],
        out_specs=[pl.BlockSpec((1, NSA_GROUPS) + s, per_b) for s in shapes],
        out_shape=[jax.ShapeDtypeStruct((BATCH, NSA_GROUPS) + s, BF16) for s in shapes],
        compiler_params=_cparams(("parallel",)),
        name="kv_prep",
    )(rmat, kvs, wa, wb, pea, peb, g1, g2, g3)


def _split3(x):
    p1 = x.astype(BF16)
    r1 = x - p1.astype(F32)
    p2 = r1.astype(BF16)
    p3 = (r1 - p2.astype(F32)).astype(BF16)
    return p1, p2, p3


def _nsa_body(q_ref, gl_ref, kce_ref, vct_ref, kse_ref, vst_ref, kwe_ref, vwt_ref,
              g0_ref, ovt_ref, et_ref, o_ref, msk_scr):
    g = pl.program_id(1)
    qi = pl.program_id(2)
    q0 = qi * TQ
    lane = lax.broadcasted_iota(jnp.int32, (1, LANES), 1)
    lo = lane < HEAD_DIM
    t_idx = q0 + lax.broadcasted_iota(jnp.int32, (1, TQ), 1)
    q0f = q0.astype(F32)

    qe = []
    for p in range(GQA_REP // 2):
        qn = _rms_pair(q_ref[:, p * LANES:(p + 1) * LANES], g0_ref[...], lo) * (HEAD_DIM ** -0.5)
        for half, base in ((0, qn), (1, pltpu.roll(qn, HEAD_DIM, 1))):
            r = 2 * p + half
            slope = jnp.where(g == 0, 2.0 ** -(r + 1), 2.0 ** -(r + 1 + GQA_REP)).astype(F32)
            ext = jnp.where(lane == HEAD_DIM, SEL_LEN * slope,
                            jnp.where(lane == HEAD_DIM + 1, slope,
                                      jnp.where(lane == HEAD_DIM + 2, -slope * q0f, 0.0)))
            qe.append(jnp.where(lo, base, ext).astype(BF16))

    n_sub = lax.broadcasted_iota(jnp.int32, (SEQ // CMP_STRIDE, 1), 0)
    cmask = (t_idx >= n_sub * CMP_STRIDE + (CMP_LEN - 1)) & (n_sub < N_CMP)
    kce = kce_ref[0, 0]
    vct = vct_ref[0, 0]
    psum = jnp.zeros((SEQ // CMP_STRIDE, TQ), F32)
    o_cmp = []
    for r in range(GQA_REP):
        s = jnp.where(cmask, lax.dot_general(kce, qe[r], _NT, preferred_element_type=F32), NEG_INF)
        e = jnp.where(cmask, jnp.exp(s - jnp.max(s, axis=0, keepdims=True)), 0.0)
        l = jnp.sum(e, axis=0, keepdims=True)
        p = e / jnp.where(l > 0.0, l, 1.0)
        psum = psum + p
        o_cmp.append(jnp.dot(vct, p.astype(BF16), preferred_element_type=F32))

    imp = jnp.zeros((LANES, TQ), F32)
    for part in _split3(psum):
        imp = imp + jnp.dot(ovt_ref[...], part, preferred_element_type=F32)
    imp = imp[:N_SEL]
    j = lax.broadcasted_iota(jnp.int32, (N_SEL, 1), 0)
    tb = t_idx >> 6
    forced = (j == 0) | (j == tb) | (j == tb - 1)
    imp = jnp.where(j <= tb, imp + jnp.where(forced, FORCE_BONUS, 0.0), -1.0)
    rank = jnp.zeros((N_SEL, TQ), F32)
    for i in range(N_SEL):
        ci = imp[i:i + 1, :]
        ahead = (ci > imp) | ((ci == imp) & (j > i))
        rank = rank + jnp.where(ahead, 1.0, 0.0)
    sel = jnp.concatenate([jnp.where(rank < float(SEL_TOPN), 1.0, 0.0),
                           jnp.zeros((LANES - N_SEL, TQ), F32)], axis=0).astype(BF16)
    msk_scr[...] = jnp.dot(et_ref[...], sel, preferred_element_type=F32)

    key_sub = lax.broadcasted_iota(jnp.int32, (TK, 1), 0)

    def kv_step(kb, carry):
        ms, ls, accs = carry
        k0 = pl.multiple_of(kb * TK, TK)
        kblk = kse_ref[0, 0, pl.ds(k0, TK), :]
        vt = vst_ref[0, 0, :, pl.ds(k0, TK)]
        allowed = (msk_scr[pl.ds(k0, TK), :] > 0.5) & (k0 + key_sub <= t_idx)
        bias = jnp.where(allowed, 0.0, NEG_INF)
        nm, nl, nacc = [], [], []
        for r in range(GQA_REP):
            s = lax.dot_general(kblk, qe[r], _NT, preferred_element_type=F32) + bias
            m_new = jnp.maximum(ms[r], jnp.max(s, axis=0, keepdims=True))
            alpha = jnp.exp(ms[r] - m_new)
            e = jnp.exp(s - m_new)
            nl.append(alpha * ls[r] + jnp.sum(e, axis=0, keepdims=True))
            nacc.append(alpha * accs[r] + jnp.dot(vt, e.astype(BF16), preferred_element_type=F32))
            nm.append(m_new)
        return tuple(nm), tuple(nl), tuple(nacc)

    init = (tuple(jnp.full((1, TQ), NEG_INF, F32) for _ in range(GQA_REP)),
            tuple(jnp.zeros((1, TQ), F32) for _ in range(GQA_REP)),
            tuple(jnp.zeros((HEAD_DIM, TQ), F32) for _ in range(GQA_REP)))
    _, ls, accs = lax.fori_loop(0, (q0 + TQ - 1) // TK + 1, kv_step, init)
    o_slc = [accs[r] / ls[r] for r in range(GQA_REP)]

    start = pl.multiple_of(jnp.maximum(q0 - WINDOW, 0), LANES)
    kw = kwe_ref[0, 0, pl.ds(start, WIN_KEYS), :]
    vt = vwt_ref[0, 0, :, pl.ds(start, WIN_KEYS)]
    wd = t_idx - (start + lax.broadcasted_iota(jnp.int32, (WIN_KEYS, 1), 0))
    wbias = jnp.where((wd >= 0) & (wd < WINDOW), 0.0, NEG_INF)
    o_win = []
    for r in range(GQA_REP):
        s = lax.dot_general(kw, qe[r], _NT, preferred_element_type=F32) + wbias
        e = jnp.exp(s - jnp.max(s, axis=0, keepdims=True))
        l = jnp.sum(e, axis=0, keepdims=True)
        o_win.append(jnp.dot(vt, e.astype(BF16), preferred_element_type=F32) / l)

    sg = jax.nn.sigmoid(gl_ref[...].T[:4 * GQA_REP])
    for p in range(GQA_REP // 2):
        pair = []
        for r in (2 * p, 2 * p + 1):
            pair.append(sg[3 * r:3 * r + 1] * o_cmp[r] + sg[3 * r + 1:3 * r + 2] * o_slc[r]
                        + sg[3 * r + 2:3 * r + 3] * o_win[r])
        o_ref[:, p * LANES:(p + 1) * LANES] = jnp.concatenate(pair, axis=0).T.astype(o_ref.dtype)


def _nsa(q, gl, kce, vct, kse, vst, kwe, vwt, g0, ovt, et):
    nq = SEQ // TQ
    gw = GQA_REP * HEAD_DIM
    tile = lambda b, g, i: (b * nq + i, g)
    per_bg = lambda b, g, i: (b, g, 0, 0)
    const2 = lambda b, g, i: (0, 0)
    return pl.pallas_call(
        _nsa_body,
        grid=(BATCH, NSA_GROUPS, nq),
        in_specs=[pl.BlockSpec((TQ, gw), tile), pl.BlockSpec((TQ, LANES), tile)]
                 + [pl.BlockSpec((1, 1) + a.shape[2:], per_bg) for a in (kce, vct, kse, vst, kwe, vwt)]
                 + [pl.BlockSpec((1, LANES), const2), pl.BlockSpec(ovt.shape, const2),
                    pl.BlockSpec(et.shape, const2)],
        out_specs=pl.BlockSpec((TQ, gw), tile),
        out_shape=jax.ShapeDtypeStruct((TOKENS, NSA_WIDTH), BF16),
        scratch_shapes=[pltpu.VMEM((SEQ, TQ), F32)],
        compiler_params=_cparams(("parallel", "parallel", "arbitrary")),
        name="nsa_attention",
    )(q, gl, kce, vct, kse, vst, kwe, vwt, g0, ovt, et)


def _conv_body(glu_ref, w_ref, b_ref, lg_ref, lb_ref, o_ref, u_scr):
    u_scr[0:CONV_PAD, :] = jnp.zeros((CONV_PAD, CONV_WIDTH), F32)
    rows = 256

    def fill(i, carry):
        r0 = pl.multiple_of(i * rows, rows)
        blk = glu_ref[pl.ds(r0, rows), :]
        u_scr[pl.ds(CONV_PAD + r0, rows), :] = blk[:, :CONV_WIDTH] * jax.nn.sigmoid(blk[:, CONV_WIDTH:])
        return carry

    lax.fori_loop(0, SEQ // rows, fill, 0)
    first = CONV_PAD - (CONV_TAPS - 1)

    def tile(i, carry):
        r0 = pl.multiple_of(i * TS_CONV, TS_CONV)
        win = u_scr[pl.ds(r0, TS_CONV + CONV_PAD), :]
        acc = jnp.zeros((TS_CONV, CONV_WIDTH), F32) + b_ref[...]
        for k in range(CONV_TAPS):
            acc = acc + win[first + k:first + k + TS_CONV, :] * w_ref[k:k + 1, :]
        mu = jnp.mean(acc, axis=-1, keepdims=True)
        d = acc - mu
        var = jnp.mean(d * d, axis=-1, keepdims=True)
        yn = d * lax.rsqrt(var + NORM_EPS) * lg_ref[...] + lb_ref[...]
        o_ref[pl.ds(r0, TS_CONV), :] = (yn * jax.nn.sigmoid(yn)).astype(o_ref.dtype)
        return carry

    lax.fori_loop(0, SEQ // TS_CONV, tile, 0)


def _conv(glu, w, b, lg, lb):
    const2 = lambda i: (0, 0)
    return pl.pallas_call(
        _conv_body,
        grid=(BATCH,),
        in_specs=[pl.BlockSpec((SEQ, 2 * CONV_WIDTH), lambda i: (i, 0)),
                  pl.BlockSpec(w.shape, const2), pl.BlockSpec(b.shape, const2),
                  pl.BlockSpec(lg.shape, const2), pl.BlockSpec(lb.shape, const2)],
        out_specs=pl.BlockSpec((SEQ, CONV_WIDTH), lambda i: (i, 0)),
        out_shape=jax.ShapeDtypeStruct((TOKENS, CONV_WIDTH), BF16),
        scratch_shapes=[pltpu.VMEM((CONV_PAD + SEQ, CONV_WIDTH), F32)],
        compiler_params=_cparams(("parallel",)),
        name="conv_mixer",
    )(glu, w, b, lg, lb)


def _outproj_body(a_ref, c_ref, x_ref, mod_ref, g_ref, wa_ref, wc_ref, x1_ref, h2_ref):
    mix = (jnp.dot(a_ref[...], wa_ref[...], preferred_element_type=F32)
           + jnp.dot(c_ref[...], wc_ref[...], preferred_element_type=F32))
    x1 = x_ref[...] + mod_ref[0, 2:3, :] * mix
    x1_ref[...] = x1
    h2_ref[...] = _norm_mod(x1, g_ref[...], mod_ref[0, 3:4, :], mod_ref[0, 4:5, :]).astype(BF16)


def _outproj(a, c, xf, mod3, g, wa, wc):
    tiles_per_batch = SEQ // TM_PROJ
    row = lambda i: (i, 0)
    const2 = lambda i: (0, 0)
    return pl.pallas_call(
        _outproj_body,
        grid=(TOKENS // TM_PROJ,),
        in_specs=[pl.BlockSpec((TM_PROJ, NSA_WIDTH), row),
                  pl.BlockSpec((TM_PROJ, CONV_WIDTH), row),
                  pl.BlockSpec((TM_PROJ, D_MODEL), row),
                  pl.BlockSpec((1, 6, D_MODEL), lambda i: (i // tiles_per_batch, 0, 0)),
                  pl.BlockSpec((1, D_MODEL), const2),
                  pl.BlockSpec(wa.shape, const2), pl.BlockSpec(wc.shape, const2)],
        out_specs=[pl.BlockSpec((TM_PROJ, D_MODEL), row), pl.BlockSpec((TM_PROJ, D_MODEL), row)],
        out_shape=[jax.ShapeDtypeStruct((TOKENS, D_MODEL), F32),
                   jax.ShapeDtypeStruct((TOKENS, D_MODEL), BF16)],
        compiler_params=_cparams(("parallel",)),
        name="out_proj",
    )(a, c, xf, mod3, g, wa, wc)


def _topk_rows(x, k):
    n, cols = x.shape
    sub = lax.broadcasted_iota(jnp.int32, (n, cols), 0).astype(F32)
    slot = lax.broadcasted_iota(jnp.int32, (k, cols), 0)
    vals = jnp.zeros((k, cols), F32)
    idxs = jnp.zeros((k, cols), F32)
    for i in range(k):
        m = jnp.max(x, axis=0, keepdims=True)
        idx = jnp.min(jnp.where(x == m, sub, IDX_BIG), axis=0, keepdims=True)
        x = jnp.where(sub == idx, NEG_BIG, x)
        vals = jnp.where(slot == i, m, vals)
        idxs = jnp.where(slot == i, idx, idxs)
    return vals, idxs


def _pair_topk(v1, v2):
    k, cols = v1.shape
    sub8 = lax.broadcasted_iota(jnp.int32, (8, cols), 0).astype(F32)
    row = lambda v, a: jnp.broadcast_to(v[a:a + 1, :], (8, cols))
    blocks = [(row(v1, 0) + v2[0:8], sub8),
              (row(v1, 0) + v2[8:16], 8.0 + sub8),
              (row(v1, 1) + v2[0:8], 16.0 + sub8)]
    for a in range(2, 8):
        nb = PEER_TOPK // (a + 1)
        blocks.append((jnp.where(sub8 < float(nb), row(v1, a) + v2[0:8], NEG_BIG), 16.0 * a + sub8))
    blocks.append((v1[8:16] + row(v2, 0), (8.0 + sub8) * 16.0))
    vals = [b[0] for b in blocks]
    flats = [b[1] for b in blocks]
    slot = lax.broadcasted_iota(jnp.int32, (k, cols), 0)
    tops = jnp.zeros((k, cols), F32)
    tflat = jnp.zeros((k, cols), F32)
    for i in range(k):
        m = functools.reduce(jnp.maximum, vals)
        m = jnp.max(m, axis=0, keepdims=True)
        f = functools.reduce(jnp.minimum, [jnp.where(v == m, fl, IDX_BIG) for v, fl in zip(vals, flats)])
        f = jnp.min(f, axis=0, keepdims=True)
        vals = [jnp.where(fl == f, NEG_BIG, v) for v, fl in zip(vals, flats)]
        tops = jnp.where(slot == i, m, tops)
        tflat = jnp.where(slot == i, f, tflat)
    return tops, tflat


def _route_body(h_ref, wq_ref, keys_ref, i1_ref, i2_ref, gw_ref, qp_scr, s1_scr, s2_scr, sw_scr):
    qp_scr[...] = jnp.dot(h_ref[...], wq_ref[...], preferred_element_type=F32)
    k = PEER_TOPK

    def head(h, carry):
        c0 = pl.multiple_of(h * PEER_QDIM, PEER_QDIM)
        tv, ti = [], []
        for c in range(2):
            qs = qp_scr[:, pl.ds(c0 + c * LANES, LANES)].astype(BF16)
            st = lax.dot_general(keys_ref[2 * h + c], qs, _NT, preferred_element_type=F32)
            v, i = _topk_rows(st, k)
            tv.append(v)
            ti.append(i)
        tops, tflat = _pair_topk(tv[0], tv[1])
        a_sel = jnp.floor(tflat * (1.0 / k))
        b_sel = tflat - a_sel * float(k)
        i1s = jnp.zeros_like(tops)
        i2s = jnp.zeros_like(tops)
        for a in range(k):
            i1s = i1s + jnp.where(a_sel == float(a), jnp.broadcast_to(ti[0][a:a + 1, :], tops.shape), 0.0)
            i2s = i2s + jnp.where(b_sel == float(a), jnp.broadcast_to(ti[1][a:a + 1, :], tops.shape), 0.0)
        e = jnp.exp(tops - jnp.max(tops, axis=0, keepdims=True))
        w = e / jnp.sum(e, axis=0, keepdims=True)
        r0 = pl.multiple_of(h * k, k)
        s1_scr[pl.ds(r0, k), :] = i1s
        s2_scr[pl.ds(r0, k), :] = i2s
        sw_scr[pl.ds(r0, k), :] = w
        return carry

    lax.fori_loop(0, PEER_HEADS, head, 0)
    i1_ref[...] = s1_scr[...].T
    i2_ref[...] = s2_scr[...].T
    gw_ref[...] = sw_scr[...].T


def _route(h2, wq, keys):
    nsel = PEER_HEADS * PEER_TOPK
    row = lambda i: (i, 0)
    return pl.pallas_call(
        _route_body,
        grid=(TOKENS // TL_ROUTE,),
        in_specs=[pl.BlockSpec((TL_ROUTE, D_MODEL), row),
                  pl.BlockSpec(wq.shape, lambda i: (0, 0)),
                  pl.BlockSpec(keys.shape, lambda i: (0, 0, 0))],
        out_specs=[pl.BlockSpec((TL_ROUTE, nsel), row)] * 3,
        out_shape=[jax.ShapeDtypeStruct((TOKENS, nsel), F32)] * 3,
        scratch_shapes=[pltpu.VMEM((TL_ROUTE, PEER_HEADS * PEER_QDIM), F32)]
                       + [pltpu.VMEM((nsel, TL_ROUTE), F32)] * 3,
        compiler_params=_cparams(("parallel",)),
        name="peer_route",
    )(h2, wq, keys)


def _peer_body(h_ref, u_ref, v_ref, i1_ref, i2_ref, gw_ref, x1_ref, mod_ref, o_ref, g_scr):
    e = pl.program_id(1)
    n = PEER_NKEYS

    @pl.when(e == 0)
    def _():
        o_ref[...] = jnp.zeros(o_ref.shape, F32)
        sub = lax.broadcasted_iota(jnp.int32, (n, n), 0).astype(F32)

        def tok(t, carry):
            w = gw_ref[pl.ds(t, 1), :]
            w_hi = w.astype(BF16).astype(F32)
            m1 = sub == i1_ref[pl.ds(t, 1), :]
            x1 = jnp.concatenate([jnp.where(m1, w_hi, 0.0), jnp.where(m1, w - w_hi, 0.0)], axis=1).astype(BF16)
            x2h = jnp.where(sub == i2_ref[pl.ds(t, 1), :], 1.0, 0.0)
            x2 = jnp.concatenate([x2h, x2h], axis=1).astype(BF16)
            g = lax.dot_general(x1, x2, _NT, preferred_element_type=F32)
            g_scr[pl.ds(pl.multiple_of(t * G_ROW_STRIDE, 8), n), :] = g
            return carry

        lax.fori_loop(0, TT_PEER, tok, 0, unroll=4)

    h = h_ref[...]
    parts = []
    for s in range(ET_PEER // ES_PEER):
        z = lax.dot_general(h, u_ref[s * ES_PEER:(s + 1) * ES_PEER, :], _NT,
                            preferred_element_type=F32)
        act = 0.5 * z * (1.0 + lax.erf(z * (2.0 ** -0.5)))
        for c in range(ES_PEER // n):
            i1 = e * (ET_PEER // n) + s * (ES_PEER // n) + c
            gc = g_scr[pl.ds(i1, TT_PEER, stride=G_ROW_STRIDE), :]
            parts.append((act[:, c * n:(c + 1) * n] * gc).astype(BF16))
    o_ref[...] += jnp.dot(jnp.concatenate(parts, axis=1), v_ref[...], preferred_element_type=F32)

    @pl.when(e == pl.num_programs(1) - 1)
    def _():
        o_ref[...] = x1_ref[...] + mod_ref[0, 5:6, :] * o_ref[...]


def _peer(h2, u, v, i1, i2, gw, x1, mod3):
    tiles_per_batch = SEQ // TT_PEER
    nsel = PEER_HEADS * PEER_TOPK
    tok = lambda i, e: (i, 0)
    exp = lambda i, e: (e, 0)
    return pl.pallas_call(
        _peer_body,
        grid=(TOKENS // TT_PEER, PEER_EXPERTS // ET_PEER),
        in_specs=[pl.BlockSpec((TT_PEER, D_MODEL), tok),
                  pl.BlockSpec((ET_PEER, D_MODEL), exp),
                  pl.BlockSpec((ET_PEER, D_MODEL), exp),
                  pl.BlockSpec((TT_PEER, nsel), tok),
                  pl.BlockSpec((TT_PEER, nsel), tok),
                  pl.BlockSpec((TT_PEER, nsel), tok),
                  pl.BlockSpec((TT_PEER, D_MODEL), tok),
                  pl.BlockSpec((1, 6, D_MODEL), lambda i, e: (i // tiles_per_batch, 0, 0))],
        out_specs=pl.BlockSpec((TT_PEER, D_MODEL), tok),
        out_shape=jax.ShapeDtypeStruct((TOKENS, D_MODEL), F32),
        scratch_shapes=[pltpu.VMEM((TT_PEER * G_ROW_STRIDE, PEER_NKEYS), F32)],
        compiler_params=_cparams(("parallel", "arbitrary")),
        name="peer_experts",
    )(h2, u, v, i1, i2, gw, x1, mod3)


def _overlap_matrix():
    start = np.arange(N_CMP)[None, :] * CMP_STRIDE
    sel = np.arange(N_SEL)[:, None] * SEL_LEN
    ov = np.clip(np.minimum(start + CMP_LEN, sel + SEL_LEN) - np.maximum(start, sel), 0, None) / CMP_LEN
    out = np.zeros((LANES, LANES), np.float32)
    out[:N_SEL, :N_CMP] = ov
    return out


def _block_expand_matrix():
    out = np.zeros((SEQ, LANES), np.float32)
    out[np.arange(SEQ), np.arange(SEQ) // SEL_LEN] = 1.0
    return out


def _cmp_weights(wk, wv, first):
    width = 2 * NSA_GROUPS * HEAD_DIM
    out = jnp.zeros((CMP_STRIDE, width, width), F32)
    for kind, w in enumerate((wk, wv)):
        for g in range(NSA_GROUPS):
            o = (kind * NSA_GROUPS + g) * HEAD_DIM
            out = out.at[:, o:o + HEAD_DIM, o:o + HEAD_DIM].set(w[first:first + CMP_STRIDE])
    return out.reshape(CMP_STRIDE * width, width).astype(BF16)


def _dup(v):
    return jnp.concatenate([v, v])[None, :]


def kernel(x, c, w_ada, b_ada, norm_g, w_in, w_out, cmp_pe_k, cmp_pe_v, w_cmp_k, w_cmp_v, qk_norm_g,
           dw_w, dw_b, conv_ln_g, conv_ln_b, peer_wq, peer_sub_keys, peer_u, peer_v):
    assert x.shape == (BATCH, SEQ, D_MODEL) and w_ada.shape[0] == DEPTH
    ovt = jnp.asarray(_overlap_matrix(), BF16)
    emat = jnp.asarray(_block_expand_matrix(), BF16)
    o_kv = NSA_WIDTH
    o_gate = o_kv + 6 * NSA_GROUPS * HEAD_DIM
    o_glu = o_gate + 3 * NSA_HEADS
    n_cmp_cols = 2 * NSA_GROUPS * HEAD_DIM
    xf = x.reshape(TOKENS, D_MODEL)
    for l in range(DEPTH):
        mod3 = _ada(c, w_ada[l], b_ada[l][None, :]).reshape(BATCH, 6, D_MODEL)
        wi = w_in[l]
        wg = jnp.zeros((D_MODEL, NSA_GROUPS * LANES), F32)
        for g in range(NSA_GROUPS):
            wg = wg.at[:, g * LANES:g * LANES + 3 * GQA_REP].set(
                wi[:, o_gate + 3 * GQA_REP * g:o_gate + 3 * GQA_REP * (g + 1)])
        q, kvc, kvs, gl, glu = _inproj(
            xf, mod3, norm_g[l, 0][None, :],
            wi[:, :o_kv].astype(BF16), wi[:, o_kv:o_kv + n_cmp_cols].astype(BF16),
            wi[:, o_kv + n_cmp_cols:o_gate].astype(BF16), wg.astype(BF16), wi[:, o_glu:].astype(BF16))
        pe = jnp.concatenate([cmp_pe_k[l], cmp_pe_k[l], cmp_pe_v[l], cmp_pe_v[l]], axis=1)
        kv_parts = _kvprep(
            kvc.reshape(BATCH, SEQ // CMP_STRIDE, CMP_STRIDE * n_cmp_cols), kvs,
            _cmp_weights(w_cmp_k[l], w_cmp_v[l], 0), _cmp_weights(w_cmp_k[l], w_cmp_v[l], CMP_STRIDE),
            pe[:CMP_STRIDE].reshape(1, -1), pe[CMP_STRIDE:].reshape(1, -1),
            _dup(qk_norm_g[l, 1]), _dup(qk_norm_g[l, 2]), _dup(qk_norm_g[l, 3]))
        a_out = _nsa(q, gl, *kv_parts, _dup(qk_norm_g[l, 0]), ovt, emat)
        c_out = _conv(glu, dw_w[l], dw_b[l][None, :], conv_ln_g[l][None, :], conv_ln_b[l][None, :])
        x1, h2 = _outproj(a_out, c_out, xf, mod3, norm_g[l, 1][None, :],
                          w_out[l, :NSA_WIDTH].astype(BF16), w_out[l, NSA_WIDTH:].astype(BF16))
        i1, i2, gw = _route(h2, peer_wq[l].astype(BF16),
                            peer_sub_keys[l].reshape(2 * PEER_HEADS, PEER_NKEYS, PEER_QDIM // 2).astype(BF16))
        xf = _peer(h2, peer_u[l].astype(BF16), peer_v[l].astype(BF16), i1, i2, gw, x1, mod3)
    return xf.reshape(BATCH, SEQ, D_MODEL)
```

```python
import functools

import numpy as np
import jax
import jax.numpy as jnp
from jax import lax
from jax.experimental import pallas as pl
from jax.experimental.pallas import tpu as pltpu

F32 = jnp.float32
BF16 = jnp.bfloat16

D_MODEL = 1024
BATCH = 8
SEQ = 2048
DEPTH = 1
TOKENS = BATCH * SEQ

HEAD_DIM = 64
NSA_HEADS = 8
NSA_GROUPS = 2
GQA_REP = NSA_HEADS // NSA_GROUPS
NSA_WIDTH = NSA_HEADS * HEAD_DIM
CONV_WIDTH = 512
CMP_LEN = 32
CMP_STRIDE = 16
N_CMP = (SEQ - CMP_LEN) // CMP_STRIDE + 1
SEL_LEN = 64
N_SEL = SEQ // SEL_LEN
SEL_TOPN = 16
WINDOW = 512
FORCE_BONUS = 1.0e4
CONV_TAPS = 31
PEER_HEADS = 8
PEER_NKEYS = 128
PEER_EXPERTS = PEER_NKEYS * PEER_NKEYS
PEER_QDIM = 256
PEER_TOPK = 16
NORM_EPS = 1e-6
NEG_INF = -1e30
NEG_BIG = -3.0e38
IDX_BIG = 1.0e9

LANES = 128
VMEM_LIMIT = 48 * 1024 * 1024

TM_PROJ = 512
TQ = 256
TK = 256
WIN_KEYS = WINDOW + TQ
TS_CONV = 64
CONV_PAD = 32
TL_ROUTE = 256
TT_PEER = 256
ET_PEER = 2048
ES_PEER = 256
G_ROW_STRIDE = PEER_NKEYS + 8

_NT = (((1,), (1,)), ((), ()))


def _cparams(sem):
    return pltpu.CompilerParams(dimension_semantics=sem, vmem_limit_bytes=VMEM_LIMIT)


def _ada_body(c_ref, w_ref, b_ref, o_ref):
    c = c_ref[...]
    sc = (c * jax.nn.sigmoid(c)).astype(BF16)
    o_ref[...] = jnp.dot(sc, w_ref[...].astype(BF16), preferred_element_type=F32) + b_ref[...]


def _ada(c, w, b):
    n = w.shape[1]
    tn = 1536
    return pl.pallas_call(
        _ada_body,
        grid=(n // tn,),
        in_specs=[pl.BlockSpec((BATCH, D_MODEL), lambda j: (0, 0)),
                  pl.BlockSpec((D_MODEL, tn), lambda j: (0, j)),
                  pl.BlockSpec((1, tn), lambda j: (0, j))],
        out_specs=pl.BlockSpec((BATCH, tn), lambda j: (0, j)),
        out_shape=jax.ShapeDtypeStruct((BATCH, n), F32),
        compiler_params=_cparams(("arbitrary",)),
        name="ada_mod",
    )(c, w, b)


def _norm_mod(x, g, shift, scale):
    ms = jnp.mean(x * x, axis=-1, keepdims=True)
    y = x * lax.rsqrt(ms + NORM_EPS) * g
    return y * (1.0 + scale) + shift


def _inproj_body(x_ref, mod_ref, g_ref, wq_ref, wkc_ref, wks_ref, wg_ref, wglu_ref,
                 q_ref, kc_ref, ks_ref, gl_ref, glu_ref):
    h = _norm_mod(x_ref[...], g_ref[...], mod_ref[0, 0:1, :], mod_ref[0, 1:2, :]).astype(BF16)
    for w_ref, o_ref in ((wq_ref, q_ref), (wkc_ref, kc_ref), (wks_ref, ks_ref),
                         (wg_ref, gl_ref), (wglu_ref, glu_ref)):
        o_ref[...] = jnp.dot(h, w_ref[...], preferred_element_type=F32)


def _inproj(xf, mod3, g, wq, wkc, wks, wg, wglu):
    tiles_per_batch = SEQ // TM_PROJ
    ws = (wq, wkc, wks, wg, wglu)
    row = lambda i: (i, 0)
    return pl.pallas_call(
        _inproj_body,
        grid=(TOKENS // TM_PROJ,),
        in_specs=[pl.BlockSpec((TM_PROJ, D_MODEL), row),
                  pl.BlockSpec((1, 6, D_MODEL), lambda i: (i // tiles_per_batch, 0, 0)),
                  pl.BlockSpec((1, D_MODEL), lambda i: (0, 0))]
                 + [pl.BlockSpec(w.shape, lambda i: (0, 0)) for w in ws],
        out_specs=[pl.BlockSpec((TM_PROJ, w.shape[1]), row) for w in ws],
        out_shape=[jax.ShapeDtypeStruct((TOKENS, w.shape[1]), F32) for w in ws],
        compiler_params=_cparams(("parallel",)),
        name="in_proj",
    )(xf, mod3, g, *ws)


def _rms_pair(x, gdup, lo):
    x2 = x * x
    s_lo = jnp.sum(jnp.where(lo, x2, 0.0), axis=-1, keepdims=True)
    s_hi = jnp.sum(jnp.where(lo, 0.0, x2), axis=-1, keepdims=True)
    rs = jnp.where(lo, lax.rsqrt(s_lo * (1.0 / HEAD_DIM) + NORM_EPS),
                   lax.rsqrt(s_hi * (1.0 / HEAD_DIM) + NORM_EPS))
    return x * rs * gdup


def _key_ext(kn, lo, lane, pos):
    ext = jnp.where(lane == HEAD_DIM, (pos >> 6).astype(F32),
                    jnp.where(lane == HEAD_DIM + 1, (pos & (SEL_LEN - 1)).astype(F32),
                              jnp.where(lane == HEAD_DIM + 2, 1.0, 0.0)))
    return jnp.where(lo, kn, ext), jnp.where(lo, pltpu.roll(kn, HEAD_DIM, 1), ext)


def _kvprep_body(r_ref, kvs_ref, wa_ref, wb_ref, pea_ref, peb_ref, g1_ref, g2_ref, g3_ref,
                 kce_ref, vct_ref, kse_ref, vst_ref, kwe_ref, vwt_ref):
    lane = lax.broadcasted_iota(jnp.int32, (1, LANES), 1)
    lo = lane < HEAD_DIM
    nrow = SEQ // CMP_STRIDE
    r = r_ref[0]
    a = jnp.dot((r + pea_ref[...]).astype(BF16), wa_ref[...], preferred_element_type=F32)
    b = jnp.dot((r + peb_ref[...]).astype(BF16), wb_ref[...], preferred_element_type=F32)
    c = a + pltpu.roll(b, nrow - 1, 0)
    end = lax.broadcasted_iota(jnp.int32, (nrow, 1), 0) * CMP_STRIDE + (CMP_LEN - 1)
    kc0, kc1 = _key_ext(_rms_pair(c[:, :LANES], g1_ref[...], lo), lo, lane, end)
    kce_ref[0, 0] = kc0.astype(BF16)
    kce_ref[0, 1] = kc1.astype(BF16)
    vct = c[:, LANES:].T.astype(BF16)
    vct_ref[0, 0] = vct[:HEAD_DIM]
    vct_ref[0, 1] = vct[HEAD_DIM:]

    rows = 256

    def chunk(i, carry):
        r0 = pl.multiple_of(i * rows, rows)
        blk = kvs_ref[pl.ds(r0, rows), :]
        pos = r0 + lax.broadcasted_iota(jnp.int32, (rows, 1), 0)
        for k_ref, v_ref, gain, off in ((kse_ref, vst_ref, g2_ref, 0), (kwe_ref, vwt_ref, g3_ref, 2 * LANES)):
            k0, k1 = _key_ext(_rms_pair(blk[:, off:off + LANES], gain[...], lo), lo, lane, pos)
            k_ref[0, 0, pl.ds(r0, rows), :] = k0.astype(BF16)
            k_ref[0, 1, pl.ds(r0, rows), :] = k1.astype(BF16)
            vt = blk[:, off + LANES:off + 2 * LANES].T.astype(BF16)
            v_ref[0, 0, :, pl.ds(r0, rows)] = vt[:HEAD_DIM]
            v_ref[0, 1, :, pl.ds(r0, rows)] = vt[HEAD_DIM:]
        return carry

    lax.fori_loop(0, SEQ // rows, chunk, 0)


def _kvprep(rmat, kvs, wa, wb, pea, peb, g1, g2, g3):
    nrow = SEQ // CMP_STRIDE
    const2 = lambda b: (0, 0)
    per_b = lambda b: (b, 0, 0, 0)
    shapes = [(nrow, LANES), (HEAD_DIM, nrow), (SEQ, LANES), (HEAD_DIM, SEQ), (SEQ, LANES), (HEAD_DIM, SEQ)]
    return pl.pallas_call(
        _kvprep_body,
        grid=(BATCH,),
        in_specs=[pl.BlockSpec((1, nrow, rmat.shape[2]), lambda b: (b, 0, 0)),
                  pl.BlockSpec((SEQ, kvs.shape[1]), lambda b: (b, 0)),
                  pl.BlockSpec(wa.shape, const2), pl.BlockSpec(wb.shape, const2),
                  pl.BlockSpec(pea.shape, const2), pl.BlockSpec(peb.shape, const2),
                  pl.BlockSpec((1, LANES), const2), pl.BlockSpec((1, LANES), const2),
                  pl.BlockSpec((1, LANES), const2)],
        out_specs=[pl.BlockSpec((1, NSA_GROUPS) + s, per_b) for s in shapes],
        out_shape=[jax.ShapeDtypeStruct((BATCH, NSA_GROUPS) + s, BF16) for s in shapes],
        compiler_params=_cparams(("parallel",)),
        name="kv_prep",
    )(rmat, kvs, wa, wb, pea, peb, g1, g2, g3)


def _split3(x):
    p1 = x.astype(BF16)
    r1 = x - p1.astype(F32)
    p2 = r1.astype(BF16)
    p3 = (r1 - p2.astype(F32)).astype(BF16)
    return p1, p2, p3


def _nsa_body(q_ref, gl_ref, kce_ref, vct_ref, kse_ref, vst_ref, kwe_ref, vwt_ref,
              g0_ref, ovt_ref, et_ref, o_ref, msk_scr, s_scr):
    g = pl.program_id(1)
    qi = pl.program_id(2)
    q0 = qi * TQ
    lane = lax.broadcasted_iota(jnp.int32, (1, LANES), 1)
    lo = lane < HEAD_DIM
    t_idx = q0 + lax.broadcasted_iota(jnp.int32, (1, TQ), 1)
    q0f = q0.astype(F32)

    qe = []
    for p in range(GQA_REP // 2):
        qn = _rms_pair(q_ref[:, p * LANES:(p + 1) * LANES], g0_ref[...], lo) * (HEAD_DIM ** -0.5)
        for half, base in ((0, qn), (1, pltpu.roll(qn, HEAD_DIM, 1))):
            r = 2 * p + half
            slope = jnp.where(g == 0, 2.0 ** -(r + 1), 2.0 ** -(r + 1 + GQA_REP)).astype(F32)
            ext = jnp.where(lane == HEAD_DIM, SEL_LEN * slope,
                            jnp.where(lane == HEAD_DIM + 1, slope,
                                      jnp.where(lane == HEAD_DIM + 2, -slope * q0f, 0.0)))
            qe.append(jnp.where(lo, base, ext).astype(BF16))

    n_sub = lax.broadcasted_iota(jnp.int32, (SEQ // CMP_STRIDE, 1), 0)
    cmask = (t_idx >= n_sub * CMP_STRIDE + (CMP_LEN - 1)) & (n_sub < N_CMP)
    kce = kce_ref[0, 0]
    vct = vct_ref[0, 0]
    psum = jnp.zeros((SEQ // CMP_STRIDE, TQ), F32)
    o_cmp = []
    for r in range(GQA_REP):
        s = jnp.where(cmask, lax.dot_general(kce, qe[r], _NT, preferred_element_type=F32), NEG_INF)
        e = jnp.where(cmask, jnp.exp(s - jnp.max(s, axis=0, keepdims=True)), 0.0)
        l = jnp.sum(e, axis=0, keepdims=True)
        p = e / jnp.where(l > 0.0, l, 1.0)
        psum = psum + p
        o_cmp.append(jnp.dot(vct, p.astype(BF16), preferred_element_type=F32))

    imp = jnp.zeros((LANES, TQ), F32)
    for part in _split3(psum):
        imp = imp + jnp.dot(ovt_ref[...], part, preferred_element_type=F32)
    imp = imp[:N_SEL]
    j = lax.broadcasted_iota(jnp.int32, (N_SEL, 1), 0)
    tb = t_idx >> 6
    forced = (j == 0) | (j == tb) | (j == tb - 1)
    imp = jnp.where(j <= tb, imp + jnp.where(forced, FORCE_BONUS, 0.0), -1.0)
    rank = jnp.zeros((N_SEL, TQ), F32)
    for i in range(N_SEL):
        ci = imp[i:i + 1, :]
        ahead = (ci > imp) | ((ci == imp) & (j > i))
        rank = rank + jnp.where(ahead, 1.0, 0.0)
    sel = jnp.concatenate([jnp.where(rank < float(SEL_TOPN), 1.0, 0.0),
                           jnp.zeros((LANES - N_SEL, TQ), F32)], axis=0).astype(BF16)
    msk_scr[...] = jnp.dot(et_ref[...], sel, preferred_element_type=F32)

    start = pl.multiple_of(jnp.maximum(q0 - WINDOW, 0), LANES)
    kw = kwe_ref[0, 0, pl.ds(start, WIN_KEYS), :]
    vwt = vwt_ref[0, 0, :, pl.ds(start, WIN_KEYS)]
    wd = t_idx - (start + lax.broadcasted_iota(jnp.int32, (WIN_KEYS, 1), 0))
    wbias = jnp.where((wd >= 0) & (wd < WINDOW), 0.0, NEG_INF)
    o_win = []
    for r in range(GQA_REP):
        s = lax.dot_general(kw, qe[r], _NT, preferred_element_type=F32) + wbias
        e = jnp.exp(s - jnp.max(s, axis=0, keepdims=True))
        l = jnp.sum(e, axis=0, keepdims=True)
        o_win.append(jnp.dot(vwt, e.astype(BF16), preferred_element_type=F32) / l)

    key_sub = lax.broadcasted_iota(jnp.int32, (TK, 1), 0)
    n_kb = (q0 + TQ - 1) // TK + 1

    def score_step(kb, ms):
        k0 = pl.multiple_of(kb * TK, TK)
        kblk = kse_ref[0, 0, pl.ds(k0, TK), :]
        allowed = (msk_scr[pl.ds(k0, TK), :] > 0.5) & (k0 + key_sub <= t_idx)
        bias = jnp.where(allowed, 0.0, NEG_INF)
        out = []
        for r in range(GQA_REP):
            s = lax.dot_general(kblk, qe[r], _NT, preferred_element_type=F32) + bias
            s_scr[r, pl.ds(k0, TK), :] = s
            out.append(jnp.maximum(ms[r], jnp.max(s, axis=0, keepdims=True)))
        return tuple(out)

    ms = lax.fori_loop(0, n_kb, score_step,
                       tuple(jnp.full((1, TQ), NEG_INF, F32) for _ in range(GQA_REP)))

    def value_step(kb, carry):
        ls, accs = carry
        k0 = pl.multiple_of(kb * TK, TK)
        vt = vst_ref[0, 0, :, pl.ds(k0, TK)]
        nl, nacc = [], []
        for r in range(GQA_REP):
            e = jnp.exp(s_scr[r, pl.ds(k0, TK), :] - ms[r])
            nl.append(ls[r] + jnp.sum(e, axis=0, keepdims=True))
            nacc.append(accs[r] + jnp.dot(vt, e.astype(BF16), preferred_element_type=F32))
        return tuple(nl), tuple(nacc)

    ls, accs = lax.fori_loop(0, n_kb, value_step,
                             (tuple(jnp.zeros((1, TQ), F32) for _ in range(GQA_REP)),
                              tuple(jnp.zeros((HEAD_DIM, TQ), F32) for _ in range(GQA_REP))))
    o_slc = [accs[r] / ls[r] for r in range(GQA_REP)]

    sg = jax.nn.sigmoid(gl_ref[...].T[:4 * GQA_REP])
    for p in range(GQA_REP // 2):
        pair = []
        for r in (2 * p, 2 * p + 1):
            pair.append(sg[3 * r:3 * r + 1] * o_cmp[r] + sg[3 * r + 1:3 * r + 2] * o_slc[r]
                        + sg[3 * r + 2:3 * r + 3] * o_win[r])
        o_ref[:, p * LANES:(p + 1) * LANES] = jnp.concatenate(pair, axis=0).T.astype(o_ref.dtype)


def _nsa(q, gl, kce, vct, kse, vst, kwe, vwt, g0, ovt, et):
    nq = SEQ // TQ
    gw = GQA_REP * HEAD_DIM
    tile = lambda b, g, i: (b * nq + i, g)
    per_bg = lambda b, g, i: (b, g, 0, 0)
    const2 = lambda b, g, i: (0, 0)
    return pl.pallas_call(
        _nsa_body,
        grid=(BATCH, NSA_GROUPS, nq),
        in_specs=[pl.BlockSpec((TQ, gw), tile), pl.BlockSpec((TQ, LANES), tile)]
                 + [pl.BlockSpec((1, 1) + a.shape[2:], per_bg) for a in (kce, vct, kse, vst, kwe, vwt)]
                 + [pl.BlockSpec((1, LANES), const2), pl.BlockSpec(ovt.shape, const2),
                    pl.BlockSpec(et.shape, const2)],
        out_specs=pl.BlockSpec((TQ, gw), tile),
        out_shape=jax.ShapeDtypeStruct((TOKENS, NSA_WIDTH), BF16),
        scratch_shapes=[pltpu.VMEM((SEQ, TQ), F32), pltpu.VMEM((GQA_REP, SEQ, TQ), F32)],
        compiler_params=_cparams(("parallel", "parallel", "arbitrary")),
        name="nsa_attention",
    )(q, gl, kce, vct, kse, vst, kwe, vwt, g0, ovt, et)


def _conv_body(glu_ref, w_ref, b_ref, lg_ref, lb_ref, o_ref, u_scr):
    u_scr[0:CONV_PAD, :] = jnp.zeros((CONV_PAD, CONV_WIDTH), F32)
    rows = 256

    def fill(i, carry):
        r0 = pl.multiple_of(i * rows, rows)
        blk = glu_ref[pl.ds(r0, rows), :]
        u_scr[pl.ds(CONV_PAD + r0, rows), :] = blk[:, :CONV_WIDTH] * jax.nn.sigmoid(blk[:, CONV_WIDTH:])
        return carry

    lax.fori_loop(0, SEQ // rows, fill, 0)
    first = CONV_PAD - (CONV_TAPS - 1)

    def tile(i, carry):
        r0 = pl.multiple_of(i * TS_CONV, TS_CONV)
        win = u_scr[pl.ds(r0, TS_CONV + CONV_PAD), :]
        acc = jnp.zeros((TS_CONV, CONV_WIDTH), F32) + b_ref[...]
        for k in range(CONV_TAPS):
            acc = acc + win[first + k:first + k + TS_CONV, :] * w_ref[k:k + 1, :]
        mu = jnp.mean(acc, axis=-1, keepdims=True)
        d = acc - mu
        var = jnp.mean(d * d, axis=-1, keepdims=True)
        yn = d * lax.rsqrt(var + NORM_EPS) * lg_ref[...] + lb_ref[...]
        o_ref[pl.ds(r0, TS_CONV), :] = (yn * jax.nn.sigmoid(yn)).astype(o_ref.dtype)
        return carry

    lax.fori_loop(0, SEQ // TS_CONV, tile, 0)


def _conv(glu, w, b, lg, lb):
    const2 = lambda i: (0, 0)
    return pl.pallas_call(
        _conv_body,
        grid=(BATCH,),
        in_specs=[pl.BlockSpec((SEQ, 2 * CONV_WIDTH), lambda i: (i, 0)),
                  pl.BlockSpec(w.shape, const2), pl.BlockSpec(b.shape, const2),
                  pl.BlockSpec(lg.shape, const2), pl.BlockSpec(lb.shape, const2)],
        out_specs=pl.BlockSpec((SEQ, CONV_WIDTH), lambda i: (i, 0)),
        out_shape=jax.ShapeDtypeStruct((TOKENS, CONV_WIDTH), BF16),
        scratch_shapes=[pltpu.VMEM((CONV_PAD + SEQ, CONV_WIDTH), F32)],
        compiler_params=_cparams(("parallel",)),
        name="conv_mixer",
    )(glu, w, b, lg, lb)


def _outproj_body(a_ref, c_ref, x_ref, mod_ref, g_ref, wa_ref, wc_ref, x1_ref, h2_ref):
    mix = (jnp.dot(a_ref[...], wa_ref[...], preferred_element_type=F32)
           + jnp.dot(c_ref[...], wc_ref[...], preferred_element_type=F32))
    x1 = x_ref[...] + mod_ref[0, 2:3, :] * mix
    x1_ref[...] = x1
    h2_ref[...] = _norm_mod(x1, g_ref[...], mod_ref[0, 3:4, :], mod_ref[0, 4:5, :]).astype(BF16)


def _outproj(a, c, xf, mod3, g, wa, wc):
    tiles_per_batch = SEQ // TM_PROJ
    row = lambda i: (i, 0)
    const2 = lambda i: (0, 0)
    return pl.pallas_call(
        _outproj_body,
        grid=(TOKENS // TM_PROJ,),
        in_specs=[pl.BlockSpec((TM_PROJ, NSA_WIDTH), row),
                  pl.BlockSpec((TM_PROJ, CONV_WIDTH), row),
                  pl.BlockSpec((TM_PROJ, D_MODEL), row),
                  pl.BlockSpec((1, 6, D_MODEL), lambda i: (i // tiles_per_batch, 0, 0)),
                  pl.BlockSpec((1, D_MODEL), const2),
                  pl.BlockSpec(wa.shape, const2), pl.BlockSpec(wc.shape, const2)],
        out_specs=[pl.BlockSpec((TM_PROJ, D_MODEL), row), pl.BlockSpec((TM_PROJ, D_MODEL), row)],
        out_shape=[jax.ShapeDtypeStruct((TOKENS, D_MODEL), F32),
                   jax.ShapeDtypeStruct((TOKENS, D_MODEL), BF16)],
        compiler_params=_cparams(("parallel",)),
        name="out_proj",
    )(a, c, xf, mod3, g, wa, wc)


def _topk_rows(x, k):
    n, cols = x.shape
    sub = lax.broadcasted_iota(jnp.int32, (n, cols), 0).astype(F32)
    slot = lax.broadcasted_iota(jnp.int32, (k, cols), 0)
    vals = jnp.zeros((k, cols), F32)
    idxs = jnp.zeros((k, cols), F32)
    for i in range(k):
        m = jnp.max(x, axis=0, keepdims=True)
        idx = jnp.min(jnp.where(x == m, sub, IDX_BIG), axis=0, keepdims=True)
        x = jnp.where(sub == idx, NEG_BIG, x)
        vals = jnp.where(slot == i, m, vals)
        idxs = jnp.where(slot == i, idx, idxs)
    return vals, idxs


def _pair_topk(v1, v2):
    k, cols = v1.shape
    sub8 = lax.broadcasted_iota(jnp.int32, (8, cols), 0).astype(F32)
    row = lambda v, a: jnp.broadcast_to(v[a:a + 1, :], (8, cols))
    blocks = [(row(v1, 0) + v2[0:8], sub8),
              (row(v1, 0) + v2[8:16], 8.0 + sub8),
              (row(v1, 1) + v2[0:8], 16.0 + sub8)]
    for a in range(2, 8):
        nb = PEER_TOPK // (a + 1)
        blocks.append((jnp.where(sub8 < float(nb), row(v1, a) + v2[0:8], NEG_BIG), 16.0 * a + sub8))
    blocks.append((v1[8:16] + row(v2, 0), (8.0 + sub8) * 16.0))
    vals = [b[0] for b in blocks]
    flats = [b[1] for b in blocks]
    slot = lax.broadcasted_iota(jnp.int32, (k, cols), 0)
    tops = jnp.zeros((k, cols), F32)
    tflat = jnp.zeros((k, cols), F32)
    for i in range(k):
        m = functools.reduce(jnp.maximum, vals)
        m = jnp.max(m, axis=0, keepdims=True)
        f = functools.reduce(jnp.minimum, [jnp.where(v == m, fl, IDX_BIG) for v, fl in zip(vals, flats)])
        f = jnp.min(f, axis=0, keepdims=True)
        vals = [jnp.where(fl == f, NEG_BIG, v) for v, fl in zip(vals, flats)]
        tops = jnp.where(slot == i, m, tops)
        tflat = jnp.where(slot == i, f, tflat)
    return tops, tflat


def _route_body(h_ref, wq_ref, keys_ref, i1_ref, i2_ref, gw_ref, qp_scr, s1_scr, s2_scr, sw_scr):
    qp_scr[...] = jnp.dot(h_ref[...], wq_ref[...], preferred_element_type=F32)
    k = PEER_TOPK

    def head(h, carry):
        c0 = pl.multiple_of(h * PEER_QDIM, PEER_QDIM)
        tv, ti = [], []
        for c in range(2):
            qs = qp_scr[:, pl.ds(c0 + c * LANES, LANES)].astype(BF16)
            st = lax.dot_general(keys_ref[2 * h + c], qs, _NT, preferred_element_type=F32)
            v, i = _topk_rows(st, k)
            tv.append(v)
            ti.append(i)
        tops, tflat = _pair_topk(tv[0], tv[1])
        a_sel = jnp.floor(tflat * (1.0 / k))
        b_sel = tflat - a_sel * float(k)
        i1s = jnp.zeros_like(tops)
        i2s = jnp.zeros_like(tops)
        for a in range(k):
            i1s = i1s + jnp.where(a_sel == float(a), jnp.broadcast_to(ti[0][a:a + 1, :], tops.shape), 0.0)
            i2s = i2s + jnp.where(b_sel == float(a), jnp.broadcast_to(ti[1][a:a + 1, :], tops.shape), 0.0)
        e = jnp.exp(tops - jnp.max(tops, axis=0, keepdims=True))
        w = e / jnp.sum(e, axis=0, keepdims=True)
        r0 = pl.multiple_of(h * k, k)
        s1_scr[pl.ds(r0, k), :] = i1s
        s2_scr[pl.ds(r0, k), :] = i2s
        sw_scr[pl.ds(r0, k), :] = w
        return carry

    lax.fori_loop(0, PEER_HEADS, head, 0)
    i1_ref[...] = s1_scr[...].T
    i2_ref[...] = s2_scr[...].T
    gw_ref[...] = sw_scr[...].T


def _route(h2, wq, keys):
    nsel = PEER_HEADS * PEER_TOPK
    row = lambda i: (i, 0)
    return pl.pallas_call(
        _route_body,
        grid=(TOKENS // TL_ROUTE,),
        in_specs=[pl.BlockSpec((TL_ROUTE, D_MODEL), row),
                  pl.BlockSpec(wq.shape, lambda i: (0, 0)),
                  pl.BlockSpec(keys.shape, lambda i: (0, 0, 0))],
        out_specs=[pl.BlockSpec((TL_ROUTE, nsel), row)] * 3,
        out_shape=[jax.ShapeDtypeStruct((TOKENS, nsel), F32)] * 3,
        scratch_shapes=[pltpu.VMEM((TL_ROUTE, PEER_HEADS * PEER_QDIM), F32)]
                       + [pltpu.VMEM((nsel, TL_ROUTE), F32)] * 3,
        compiler_params=_cparams(("parallel",)),
        name="peer_route",
    )(h2, wq, keys)


def _peer_body(h_ref, u_ref, v_ref, i1_ref, i2_ref, gw_ref, x1_ref, mod_ref, o_ref, g_scr):
    e = pl.program_id(1)
    n = PEER_NKEYS

    @pl.when(e == 0)
    def _():
        o_ref[...] = jnp.zeros(o_ref.shape, F32)
        sub = lax.broadcasted_iota(jnp.int32, (n, n), 0).astype(F32)

        def tok(t, carry):
            w = gw_ref[pl.ds(t, 1), :]
            w_hi = w.astype(BF16).astype(F32)
            m1 = sub == i1_ref[pl.ds(t, 1), :]
            x1 = jnp.concatenate([jnp.where(m1, w_hi, 0.0), jnp.where(m1, w - w_hi, 0.0)], axis=1).astype(BF16)
            x2h = jnp.where(sub == i2_ref[pl.ds(t, 1), :], 1.0, 0.0)
            x2 = jnp.concatenate([x2h, x2h], axis=1).astype(BF16)
            g = lax.dot_general(x1, x2, _NT, preferred_element_type=F32)
            g_scr[pl.ds(pl.multiple_of(t * G_ROW_STRIDE, 8), n), :] = g
            return carry

        lax.fori_loop(0, TT_PEER, tok, 0, unroll=16)

    h = h_ref[...]
    parts = []
    for s in range(ET_PEER // ES_PEER):
        z = lax.dot_general(h, u_ref[s * ES_PEER:(s + 1) * ES_PEER, :], _NT,
                            preferred_element_type=F32)
        act = 0.5 * z * (1.0 + lax.erf(z * (2.0 ** -0.5)))
        for c in range(ES_PEER // n):
            i1 = e * (ET_PEER // n) + s * (ES_PEER // n) + c
            gc = g_scr[pl.ds(i1, TT_PEER, stride=G_ROW_STRIDE), :]
            parts.append((act[:, c * n:(c + 1) * n] * gc).astype(BF16))
    o_ref[...] += jnp.dot(jnp.concatenate(parts, axis=1), v_ref[...], preferred_element_type=F32)

    @pl.when(e == pl.num_programs(1) - 1)
    def _():
        o_ref[...] = x1_ref[...] + mod_ref[0, 5:6, :] * o_ref[...]


def _peer(h2, u, v, i1, i2, gw, x1, mod3):
    tiles_per_batch = SEQ // TT_PEER
    nsel = PEER_HEADS * PEER_TOPK
    tok = lambda i, e: (i, 0)
    exp = lambda i, e: (e, 0)
    return pl.pallas_call(
        _peer_body,
        grid=(TOKENS // TT_PEER, PEER_EXPERTS // ET_PEER),
        in_specs=[pl.BlockSpec((TT_PEER, D_MODEL), tok),
                  pl.BlockSpec((ET_PEER, D_MODEL), exp),
                  pl.BlockSpec((ET_PEER, D_MODEL), exp),
                  pl.BlockSpec((TT_PEER, nsel), tok),
                  pl.BlockSpec((TT_PEER, nsel), tok),
                  pl.BlockSpec((TT_PEER, nsel), tok),
                  pl.BlockSpec((TT_PEER, D_MODEL), tok),
                  pl.BlockSpec((1, 6, D_MODEL), lambda i, e: (i // tiles_per_batch, 0, 0))],
        out_specs=pl.BlockSpec((TT_PEER, D_MODEL), tok),
        out_shape=jax.ShapeDtypeStruct((TOKENS, D_MODEL), F32),
        scratch_shapes=[pltpu.VMEM((TT_PEER * G_ROW_STRIDE, PEER_NKEYS), F32)],
        compiler_params=_cparams(("parallel", "arbitrary")),
        name="peer_experts",
    )(h2, u, v, i1, i2, gw, x1, mod3)


def _overlap_matrix():
    start = np.arange(N_CMP)[None, :] * CMP_STRIDE
    sel = np.arange(N_SEL)[:, None] * SEL_LEN
    ov = np.clip(np.minimum(start + CMP_LEN, sel + SEL_LEN) - np.maximum(start, sel), 0, None) / CMP_LEN
    out = np.zeros((LANES, LANES), np.float32)
    out[:N_SEL, :N_CMP] = ov
    return out


def _block_expand_matrix():
    out = np.zeros((SEQ, LANES), np.float32)
    out[np.arange(SEQ), np.arange(SEQ) // SEL_LEN] = 1.0
    return out


def _cmp_weights(wk, wv, first):
    width = 2 * NSA_GROUPS * HEAD_DIM
    blocks = [w[first:first + CMP_STRIDE].astype(BF16) for w in (wk, wv) for _ in range(NSA_GROUPS)]
    zero = jnp.zeros_like(blocks[0])
    rows = [jnp.concatenate([blk if i == j else zero for j in range(len(blocks))], axis=2)
            for i, blk in enumerate(blocks)]
    return jnp.concatenate(rows, axis=1).reshape(CMP_STRIDE * width, width)


def _dup(v):
    return jnp.concatenate([v, v])[None, :]


def kernel(x, c, w_ada, b_ada, norm_g, w_in, w_out, cmp_pe_k, cmp_pe_v, w_cmp_k, w_cmp_v, qk_norm_g,
           dw_w, dw_b, conv_ln_g, conv_ln_b, peer_wq, peer_sub_keys, peer_u, peer_v):
    assert x.shape == (BATCH, SEQ, D_MODEL) and w_ada.shape[0] == DEPTH
    ovt = jnp.asarray(_overlap_matrix(), BF16)
    emat = jnp.asarray(_block_expand_matrix(), BF16)
    o_kv = NSA_WIDTH
    o_gate = o_kv + 6 * NSA_GROUPS * HEAD_DIM
    o_glu = o_gate + 3 * NSA_HEADS
    n_cmp_cols = 2 * NSA_GROUPS * HEAD_DIM
    xf = x.reshape(TOKENS, D_MODEL)
    for l in range(DEPTH):
        mod3 = _ada(c, w_ada[l], b_ada[l][None, :]).reshape(BATCH, 6, D_MODEL)
        wi = w_in[l]
        gate_pad = jnp.zeros((D_MODEL, LANES - 3 * GQA_REP), F32)
        wg = jnp.concatenate(
            [part for g in range(NSA_GROUPS)
             for part in (wi[:, o_gate + 3 * GQA_REP * g:o_gate + 3 * GQA_REP * (g + 1)], gate_pad)], axis=1)
        q, kvc, kvs, gl, glu = _inproj(
            xf, mod3, norm_g[l, 0][None, :],
            wi[:, :o_kv].astype(BF16), wi[:, o_kv:o_kv + n_cmp_cols].astype(BF16),
            wi[:, o_kv + n_cmp_cols:o_gate].astype(BF16), wg.astype(BF16), wi[:, o_glu:].astype(BF16))
        pe = jnp.concatenate([cmp_pe_k[l], cmp_pe_k[l], cmp_pe_v[l], cmp_pe_v[l]], axis=1)
        kv_parts = _kvprep(
            kvc.reshape(BATCH, SEQ // CMP_STRIDE, CMP_STRIDE * n_cmp_cols), kvs,
            _cmp_weights(w_cmp_k[l], w_cmp_v[l], 0), _cmp_weights(w_cmp_k[l], w_cmp_v[l], CMP_STRIDE),
            pe[:CMP_STRIDE].reshape(1, -1), pe[CMP_STRIDE:].reshape(1, -1),
            _dup(qk_norm_g[l, 1]), _dup(qk_norm_g[l, 2]), _dup(qk_norm_g[l, 3]))
        a_out = _nsa(q, gl, *kv_parts, _dup(qk_norm_g[l, 0]), ovt, emat)
        c_out = _conv(glu, dw_w[l], dw_b[l][None, :], conv_ln_g[l][None, :], conv_ln_b[l][None, :])
        x1, h2 = _outproj(a_out, c_out, xf, mod3, norm_g[l, 1][None, :],
                          w_out[l, :NSA_WIDTH].astype(BF16), w_out[l, NSA_WIDTH:].astype(BF16))
        i1, i2, gw = _route(h2, peer_wq[l].astype(BF16),
                            peer_sub_keys[l].reshape(2 * PEER_HEADS, PEER_NKEYS, PEER_QDIM // 2).astype(BF16))
        xf = _peer(h2, peer_u[l].astype(BF16), peer_v[l].astype(BF16), i1, i2, gw, x1, mod3)
    return xf.reshape(BATCH, SEQ, D_MODEL)
```

```python
import functools

import numpy as np
import jax
import jax.numpy as jnp
from jax import lax
from jax.experimental import pallas as pl
from jax.experimental.pallas import tpu as pltpu

F32 = jnp.float32
BF16 = jnp.bfloat16

D_MODEL = 1024
BATCH = 8
SEQ = 2048
DEPTH = 1
TOKENS = BATCH * SEQ

HEAD_DIM = 64
NSA_HEADS = 8
NSA_GROUPS = 2
GQA_REP = NSA_HEADS // NSA_GROUPS
NSA_WIDTH = NSA_HEADS * HEAD_DIM
CONV_WIDTH = 512
CMP_LEN = 32
CMP_STRIDE = 16
N_CMP = (SEQ - CMP_LEN) // CMP_STRIDE + 1
SEL_LEN = 64
N_SEL = SEQ // SEL_LEN
SEL_TOPN = 16
WINDOW = 512
FORCE_BONUS = 1.0e4
CONV_TAPS = 31
PEER_HEADS = 8
PEER_NKEYS = 128
PEER_EXPERTS = PEER_NKEYS * PEER_NKEYS
PEER_QDIM = 256
PEER_TOPK = 16
NORM_EPS = 1e-6
NEG_INF = -1e30
NEG_BIG = -3.0e38
IDX_BIG = 1.0e9

LANES = 128
VMEM_V7X = 64 * 1024 * 1024
VMEM_LIMIT = VMEM_V7X * 3 // 4
VMEM_LIMIT_PEER = VMEM_V7X * 7 // 8

TM_PROJ = 512
TQ = 256
TK = 256
WIN_KEYS = WINDOW + TQ
TS_CONV = 64
CONV_PAD = 32
TL_ROUTE = 256
TT_PEER = 512
ET_PEER = 1024
ES_PEER = 256
G_ROW_STRIDE = PEER_NKEYS + 8

_NT = (((1,), (1,)), ((), ()))


def _cparams(sem):
    return pltpu.CompilerParams(dimension_semantics=sem, vmem_limit_bytes=VMEM_LIMIT)


def _ada_body(c_ref, w_ref, b_ref, o_ref):
    c = c_ref[...]
    sc = (c * jax.nn.sigmoid(c)).astype(BF16)
    o_ref[...] = jnp.dot(sc, w_ref[...].astype(BF16), preferred_element_type=F32) + b_ref[...]


def _ada(c, w, b):
    n = w.shape[1]
    tn = 1536
    return pl.pallas_call(
        _ada_body,
        grid=(n // tn,),
        in_specs=[pl.BlockSpec((BATCH, D_MODEL), lambda j: (0, 0)),
                  pl.BlockSpec((D_MODEL, tn), lambda j: (0, j)),
                  pl.BlockSpec((1, tn), lambda j: (0, j))],
        out_specs=pl.BlockSpec((BATCH, tn), lambda j: (0, j)),
        out_shape=jax.ShapeDtypeStruct((BATCH, n), F32),
        compiler_params=_cparams(("arbitrary",)),
        name="ada_mod",
    )(c, w, b)


def _norm_mod(x, g, shift, scale):
    ms = jnp.mean(x * x, axis=-1, keepdims=True)
    y = x * lax.rsqrt(ms + NORM_EPS) * g
    return y * (1.0 + scale) + shift


def _inproj_body(x_ref, mod_ref, g_ref, wq_ref, wkc_ref, wks_ref, wg_ref, wglu_ref,
                 q_ref, kc_ref, ks_ref, gl_ref, glu_ref):
    h = _norm_mod(x_ref[...], g_ref[...], mod_ref[0, 0:1, :], mod_ref[0, 1:2, :]).astype(BF16)
    for w_ref, o_ref in ((wq_ref, q_ref), (wkc_ref, kc_ref), (wks_ref, ks_ref),
                         (wg_ref, gl_ref), (wglu_ref, glu_ref)):
        o_ref[...] = jnp.dot(h, w_ref[...], preferred_element_type=F32)


def _inproj(xf, mod3, g, wq, wkc, wks, wg, wglu):
    tiles_per_batch = SEQ // TM_PROJ
    ws = (wq, wkc, wks, wg, wglu)
    row = lambda i: (i, 0)
    return pl.pallas_call(
        _inproj_body,
        grid=(TOKENS // TM_PROJ,),
        in_specs=[pl.BlockSpec((TM_PROJ, D_MODEL), row),
                  pl.BlockSpec((1, 6, D_MODEL), lambda i: (i // tiles_per_batch, 0, 0)),
                  pl.BlockSpec((1, D_MODEL), lambda i: (0, 0))]
                 + [pl.BlockSpec(w.shape, lambda i: (0, 0)) for w in ws],
        out_specs=[pl.BlockSpec((TM_PROJ, w.shape[1]), row) for w in ws],
        out_shape=[jax.ShapeDtypeStruct((TOKENS, w.shape[1]), F32) for w in ws],
        compiler_params=_cparams(("parallel",)),
        name="in_proj",
    )(xf, mod3, g, *ws)


def _rms_pair(x, gdup, lo):
    x2 = x * x
    s_lo = jnp.sum(jnp.where(lo, x2, 0.0), axis=-1, keepdims=True)
    s_hi = jnp.sum(jnp.where(lo, 0.0, x2), axis=-1, keepdims=True)
    rs = jnp.where(lo, lax.rsqrt(s_lo * (1.0 / HEAD_DIM) + NORM_EPS),
                   lax.rsqrt(s_hi * (1.0 / HEAD_DIM) + NORM_EPS))
    return x * rs * gdup


def _key_ext(kn, lo, lane, pos):
    ext = jnp.where(lane == HEAD_DIM, (pos >> 6).astype(F32),
                    jnp.where(lane == HEAD_DIM + 1, (pos & (SEL_LEN - 1)).astype(F32),
                              jnp.where(lane == HEAD_DIM + 2, 1.0, 0.0)))
    return jnp.where(lo, kn, ext), jnp.where(lo, pltpu.roll(kn, HEAD_DIM, 1), ext)


def _kvprep_body(r_ref, kvs_ref, wa_ref, wb_ref, pea_ref, peb_ref, g1_ref, g2_ref, g3_ref,
                 kce_ref, vct_ref, kse_ref, vst_ref, kwe_ref, vwt_ref):
    lane = lax.broadcasted_iota(jnp.int32, (1, LANES), 1)
    lo = lane < HEAD_DIM
    nrow = SEQ // CMP_STRIDE
    r = r_ref[0]
    a = jnp.dot((r + pea_ref[...]).astype(BF16), wa_ref[...], preferred_element_type=F32)
    b = jnp.dot((r + peb_ref[...]).astype(BF16), wb_ref[...], preferred_element_type=F32)
    c = a + pltpu.roll(b, nrow - 1, 0)
    end = lax.broadcasted_iota(jnp.int32, (nrow, 1), 0) * CMP_STRIDE + (CMP_LEN - 1)
    kc0, kc1 = _key_ext(_rms_pair(c[:, :LANES], g1_ref[...], lo), lo, lane, end)
    kce_ref[0, 0] = kc0.astype(BF16)
    kce_ref[0, 1] = kc1.astype(BF16)
    vct = c[:, LANES:].T.astype(BF16)
    vct_ref[0, 0] = vct[:HEAD_DIM]
    vct_ref[0, 1] = vct[HEAD_DIM:]

    rows = 256

    def chunk(i, carry):
        r0 = pl.multiple_of(i * rows, rows)
        blk = kvs_ref[pl.ds(r0, rows), :]
        pos = r0 + lax.broadcasted_iota(jnp.int32, (rows, 1), 0)
        for k_ref, v_ref, gain, off in ((kse_ref, vst_ref, g2_ref, 0), (kwe_ref, vwt_ref, g3_ref, 2 * LANES)):
            k0, k1 = _key_ext(_rms_pair(blk[:, off:off + LANES], gain[...], lo), lo, lane, pos)
            k_ref[0, 0, pl.ds(r0, rows), :] = k0.astype(BF16)
            k_ref[0, 1, pl.ds(r0, rows), :] = k1.astype(BF16)
            vt = blk[:, off + LANES:off + 2 * LANES].T.astype(BF16)
            v_ref[0, 0, :, pl.ds(r0, rows)] = vt[:HEAD_DIM]
            v_ref[0, 1, :, pl.ds(r0, rows)] = vt[HEAD_DIM:]
        return carry

    lax.fori_loop(0, SEQ // rows, chunk, 0)


def _kvprep(rmat, kvs, wa, wb, pea, peb, g1, g2, g3):
    nrow = SEQ // CMP_STRIDE
    const2 = lambda b: (0, 0)
    per_b = lambda b: (b, 0, 0, 0)
    shapes = [(nrow, LANES), (HEAD_DIM, nrow), (SEQ, LANES), (HEAD_DIM, SEQ), (SEQ, LANES), (HEAD_DIM, SEQ)]
    return pl.pallas_call(
        _kvprep_body,
        grid=(BATCH,),
        in_specs=[pl.BlockSpec((1, nrow, rmat.shape[2]), lambda b: (b, 0, 0)),
                  pl.BlockSpec((SEQ, kvs.shape[1]), lambda b: (b, 0)),
                  pl.BlockSpec(wa.shape, const2), pl.BlockSpec(wb.shape, const2),
                  pl.BlockSpec(pea.shape, const2), pl.BlockSpec(peb.shape, const2),
                  pl.BlockSpec((1, LANES), const2), pl.BlockSpec((1, LANES), const2),
                  pl.BlockSpec((1, LANES), const2)],
        out_specs=[pl.BlockSpec((1, NSA_GROUPS) + s, per_b) for s in shapes],
        out_shape=[jax.ShapeDtypeStruct((BATCH, NSA_GROUPS) + s, BF16) for s in shapes],
        compiler_params=_cparams(("parallel",)),
        name="kv_prep",
    )(rmat, kvs, wa, wb, pea, peb, g1, g2, g3)


def _split3(x):
    p1 = x.astype(BF16)
    r1 = x - p1.astype(F32)
    p2 = r1.astype(BF16)
    p3 = (r1 - p2.astype(F32)).astype(BF16)
    return p1, p2, p3


def _nsa_body(q_ref, gl_ref, kce_ref, vct_ref, kse_ref, vst_ref, kwe_ref, vwt_ref,
              g0_ref, ovt_ref, et_ref, o_ref, msk_scr, s_scr):
    g = pl.program_id(1)
    qi = pl.program_id(2)
    q0 = qi * TQ
    lane = lax.broadcasted_iota(jnp.int32, (1, LANES), 1)
    lo = lane < HEAD_DIM
    t_idx = q0 + lax.broadcasted_iota(jnp.int32, (1, TQ), 1)
    q0f = q0.astype(F32)

    qe = []
    for p in range(GQA_REP // 2):
        qn = _rms_pair(q_ref[:, p * LANES:(p + 1) * LANES], g0_ref[...], lo) * (HEAD_DIM ** -0.5)
        for half, base in ((0, qn), (1, pltpu.roll(qn, HEAD_DIM, 1))):
            r = 2 * p + half
            slope = jnp.where(g == 0, 2.0 ** -(r + 1), 2.0 ** -(r + 1 + GQA_REP)).astype(F32)
            ext = jnp.where(lane == HEAD_DIM, SEL_LEN * slope,
                            jnp.where(lane == HEAD_DIM + 1, slope,
                                      jnp.where(lane == HEAD_DIM + 2, -slope * q0f, 0.0)))
            qe.append(jnp.where(lo, base, ext).astype(BF16))

    n_sub = lax.broadcasted_iota(jnp.int32, (SEQ // CMP_STRIDE, 1), 0)
    cmask = (t_idx >= n_sub * CMP_STRIDE + (CMP_LEN - 1)) & (n_sub < N_CMP)
    kce = kce_ref[0, 0]
    vct = vct_ref[0, 0]
    psum = jnp.zeros((SEQ // CMP_STRIDE, TQ), F32)
    o_cmp = []
    for r in range(GQA_REP):
        s = jnp.where(cmask, lax.dot_general(kce, qe[r], _NT, preferred_element_type=F32), NEG_INF)
        e = jnp.where(cmask, jnp.exp(s - jnp.max(s, axis=0, keepdims=True)), 0.0)
        l = jnp.sum(e, axis=0, keepdims=True)
        p = e / jnp.where(l > 0.0, l, 1.0)
        psum = psum + p
        o_cmp.append(jnp.dot(vct, p.astype(BF16), preferred_element_type=F32))

    imp = jnp.zeros((LANES, TQ), F32)
    for part in _split3(psum):
        imp = imp + jnp.dot(ovt_ref[...], part, preferred_element_type=F32)
    imp = imp[:N_SEL]
    j = lax.broadcasted_iota(jnp.int32, (N_SEL, 1), 0)
    tb = t_idx >> 6
    forced = (j == 0) | (j == tb) | (j == tb - 1)
    imp = jnp.where(j <= tb, imp + jnp.where(forced, FORCE_BONUS, 0.0), -1.0)
    rank = jnp.zeros((N_SEL, TQ), F32)
    for i in range(N_SEL):
        ci = imp[i:i + 1, :]
        ahead = (ci > imp) | ((ci == imp) & (j > i))
        rank = rank + jnp.where(ahead, 1.0, 0.0)
    sel = jnp.concatenate([jnp.where(rank < float(SEL_TOPN), 1.0, 0.0),
                           jnp.zeros((LANES - N_SEL, TQ), F32)], axis=0).astype(BF16)
    msk_scr[...] = jnp.dot(et_ref[...], sel, preferred_element_type=F32)

    start = pl.multiple_of(jnp.maximum(q0 - WINDOW, 0), LANES)
    kw = kwe_ref[0, 0, pl.ds(start, WIN_KEYS), :]
    vwt = vwt_ref[0, 0, :, pl.ds(start, WIN_KEYS)]
    wd = t_idx - (start + lax.broadcasted_iota(jnp.int32, (WIN_KEYS, 1), 0))
    wbias = jnp.where((wd >= 0) & (wd < WINDOW), 0.0, NEG_INF)
    o_win = []
    for r in range(GQA_REP):
        s = lax.dot_general(kw, qe[r], _NT, preferred_element_type=F32) + wbias
        e = jnp.exp(s - jnp.max(s, axis=0, keepdims=True))
        l = jnp.sum(e, axis=0, keepdims=True)
        o_win.append(jnp.dot(vwt, e.astype(BF16), preferred_element_type=F32) / l)

    key_sub = lax.broadcasted_iota(jnp.int32, (TK, 1), 0)
    n_kb = (q0 + TQ - 1) // TK + 1

    def score_step(kb, ms):
        k0 = pl.multiple_of(kb * TK, TK)
        kblk = kse_ref[0, 0, pl.ds(k0, TK), :]
        allowed = (msk_scr[pl.ds(k0, TK), :] > 0.5) & (k0 + key_sub <= t_idx)
        bias = jnp.where(allowed, 0.0, NEG_INF)
        out = []
        for r in range(GQA_REP):
            s = lax.dot_general(kblk, qe[r], _NT, preferred_element_type=F32) + bias
            s_scr[r, pl.ds(k0, TK), :] = s
            out.append(jnp.maximum(ms[r], jnp.max(s, axis=0, keepdims=True)))
        return tuple(out)

    ms = lax.fori_loop(0, n_kb, score_step,
                       tuple(jnp.full((1, TQ), NEG_INF, F32) for _ in range(GQA_REP)))

    def value_step(kb, carry):
        ls, accs = carry
        k0 = pl.multiple_of(kb * TK, TK)
        vt = vst_ref[0, 0, :, pl.ds(k0, TK)]
        nl, nacc = [], []
        for r in range(GQA_REP):
            e = jnp.exp(s_scr[r, pl.ds(k0, TK), :] - ms[r])
            nl.append(ls[r] + jnp.sum(e, axis=0, keepdims=True))
            nacc.append(accs[r] + jnp.dot(vt, e.astype(BF16), preferred_element_type=F32))
        return tuple(nl), tuple(nacc)

    ls, accs = lax.fori_loop(0, n_kb, value_step,
                             (tuple(jnp.zeros((1, TQ), F32) for _ in range(GQA_REP)),
                              tuple(jnp.zeros((HEAD_DIM, TQ), F32) for _ in range(GQA_REP))))
    o_slc = [accs[r] / ls[r] for r in range(GQA_REP)]

    sg = jax.nn.sigmoid(gl_ref[...].T[:4 * GQA_REP])
    for p in range(GQA_REP // 2):
        pair = []
        for r in (2 * p, 2 * p + 1):
            pair.append(sg[3 * r:3 * r + 1] * o_cmp[r] + sg[3 * r + 1:3 * r + 2] * o_slc[r]
                        + sg[3 * r + 2:3 * r + 3] * o_win[r])
        o_ref[:, p * LANES:(p + 1) * LANES] = jnp.concatenate(pair, axis=0).T.astype(o_ref.dtype)


def _nsa(q, gl, kce, vct, kse, vst, kwe, vwt, g0, ovt, et):
    nq = SEQ // TQ
    gw = GQA_REP * HEAD_DIM
    tile = lambda b, g, i: (b * nq + i, g)
    per_bg = lambda b, g, i: (b, g, 0, 0)
    const2 = lambda b, g, i: (0, 0)
    return pl.pallas_call(
        _nsa_body,
        grid=(BATCH, NSA_GROUPS, nq),
        in_specs=[pl.BlockSpec((TQ, gw), tile), pl.BlockSpec((TQ, LANES), tile)]
                 + [pl.BlockSpec((1, 1) + a.shape[2:], per_bg) for a in (kce, vct, kse, vst, kwe, vwt)]
                 + [pl.BlockSpec((1, LANES), const2), pl.BlockSpec(ovt.shape, const2),
                    pl.BlockSpec(et.shape, const2)],
        out_specs=pl.BlockSpec((TQ, gw), tile),
        out_shape=jax.ShapeDtypeStruct((TOKENS, NSA_WIDTH), BF16),
        scratch_shapes=[pltpu.VMEM((SEQ, TQ), F32), pltpu.VMEM((GQA_REP, SEQ, TQ), F32)],
        compiler_params=_cparams(("parallel", "parallel", "arbitrary")),
        name="nsa_attention",
    )(q, gl, kce, vct, kse, vst, kwe, vwt, g0, ovt, et)


def _conv_body(glu_ref, w_ref, b_ref, lg_ref, lb_ref, o_ref, u_scr):
    u_scr[0:CONV_PAD, :] = jnp.zeros((CONV_PAD, CONV_WIDTH), F32)
    rows = 256

    def fill(i, carry):
        r0 = pl.multiple_of(i * rows, rows)
        blk = glu_ref[pl.ds(r0, rows), :]
        u_scr[pl.ds(CONV_PAD + r0, rows), :] = blk[:, :CONV_WIDTH] * jax.nn.sigmoid(blk[:, CONV_WIDTH:])
        return carry

    lax.fori_loop(0, SEQ // rows, fill, 0)
    first = CONV_PAD - (CONV_TAPS - 1)

    def tile(i, carry):
        r0 = pl.multiple_of(i * TS_CONV, TS_CONV)
        win = u_scr[pl.ds(r0, TS_CONV + CONV_PAD), :]
        acc = jnp.zeros((TS_CONV, CONV_WIDTH), F32) + b_ref[...]
        for k in range(CONV_TAPS):
            acc = acc + win[first + k:first + k + TS_CONV, :] * w_ref[k:k + 1, :]
        mu = jnp.mean(acc, axis=-1, keepdims=True)
        d = acc - mu
        var = jnp.mean(d * d, axis=-1, keepdims=True)
        yn = d * lax.rsqrt(var + NORM_EPS) * lg_ref[...] + lb_ref[...]
        o_ref[pl.ds(r0, TS_CONV), :] = (yn * jax.nn.sigmoid(yn)).astype(o_ref.dtype)
        return carry

    lax.fori_loop(0, SEQ // TS_CONV, tile, 0)


def _conv(glu, w, b, lg, lb):
    const2 = lambda i: (0, 0)
    return pl.pallas_call(
        _conv_body,
        grid=(BATCH,),
        in_specs=[pl.BlockSpec((SEQ, 2 * CONV_WIDTH), lambda i: (i, 0)),
                  pl.BlockSpec(w.shape, const2), pl.BlockSpec(b.shape, const2),
                  pl.BlockSpec(lg.shape, const2), pl.BlockSpec(lb.shape, const2)],
        out_specs=pl.BlockSpec((SEQ, CONV_WIDTH), lambda i: (i, 0)),
        out_shape=jax.ShapeDtypeStruct((TOKENS, CONV_WIDTH), BF16),
        scratch_shapes=[pltpu.VMEM((CONV_PAD + SEQ, CONV_WIDTH), F32)],
        compiler_params=_cparams(("parallel",)),
        name="conv_mixer",
    )(glu, w, b, lg, lb)


def _outproj_body(a_ref, c_ref, x_ref, mod_ref, g_ref, wa_ref, wc_ref, x1_ref, h2_ref):
    mix = (jnp.dot(a_ref[...], wa_ref[...], preferred_element_type=F32)
           + jnp.dot(c_ref[...], wc_ref[...], preferred_element_type=F32))
    x1 = x_ref[...] + mod_ref[0, 2:3, :] * mix
    x1_ref[...] = x1
    h2_ref[...] = _norm_mod(x1, g_ref[...], mod_ref[0, 3:4, :], mod_ref[0, 4:5, :]).astype(BF16)


def _outproj(a, c, xf, mod3, g, wa, wc):
    tiles_per_batch = SEQ // TM_PROJ
    row = lambda i: (i, 0)
    const2 = lambda i: (0, 0)
    return pl.pallas_call(
        _outproj_body,
        grid=(TOKENS // TM_PROJ,),
        in_specs=[pl.BlockSpec((TM_PROJ, NSA_WIDTH), row),
                  pl.BlockSpec((TM_PROJ, CONV_WIDTH), row),
                  pl.BlockSpec((TM_PROJ, D_MODEL), row),
                  pl.BlockSpec((1, 6, D_MODEL), lambda i: (i // tiles_per_batch, 0, 0)),
                  pl.BlockSpec((1, D_MODEL), const2),
                  pl.BlockSpec(wa.shape, const2), pl.BlockSpec(wc.shape, const2)],
        out_specs=[pl.BlockSpec((TM_PROJ, D_MODEL), row), pl.BlockSpec((TM_PROJ, D_MODEL), row)],
        out_shape=[jax.ShapeDtypeStruct((TOKENS, D_MODEL), F32),
                   jax.ShapeDtypeStruct((TOKENS, D_MODEL), BF16)],
        compiler_params=_cparams(("parallel",)),
        name="out_proj",
    )(a, c, xf, mod3, g, wa, wc)


def _topk_rows(x, k):
    n, cols = x.shape
    sub = lax.broadcasted_iota(jnp.int32, (n, cols), 0).astype(F32)
    slot = lax.broadcasted_iota(jnp.int32, (k, cols), 0)
    vals = jnp.zeros((k, cols), F32)
    idxs = jnp.zeros((k, cols), F32)
    for i in range(k):
        m = jnp.max(x, axis=0, keepdims=True)
        idx = jnp.min(jnp.where(x == m, sub, IDX_BIG), axis=0, keepdims=True)
        x = jnp.where(sub == idx, NEG_BIG, x)
        vals = jnp.where(slot == i, m, vals)
        idxs = jnp.where(slot == i, idx, idxs)
    return vals, idxs


def _pair_topk(v1, v2):
    k, cols = v1.shape
    sub8 = lax.broadcasted_iota(jnp.int32, (8, cols), 0).astype(F32)
    row = lambda v, a: jnp.broadcast_to(v[a:a + 1, :], (8, cols))
    blocks = [(row(v1, 0) + v2[0:8], sub8),
              (row(v1, 0) + v2[8:16], 8.0 + sub8),
              (row(v1, 1) + v2[0:8], 16.0 + sub8)]
    for a in range(2, 8):
        nb = PEER_TOPK // (a + 1)
        blocks.append((jnp.where(sub8 < float(nb), row(v1, a) + v2[0:8], NEG_BIG), 16.0 * a + sub8))
    blocks.append((v1[8:16] + row(v2, 0), (8.0 + sub8) * 16.0))
    vals = [b[0] for b in blocks]
    flats = [b[1] for b in blocks]
    slot = lax.broadcasted_iota(jnp.int32, (k, cols), 0)
    tops = jnp.zeros((k, cols), F32)
    tflat = jnp.zeros((k, cols), F32)
    for i in range(k):
        m = functools.reduce(jnp.maximum, vals)
        m = jnp.max(m, axis=0, keepdims=True)
        f = functools.reduce(jnp.minimum, [jnp.where(v == m, fl, IDX_BIG) for v, fl in zip(vals, flats)])
        f = jnp.min(f, axis=0, keepdims=True)
        vals = [jnp.where(fl == f, NEG_BIG, v) for v, fl in zip(vals, flats)]
        tops = jnp.where(slot == i, m, tops)
        tflat = jnp.where(slot == i, f, tflat)
    return tops, tflat


def _route_body(h_ref, wq_ref, keys_ref, i1_ref, i2_ref, gw_ref, qp_scr, s1_scr, s2_scr, sw_scr):
    qp_scr[...] = jnp.dot(h_ref[...], wq_ref[...], preferred_element_type=F32)
    k = PEER_TOPK

    def head(h, carry):
        c0 = pl.multiple_of(h * PEER_QDIM, PEER_QDIM)
        tv, ti = [], []
        for c in range(2):
            qs = qp_scr[:, pl.ds(c0 + c * LANES, LANES)].astype(BF16)
            st = lax.dot_general(keys_ref[2 * h + c], qs, _NT, preferred_element_type=F32)
            v, i = _topk_rows(st, k)
            tv.append(v)
            ti.append(i)
        tops, tflat = _pair_topk(tv[0], tv[1])
        a_sel = jnp.floor(tflat * (1.0 / k))
        b_sel = tflat - a_sel * float(k)
        i1s = jnp.zeros_like(tops)
        i2s = jnp.zeros_like(tops)
        for a in range(k):
            i1s = i1s + jnp.where(a_sel == float(a), jnp.broadcast_to(ti[0][a:a + 1, :], tops.shape), 0.0)
            i2s = i2s + jnp.where(b_sel == float(a), jnp.broadcast_to(ti[1][a:a + 1, :], tops.shape), 0.0)
        e = jnp.exp(tops - jnp.max(tops, axis=0, keepdims=True))
        w = e / jnp.sum(e, axis=0, keepdims=True)
        r0 = pl.multiple_of(h * k, k)
        s1_scr[pl.ds(r0, k), :] = i1s
        s2_scr[pl.ds(r0, k), :] = i2s
        sw_scr[pl.ds(r0, k), :] = w
        return carry

    lax.fori_loop(0, PEER_HEADS, head, 0, unroll=2)
    i1_ref[...] = s1_scr[...].T
    i2_ref[...] = s2_scr[...].T
    gw_ref[...] = sw_scr[...].T


def _route(h2, wq, keys):
    nsel = PEER_HEADS * PEER_TOPK
    row = lambda i: (i, 0)
    return pl.pallas_call(
        _route_body,
        grid=(TOKENS // TL_ROUTE,),
        in_specs=[pl.BlockSpec((TL_ROUTE, D_MODEL), row),
                  pl.BlockSpec(wq.shape, lambda i: (0, 0)),
                  pl.BlockSpec(keys.shape, lambda i: (0, 0, 0))],
        out_specs=[pl.BlockSpec((TL_ROUTE, nsel), row)] * 3,
        out_shape=[jax.ShapeDtypeStruct((TOKENS, nsel), F32)] * 3,
        scratch_shapes=[pltpu.VMEM((TL_ROUTE, PEER_HEADS * PEER_QDIM), F32)]
                       + [pltpu.VMEM((nsel, TL_ROUTE), F32)] * 3,
        compiler_params=_cparams(("parallel",)),
        name="peer_route",
    )(h2, wq, keys)


def _peer_body(h_ref, u_ref, v_ref, i1_ref, i2_ref, gw_ref, x1_ref, mod_ref, o_ref, g_scr):
    e = pl.program_id(1)
    n = PEER_NKEYS

    @pl.when(e == 0)
    def _():
        o_ref[...] = jnp.zeros(o_ref.shape, F32)
        sub = lax.broadcasted_iota(jnp.int32, (n, n), 0).astype(F32)

        def tok(t, carry):
            w = gw_ref[pl.ds(t, 1), :]
            w_hi = w.astype(BF16).astype(F32)
            m1 = sub == i1_ref[pl.ds(t, 1), :]
            x1 = jnp.concatenate([jnp.where(m1, w_hi, 0.0), jnp.where(m1, w - w_hi, 0.0)], axis=1).astype(BF16)
            x2h = jnp.where(sub == i2_ref[pl.ds(t, 1), :], 1.0, 0.0)
            x2 = jnp.concatenate([x2h, x2h], axis=1).astype(BF16)
            g = lax.dot_general(x1, x2, _NT, preferred_element_type=F32)
            g_scr[pl.ds(pl.multiple_of(t * G_ROW_STRIDE, 8), n), :] = g
            return carry

        lax.fori_loop(0, TT_PEER, tok, 0, unroll=16)

    h = h_ref[...]
    parts = []
    for s in range(ET_PEER // ES_PEER):
        z = lax.dot_general(h, u_ref[s * ES_PEER:(s + 1) * ES_PEER, :], _NT,
                            preferred_element_type=F32)
        act = 0.5 * z * (1.0 + lax.erf(z * (2.0 ** -0.5)))
        for c in range(ES_PEER // n):
            i1 = e * (ET_PEER // n) + s * (ES_PEER // n) + c
            gc = g_scr[pl.ds(i1, TT_PEER, stride=G_ROW_STRIDE), :]
            parts.append((act[:, c * n:(c + 1) * n] * gc).astype(BF16))
    o_ref[...] += jnp.dot(jnp.concatenate(parts, axis=1), v_ref[...], preferred_element_type=F32)

    @pl.when(e == pl.num_programs(1) - 1)
    def _():
        o_ref[...] = x1_ref[...] + mod_ref[0, 5:6, :] * o_ref[...]


def _peer(h2, u, v, i1, i2, gw, x1, mod3):
    tiles_per_batch = SEQ // TT_PEER
    nsel = PEER_HEADS * PEER_TOPK
    tok = lambda i, e: (i, 0)
    exp = lambda i, e: (e, 0)
    once = pl.Buffered(1)
    return pl.pallas_call(
        _peer_body,
        grid=(TOKENS // TT_PEER, PEER_EXPERTS // ET_PEER),
        in_specs=[pl.BlockSpec((TT_PEER, D_MODEL), tok, pipeline_mode=once),
                  pl.BlockSpec((ET_PEER, D_MODEL), exp),
                  pl.BlockSpec((ET_PEER, D_MODEL), exp),
                  pl.BlockSpec((TT_PEER, nsel), tok, pipeline_mode=once),
                  pl.BlockSpec((TT_PEER, nsel), tok, pipeline_mode=once),
                  pl.BlockSpec((TT_PEER, nsel), tok, pipeline_mode=once),
                  pl.BlockSpec((TT_PEER, D_MODEL), tok, pipeline_mode=once),
                  pl.BlockSpec((1, 6, D_MODEL), lambda i, e: (i // tiles_per_batch, 0, 0))],
        out_specs=pl.BlockSpec((TT_PEER, D_MODEL), tok),
        out_shape=jax.ShapeDtypeStruct((TOKENS, D_MODEL), F32),
        scratch_shapes=[pltpu.VMEM((TT_PEER * G_ROW_STRIDE, PEER_NKEYS), F32)],
        compiler_params=pltpu.CompilerParams(dimension_semantics=("parallel", "arbitrary"),
                                             vmem_limit_bytes=VMEM_LIMIT_PEER),
        name="peer_experts",
    )(h2, u, v, i1, i2, gw, x1, mod3)


def _overlap_matrix():
    start = np.arange(N_CMP)[None, :] * CMP_STRIDE
    sel = np.arange(N_SEL)[:, None] * SEL_LEN
    ov = np.clip(np.minimum(start + CMP_LEN, sel + SEL_LEN) - np.maximum(start, sel), 0, None) / CMP_LEN
    out = np.zeros((LANES, LANES), np.float32)
    out[:N_SEL, :N_CMP] = ov
    return out


def _block_expand_matrix():
    out = np.zeros((SEQ, LANES), np.float32)
    out[np.arange(SEQ), np.arange(SEQ) // SEL_LEN] = 1.0
    return out


def _cmp_weights(wk, wv, first):
    width = 2 * NSA_GROUPS * HEAD_DIM
    blocks = [w[first:first + CMP_STRIDE].astype(BF16) for w in (wk, wv) for _ in range(NSA_GROUPS)]
    zero = jnp.zeros_like(blocks[0])
    rows = [jnp.concatenate([blk if i == j else zero for j in range(len(blocks))], axis=2)
            for i, blk in enumerate(blocks)]
    return jnp.concatenate(rows, axis=1).reshape(CMP_STRIDE * width, width)


def _dup(v):
    return jnp.concatenate([v, v])[None, :]


def kernel(x, c, w_ada, b_ada, norm_g, w_in, w_out, cmp_pe_k, cmp_pe_v, w_cmp_k, w_cmp_v, qk_norm_g,
           dw_w, dw_b, conv_ln_g, conv_ln_b, peer_wq, peer_sub_keys, peer_u, peer_v):
    assert x.shape == (BATCH, SEQ, D_MODEL) and w_ada.shape[0] == DEPTH
    ovt = jnp.asarray(_overlap_matrix(), BF16)
    emat = jnp.asarray(_block_expand_matrix(), BF16)
    o_kv = NSA_WIDTH
    o_gate = o_kv + 6 * NSA_GROUPS * HEAD_DIM
    o_glu = o_gate + 3 * NSA_HEADS
    n_cmp_cols = 2 * NSA_GROUPS * HEAD_DIM
    xf = x.reshape(TOKENS, D_MODEL)
    for l in range(DEPTH):
        mod3 = _ada(c, w_ada[l], b_ada[l][None, :]).reshape(BATCH, 6, D_MODEL)
        wi = w_in[l]
        gate_pad = jnp.zeros((D_MODEL, LANES - 3 * GQA_REP), F32)
        wg = jnp.concatenate(
            [part for g in range(NSA_GROUPS)
             for part in (wi[:, o_gate + 3 * GQA_REP * g:o_gate + 3 * GQA_REP * (g + 1)], gate_pad)], axis=1)
        q, kvc, kvs, gl, glu = _inproj(
            xf, mod3, norm_g[l, 0][None, :],
            wi[:, :o_kv].astype(BF16), wi[:, o_kv:o_kv + n_cmp_cols].astype(BF16),
            wi[:, o_kv + n_cmp_cols:o_gate].astype(BF16), wg.astype(BF16), wi[:, o_glu:].astype(BF16))
        pe = jnp.concatenate([cmp_pe_k[l], cmp_pe_k[l], cmp_pe_v[l], cmp_pe_v[l]], axis=1)
        kv_parts = _kvprep(
            kvc.reshape(BATCH, SEQ // CMP_STRIDE, CMP_STRIDE * n_cmp_cols), kvs,
            _cmp_weights(w_cmp_k[l], w_cmp_v[l], 0), _cmp_weights(w_cmp_k[l], w_cmp_v[l], CMP_STRIDE),
            pe[:CMP_STRIDE].reshape(1, -1), pe[CMP_STRIDE:].reshape(1, -1),
            _dup(qk_norm_g[l, 1]), _dup(qk_norm_g[l, 2]), _dup(qk_norm_g[l, 3]))
        a_out = _nsa(q, gl, *kv_parts, _dup(qk_norm_g[l, 0]), ovt, emat)
        c_out = _conv(glu, dw_w[l], dw_b[l][None, :], conv_ln_g[l][None, :], conv_ln_b[l][None, :])
        x1, h2 = _outproj(a_out, c_out, xf, mod3, norm_g[l, 1][None, :],
                          w_out[l, :NSA_WIDTH].astype(BF16), w_out[l, NSA_WIDTH:].astype(BF16))
        i1, i2, gw = _route(h2, peer_wq[l].astype(BF16),
                            peer_sub_keys[l].reshape(2 * PEER_HEADS, PEER_NKEYS, PEER_QDIM // 2).astype(BF16))
        xf = _peer(h2, peer_u[l].astype(BF16), peer_v[l].astype(BF16), i1, i2, gw, x1, mod3)
    return xf.reshape(BATCH, SEQ, D_MODEL)
```

```python
import functools

import numpy as np
import jax
import jax.numpy as jnp
from jax import lax
from jax.experimental import pallas as pl
from jax.experimental.pallas import tpu as pltpu

F32 = jnp.float32
BF16 = jnp.bfloat16

D_MODEL = 1024
BATCH = 8
SEQ = 2048
DEPTH = 1
TOKENS = BATCH * SEQ

HEAD_DIM = 64
NSA_HEADS = 8
NSA_GROUPS = 2
GQA_REP = NSA_HEADS // NSA_GROUPS
NSA_WIDTH = NSA_HEADS * HEAD_DIM
CONV_WIDTH = 512
CMP_LEN = 32
CMP_STRIDE = 16
N_CMP = (SEQ - CMP_LEN) // CMP_STRIDE + 1
SEL_LEN = 64
N_SEL = SEQ // SEL_LEN
SEL_TOPN = 16
WINDOW = 512
FORCE_BONUS = 1.0e4
CONV_TAPS = 31
PEER_HEADS = 8
PEER_NKEYS = 128
PEER_EXPERTS = PEER_NKEYS * PEER_NKEYS
PEER_QDIM = 256
PEER_TOPK = 16
NORM_EPS = 1e-6
NEG_INF = -1e30
NEG_BIG = -3.0e38
IDX_BIG = 1.0e9

LANES = 128
VMEM_V7X = 64 * 1024 * 1024
VMEM_LIMIT = VMEM_V7X * 3 // 4
VMEM_LIMIT_PEER = VMEM_V7X * 7 // 8

TM_PROJ = 512
TQ = 256
TK = 256
WIN_KEYS = WINDOW + TQ
TS_CONV = 64
CONV_PAD = 32
TL_ROUTE = 256
TT_PEER = 512
ET_PEER = 1024
ES_PEER = 256
G_ROW_STRIDE = PEER_NKEYS + 8

_NT = (((1,), (1,)), ((), ()))


def _cparams(sem):
    return pltpu.CompilerParams(dimension_semantics=sem, vmem_limit_bytes=VMEM_LIMIT)


def _ada_body(c_ref, w_ref, b_ref, o_ref):
    c = c_ref[...]
    sc = (c * jax.nn.sigmoid(c)).astype(BF16)
    o_ref[...] = jnp.dot(sc, w_ref[...].astype(BF16), preferred_element_type=F32) + b_ref[...]


def _ada(c, w, b):
    n = w.shape[1]
    tn = 1536
    return pl.pallas_call(
        _ada_body,
        grid=(n // tn,),
        in_specs=[pl.BlockSpec((BATCH, D_MODEL), lambda j: (0, 0)),
                  pl.BlockSpec((D_MODEL, tn), lambda j: (0, j)),
                  pl.BlockSpec((1, tn), lambda j: (0, j))],
        out_specs=pl.BlockSpec((BATCH, tn), lambda j: (0, j)),
        out_shape=jax.ShapeDtypeStruct((BATCH, n), F32),
        compiler_params=_cparams(("arbitrary",)),
        name="ada_mod",
    )(c, w, b)


def _norm_mod(x, g, shift, scale):
    ms = jnp.mean(x * x, axis=-1, keepdims=True)
    y = x * lax.rsqrt(ms + NORM_EPS) * g
    return y * (1.0 + scale) + shift


def _inproj_body(x_ref, mod_ref, g_ref, wq_ref, wkc_ref, wks_ref, wg_ref, wglu_ref,
                 q_ref, kc_ref, ks_ref, gl_ref, glu_ref):
    h = _norm_mod(x_ref[...], g_ref[...], mod_ref[0, 0:1, :], mod_ref[0, 1:2, :]).astype(BF16)
    for w_ref, o_ref in ((wq_ref, q_ref), (wkc_ref, kc_ref), (wks_ref, ks_ref),
                         (wg_ref, gl_ref), (wglu_ref, glu_ref)):
        o_ref[...] = jnp.dot(h, w_ref[...], preferred_element_type=F32)


def _inproj(xf, mod3, g, wq, wkc, wks, wg, wglu):
    tiles_per_batch = SEQ // TM_PROJ
    ws = (wq, wkc, wks, wg, wglu)
    row = lambda i: (i, 0)
    return pl.pallas_call(
        _inproj_body,
        grid=(TOKENS // TM_PROJ,),
        in_specs=[pl.BlockSpec((TM_PROJ, D_MODEL), row),
                  pl.BlockSpec((1, 6, D_MODEL), lambda i: (i // tiles_per_batch, 0, 0)),
                  pl.BlockSpec((1, D_MODEL), lambda i: (0, 0))]
                 + [pl.BlockSpec(w.shape, lambda i: (0, 0)) for w in ws],
        out_specs=[pl.BlockSpec((TM_PROJ, w.shape[1]), row) for w in ws],
        out_shape=[jax.ShapeDtypeStruct((TOKENS, w.shape[1]), F32) for w in ws],
        compiler_params=_cparams(("parallel",)),
        name="in_proj",
    )(xf, mod3, g, *ws)


def _rms_pair(x, gdup, lo):
    x2 = x * x
    s_lo = jnp.sum(jnp.where(lo, x2, 0.0), axis=-1, keepdims=True)
    s_hi = jnp.sum(jnp.where(lo, 0.0, x2), axis=-1, keepdims=True)
    rs = jnp.where(lo, lax.rsqrt(s_lo * (1.0 / HEAD_DIM) + NORM_EPS),
                   lax.rsqrt(s_hi * (1.0 / HEAD_DIM) + NORM_EPS))
    return x * rs * gdup


def _key_ext(kn, lo, lane, pos):
    ext = jnp.where(lane == HEAD_DIM, (pos >> 6).astype(F32),
                    jnp.where(lane == HEAD_DIM + 1, (pos & (SEL_LEN - 1)).astype(F32),
                              jnp.where(lane == HEAD_DIM + 2, 1.0, 0.0)))
    return jnp.where(lo, kn, ext), jnp.where(lo, pltpu.roll(kn, HEAD_DIM, 1), ext)


def _kvprep_body(r_ref, kvs_ref, wa_ref, wb_ref, pea_ref, peb_ref, g1_ref, g2_ref, g3_ref,
                 kce_ref, vct_ref, kse_ref, vst_ref, kwe_ref, vwt_ref):
    lane = lax.broadcasted_iota(jnp.int32, (1, LANES), 1)
    lo = lane < HEAD_DIM
    nrow = SEQ // CMP_STRIDE
    r = r_ref[0]
    a = jnp.dot((r + pea_ref[...]).astype(BF16), wa_ref[...], preferred_element_type=F32)
    b = jnp.dot((r + peb_ref[...]).astype(BF16), wb_ref[...], preferred_element_type=F32)
    c = a + pltpu.roll(b, nrow - 1, 0)
    end = lax.broadcasted_iota(jnp.int32, (nrow, 1), 0) * CMP_STRIDE + (CMP_LEN - 1)
    kc0, kc1 = _key_ext(_rms_pair(c[:, :LANES], g1_ref[...], lo), lo, lane, end)
    kce_ref[0, 0] = kc0.astype(BF16)
    kce_ref[0, 1] = kc1.astype(BF16)
    vct = c[:, LANES:].T.astype(BF16)
    vct_ref[0, 0] = vct[:HEAD_DIM]
    vct_ref[0, 1] = vct[HEAD_DIM:]

    rows = 256

    def chunk(i, carry):
        r0 = pl.multiple_of(i * rows, rows)
        blk = kvs_ref[pl.ds(r0, rows), :]
        pos = r0 + lax.broadcasted_iota(jnp.int32, (rows, 1), 0)
        for k_ref, v_ref, gain, off in ((kse_ref, vst_ref, g2_ref, 0), (kwe_ref, vwt_ref, g3_ref, 2 * LANES)):
            k0, k1 = _key_ext(_rms_pair(blk[:, off:off + LANES], gain[...], lo), lo, lane, pos)
            k_ref[0, 0, pl.ds(r0, rows), :] = k0.astype(BF16)
            k_ref[0, 1, pl.ds(r0, rows), :] = k1.astype(BF16)
            vt = blk[:, off + LANES:off + 2 * LANES].T.astype(BF16)
            v_ref[0, 0, :, pl.ds(r0, rows)] = vt[:HEAD_DIM]
            v_ref[0, 1, :, pl.ds(r0, rows)] = vt[HEAD_DIM:]
        return carry

    lax.fori_loop(0, SEQ // rows, chunk, 0)


def _kvprep(rmat, kvs, wa, wb, pea, peb, g1, g2, g3):
    nrow = SEQ // CMP_STRIDE
    const2 = lambda b: (0, 0)
    per_b = lambda b: (b, 0, 0, 0)
    shapes = [(nrow, LANES), (HEAD_DIM, nrow), (SEQ, LANES), (HEAD_DIM, SEQ), (SEQ, LANES), (HEAD_DIM, SEQ)]
    return pl.pallas_call(
        _kvprep_body,
        grid=(BATCH,),
        in_specs=[pl.BlockSpec((1, nrow, rmat.shape[2]), lambda b: (b, 0, 0)),
                  pl.BlockSpec((SEQ, kvs.shape[1]), lambda b: (b, 0)),
                  pl.BlockSpec(wa.shape, const2), pl.BlockSpec(wb.shape, const2),
                  pl.BlockSpec(pea.shape, const2), pl.BlockSpec(peb.shape, const2),
                  pl.BlockSpec((1, LANES), const2), pl.BlockSpec((1, LANES), const2),
                  pl.BlockSpec((1, LANES), const2)],
        out_specs=[pl.BlockSpec((1, NSA_GROUPS) + s, per_b) for s in shapes],
        out_shape=[jax.ShapeDtypeStruct((BATCH, NSA_GROUPS) + s, BF16) for s in shapes],
        compiler_params=_cparams(("parallel",)),
        name="kv_prep",
    )(rmat, kvs, wa, wb, pea, peb, g1, g2, g3)


def _split3(x):
    p1 = x.astype(BF16)
    r1 = x - p1.astype(F32)
    p2 = r1.astype(BF16)
    p3 = (r1 - p2.astype(F32)).astype(BF16)
    return p1, p2, p3


def _nsa_body(q_ref, gl_ref, kce_ref, vct_ref, kse_ref, vst_ref, kwe_ref, vwt_ref,
              g0_ref, ovt_ref, et_ref, o_ref, msk_scr, s_scr):
    g = pl.program_id(1)
    qi = pl.program_id(2)
    q0 = qi * TQ
    lane = lax.broadcasted_iota(jnp.int32, (1, LANES), 1)
    lo = lane < HEAD_DIM
    t_idx = q0 + lax.broadcasted_iota(jnp.int32, (1, TQ), 1)
    q0f = q0.astype(F32)

    qe = []
    for p in range(GQA_REP // 2):
        qn = _rms_pair(q_ref[:, p * LANES:(p + 1) * LANES], g0_ref[...], lo) * (HEAD_DIM ** -0.5)
        for half, base in ((0, qn), (1, pltpu.roll(qn, HEAD_DIM, 1))):
            r = 2 * p + half
            slope = jnp.where(g == 0, 2.0 ** -(r + 1), 2.0 ** -(r + 1 + GQA_REP)).astype(F32)
            ext = jnp.where(lane == HEAD_DIM, SEL_LEN * slope,
                            jnp.where(lane == HEAD_DIM + 1, slope,
                                      jnp.where(lane == HEAD_DIM + 2, -slope * q0f, 0.0)))
            qe.append(jnp.where(lo, base, ext).astype(BF16))

    n_sub = lax.broadcasted_iota(jnp.int32, (SEQ // CMP_STRIDE, 1), 0)
    cmask = (t_idx >= n_sub * CMP_STRIDE + (CMP_LEN - 1)) & (n_sub < N_CMP)
    kce = kce_ref[0, 0]
    vct = vct_ref[0, 0]
    psum = jnp.zeros((SEQ // CMP_STRIDE, TQ), F32)
    o_cmp = []
    for r in range(GQA_REP):
        s = jnp.where(cmask, lax.dot_general(kce, qe[r], _NT, preferred_element_type=F32), NEG_INF)
        e = jnp.where(cmask, jnp.exp(s - jnp.max(s, axis=0, keepdims=True)), 0.0)
        l = jnp.sum(e, axis=0, keepdims=True)
        p = e / jnp.where(l > 0.0, l, 1.0)
        psum = psum + p
        o_cmp.append(jnp.dot(vct, p.astype(BF16), preferred_element_type=F32))

    imp = jnp.zeros((LANES, TQ), F32)
    for part in _split3(psum):
        imp = imp + jnp.dot(ovt_ref[...], part, preferred_element_type=F32)
    imp = imp[:N_SEL]
    j = lax.broadcasted_iota(jnp.int32, (N_SEL, 1), 0)
    tb = t_idx >> 6
    forced = (j == 0) | (j == tb) | (j == tb - 1)
    imp = jnp.where(j <= tb, imp + jnp.where(forced, FORCE_BONUS, 0.0), -1.0)
    rank = jnp.zeros((N_SEL, TQ), F32)
    for i in range(N_SEL):
        ci = imp[i:i + 1, :]
        ahead = (ci > imp) | ((ci == imp) & (j > i))
        rank = rank + jnp.where(ahead, 1.0, 0.0)
    sel = jnp.concatenate([jnp.where(rank < float(SEL_TOPN), 1.0, 0.0),
                           jnp.zeros((LANES - N_SEL, TQ), F32)], axis=0).astype(BF16)
    msk_scr[...] = jnp.dot(et_ref[...], sel, preferred_element_type=F32)

    start = pl.multiple_of(jnp.maximum(q0 - WINDOW, 0), LANES)
    kw = kwe_ref[0, 0, pl.ds(start, WIN_KEYS), :]
    vwt = vwt_ref[0, 0, :, pl.ds(start, WIN_KEYS)]
    wd = t_idx - (start + lax.broadcasted_iota(jnp.int32, (WIN_KEYS, 1), 0))
    wbias = jnp.where((wd >= 0) & (wd < WINDOW), 0.0, NEG_INF)
    o_win = []
    for r in range(GQA_REP):
        s = lax.dot_general(kw, qe[r], _NT, preferred_element_type=F32) + wbias
        e = jnp.exp(s - jnp.max(s, axis=0, keepdims=True))
        l = jnp.sum(e, axis=0, keepdims=True)
        o_win.append(jnp.dot(vwt, e.astype(BF16), preferred_element_type=F32) / l)

    key_sub = lax.broadcasted_iota(jnp.int32, (TK, 1), 0)
    n_kb = (q0 + TQ - 1) // TK + 1

    def score_step(kb, ms):
        k0 = pl.multiple_of(kb * TK, TK)
        kblk = kse_ref[0, 0, pl.ds(k0, TK), :]
        allowed = (msk_scr[pl.ds(k0, TK), :] > 0.5) & (k0 + key_sub <= t_idx)
        bias = jnp.where(allowed, 0.0, NEG_INF)
        out = []
        for r in range(GQA_REP):
            s = lax.dot_general(kblk, qe[r], _NT, preferred_element_type=F32) + bias
            s_scr[r, pl.ds(k0, TK), :] = s
            out.append(jnp.maximum(ms[r], jnp.max(s, axis=0, keepdims=True)))
        return tuple(out)

    ms = lax.fori_loop(0, n_kb, score_step,
                       tuple(jnp.full((1, TQ), NEG_INF, F32) for _ in range(GQA_REP)))

    def value_step(kb, carry):
        ls, accs = carry
        k0 = pl.multiple_of(kb * TK, TK)
        vt = vst_ref[0, 0, :, pl.ds(k0, TK)]
        nl, nacc = [], []
        for r in range(GQA_REP):
            e = jnp.exp(s_scr[r, pl.ds(k0, TK), :] - ms[r])
            nl.append(ls[r] + jnp.sum(e, axis=0, keepdims=True))
            nacc.append(accs[r] + jnp.dot(vt, e.astype(BF16), preferred_element_type=F32))
        return tuple(nl), tuple(nacc)

    ls, accs = lax.fori_loop(0, n_kb, value_step,
                             (tuple(jnp.zeros((1, TQ), F32) for _ in range(GQA_REP)),
                              tuple(jnp.zeros((HEAD_DIM, TQ), F32) for _ in range(GQA_REP))))
    o_slc = [accs[r] / ls[r] for r in range(GQA_REP)]

    sg = jax.nn.sigmoid(gl_ref[...].T[:4 * GQA_REP])
    for p in range(GQA_REP // 2):
        pair = []
        for r in (2 * p, 2 * p + 1):
            pair.append(sg[3 * r:3 * r + 1] * o_cmp[r] + sg[3 * r + 1:3 * r + 2] * o_slc[r]
                        + sg[3 * r + 2:3 * r + 3] * o_win[r])
        o_ref[:, p * LANES:(p + 1) * LANES] = jnp.concatenate(pair, axis=0).T.astype(o_ref.dtype)


def _nsa(q, gl, kce, vct, kse, vst, kwe, vwt, g0, ovt, et):
    nq = SEQ // TQ
    gw = GQA_REP * HEAD_DIM
    tile = lambda b, g, i: (b * nq + i, g)
    per_bg = lambda b, g, i: (b, g, 0, 0)
    const2 = lambda b, g, i: (0, 0)
    return pl.pallas_call(
        _nsa_body,
        grid=(BATCH, NSA_GROUPS, nq),
        in_specs=[pl.BlockSpec((TQ, gw), tile), pl.BlockSpec((TQ, LANES), tile)]
                 + [pl.BlockSpec((1, 1) + a.shape[2:], per_bg) for a in (kce, vct, kse, vst, kwe, vwt)]
                 + [pl.BlockSpec((1, LANES), const2), pl.BlockSpec(ovt.shape, const2),
                    pl.BlockSpec(et.shape, const2)],
        out_specs=pl.BlockSpec((TQ, gw), tile),
        out_shape=jax.ShapeDtypeStruct((TOKENS, NSA_WIDTH), BF16),
        scratch_shapes=[pltpu.VMEM((SEQ, TQ), F32), pltpu.VMEM((GQA_REP, SEQ, TQ), F32)],
        compiler_params=_cparams(("parallel", "parallel", "arbitrary")),
        name="nsa_attention",
    )(q, gl, kce, vct, kse, vst, kwe, vwt, g0, ovt, et)


def _conv_body(glu_ref, w_ref, b_ref, lg_ref, lb_ref, o_ref, u_scr, sh_scr):
    u_scr[0:CONV_PAD, :] = jnp.zeros((CONV_PAD, CONV_WIDTH), F32)
    rows = 256

    def fill(i, carry):
        r0 = pl.multiple_of(i * rows, rows)
        blk = glu_ref[pl.ds(r0, rows), :]
        u_scr[pl.ds(CONV_PAD + r0, rows), :] = blk[:, :CONV_WIDTH] * jax.nn.sigmoid(blk[:, CONV_WIDTH:])
        return carry

    lax.fori_loop(0, SEQ // rows, fill, 0)
    first = CONV_PAD - (CONV_TAPS - 1)

    def tile(i, carry):
        r0 = pl.multiple_of(i * TS_CONV, TS_CONV)
        win = u_scr[pl.ds(r0, TS_CONV + CONV_PAD), :]
        span = TS_CONV + CONV_PAD - 8
        for s in range(1, 8):
            sh_scr[s - 1] = win[s:s + span, :]
        acc = jnp.zeros((TS_CONV, CONV_WIDTH), F32) + b_ref[...]
        for k in range(CONV_TAPS):
            s = (first + k) % 8
            base = first + k - s
            tap = win[base:base + TS_CONV, :] if s == 0 else sh_scr[s - 1, base:base + TS_CONV, :]
            acc = acc + tap * w_ref[k:k + 1, :]
        mu = jnp.mean(acc, axis=-1, keepdims=True)
        d = acc - mu
        var = jnp.mean(d * d, axis=-1, keepdims=True)
        yn = d * lax.rsqrt(var + NORM_EPS) * lg_ref[...] + lb_ref[...]
        o_ref[pl.ds(r0, TS_CONV), :] = (yn * jax.nn.sigmoid(yn)).astype(o_ref.dtype)
        return carry

    lax.fori_loop(0, SEQ // TS_CONV, tile, 0)


def _conv(glu, w, b, lg, lb):
    const2 = lambda i: (0, 0)
    return pl.pallas_call(
        _conv_body,
        grid=(BATCH,),
        in_specs=[pl.BlockSpec((SEQ, 2 * CONV_WIDTH), lambda i: (i, 0)),
                  pl.BlockSpec(w.shape, const2), pl.BlockSpec(b.shape, const2),
                  pl.BlockSpec(lg.shape, const2), pl.BlockSpec(lb.shape, const2)],
        out_specs=pl.BlockSpec((SEQ, CONV_WIDTH), lambda i: (i, 0)),
        out_shape=jax.ShapeDtypeStruct((TOKENS, CONV_WIDTH), BF16),
        scratch_shapes=[pltpu.VMEM((CONV_PAD + SEQ, CONV_WIDTH), F32),
                        pltpu.VMEM((7, TS_CONV + CONV_PAD - 8, CONV_WIDTH), F32)],
        compiler_params=_cparams(("parallel",)),
        name="conv_mixer",
    )(glu, w, b, lg, lb)


def _outproj_body(a_ref, c_ref, x_ref, mod_ref, g_ref, wa_ref, wc_ref, x1_ref, h2_ref):
    mix = (jnp.dot(a_ref[...], wa_ref[...], preferred_element_type=F32)
           + jnp.dot(c_ref[...], wc_ref[...], preferred_element_type=F32))
    x1 = x_ref[...] + mod_ref[0, 2:3, :] * mix
    x1_ref[...] = x1
    h2_ref[...] = _norm_mod(x1, g_ref[...], mod_ref[0, 3:4, :], mod_ref[0, 4:5, :]).astype(BF16)


def _outproj(a, c, xf, mod3, g, wa, wc):
    tiles_per_batch = SEQ // TM_PROJ
    row = lambda i: (i, 0)
    const2 = lambda i: (0, 0)
    return pl.pallas_call(
        _outproj_body,
        grid=(TOKENS // TM_PROJ,),
        in_specs=[pl.BlockSpec((TM_PROJ, NSA_WIDTH), row),
                  pl.BlockSpec((TM_PROJ, CONV_WIDTH), row),
                  pl.BlockSpec((TM_PROJ, D_MODEL), row),
                  pl.BlockSpec((1, 6, D_MODEL), lambda i: (i // tiles_per_batch, 0, 0)),
                  pl.BlockSpec((1, D_MODEL), const2),
                  pl.BlockSpec(wa.shape, const2), pl.BlockSpec(wc.shape, const2)],
        out_specs=[pl.BlockSpec((TM_PROJ, D_MODEL), row), pl.BlockSpec((TM_PROJ, D_MODEL), row)],
        out_shape=[jax.ShapeDtypeStruct((TOKENS, D_MODEL), F32),
                   jax.ShapeDtypeStruct((TOKENS, D_MODEL), BF16)],
        compiler_params=_cparams(("parallel",)),
        name="out_proj",
    )(a, c, xf, mod3, g, wa, wc)


_SORT4 = ((0, 1), (2, 3), (0, 2), (1, 3), (1, 2))


def _topk_rows(x, k):
    n, cols = x.shape
    q = n // 4
    row = lax.broadcasted_iota(jnp.int32, (q, cols), 0).astype(F32)
    vals = [x[i * q:(i + 1) * q] for i in range(4)]
    idxs = [row + float(i * q) for i in range(4)]
    for i, j in _SORT4:
        swap = (vals[j] > vals[i]) | ((vals[j] == vals[i]) & (idxs[j] < idxs[i]))
        vals[i], vals[j] = jnp.where(swap, vals[j], vals[i]), jnp.where(swap, vals[i], vals[j])
        idxs[i], idxs[j] = jnp.where(swap, idxs[j], idxs[i]), jnp.where(swap, idxs[i], idxs[j])
    slot = lax.broadcasted_iota(jnp.int32, (k, cols), 0)
    out_v = jnp.zeros((k, cols), F32)
    out_i = jnp.zeros((k, cols), F32)
    for it in range(k):
        m = jnp.max(vals[0], axis=0, keepdims=True)
        idx = jnp.min(jnp.where(vals[0] == m, idxs[0], IDX_BIG), axis=0, keepdims=True)
        hit = idxs[0] == idx
        for lvl in range(3):
            vals[lvl] = jnp.where(hit, vals[lvl + 1], vals[lvl])
            idxs[lvl] = jnp.where(hit, idxs[lvl + 1], idxs[lvl])
        vals[3] = jnp.where(hit, NEG_BIG, vals[3])
        out_v = jnp.where(slot == it, m, out_v)
        out_i = jnp.where(slot == it, idx, out_i)
    return out_v, out_i


def _pair_topk(v1, v2):
    k, cols = v1.shape
    half = k // 2
    alo = lax.broadcasted_iota(jnp.int32, (half, cols), 0).astype(F32)
    ahi = alo + float(half)
    levels = []
    for b in range(k):
        lvl = v1[:half] + jnp.broadcast_to(v2[b:b + 1, :], (half, cols))
        if k // (b + 1) < half:
            lvl = jnp.where(alo < float(k // (b + 1)), lvl, NEG_BIG)
        levels.append(lvl)
    top_hi = v1[half:] + jnp.broadcast_to(v2[0:1, :], (half, cols))
    depth = jnp.zeros((half, cols), F32)
    slot = lax.broadcasted_iota(jnp.int32, (k, cols), 0)
    tops = jnp.zeros((k, cols), F32)
    a_out = jnp.zeros((k, cols), F32)
    b_out = jnp.zeros((k, cols), F32)
    for it in range(k):
        m = jnp.max(jnp.maximum(levels[0], top_hi), axis=0, keepdims=True)
        a_sel = jnp.min(jnp.minimum(jnp.where(levels[0] == m, alo, IDX_BIG),
                                    jnp.where(top_hi == m, ahi, IDX_BIG)), axis=0, keepdims=True)
        hit = alo == a_sel
        b_sel = jnp.sum(jnp.where(hit, depth, 0.0), axis=0, keepdims=True)
        depth = jnp.where(hit, depth + 1.0, depth)
        for b in range(k - 1):
            levels[b] = jnp.where(hit, levels[b + 1], levels[b])
        levels[k - 1] = jnp.where(hit, NEG_BIG, levels[k - 1])
        top_hi = jnp.where(ahi == a_sel, NEG_BIG, top_hi)
        tops = jnp.where(slot == it, m, tops)
        a_out = jnp.where(slot == it, a_sel, a_out)
        b_out = jnp.where(slot == it, b_sel, b_out)
    return tops, a_out, b_out


def _route_body(h_ref, wq_ref, keys_ref, i1_ref, i2_ref, gw_ref, qp_scr, s1_scr, s2_scr, sw_scr):
    qp_scr[...] = jnp.dot(h_ref[...], wq_ref[...], preferred_element_type=F32)
    k = PEER_TOPK

    def head(h, carry):
        c0 = pl.multiple_of(h * PEER_QDIM, PEER_QDIM)
        tv, ti = [], []
        for c in range(2):
            qs = qp_scr[:, pl.ds(c0 + c * LANES, LANES)].astype(BF16)
            st = lax.dot_general(keys_ref[2 * h + c], qs, _NT, preferred_element_type=F32)
            v, i = _topk_rows(st, k)
            tv.append(v)
            ti.append(i)
        tops, a_sel, b_sel = _pair_topk(tv[0], tv[1])
        i1s = jnp.zeros_like(tops)
        i2s = jnp.zeros_like(tops)
        for a in range(k):
            i1s = i1s + jnp.where(a_sel == float(a), jnp.broadcast_to(ti[0][a:a + 1, :], tops.shape), 0.0)
            i2s = i2s + jnp.where(b_sel == float(a), jnp.broadcast_to(ti[1][a:a + 1, :], tops.shape), 0.0)
        e = jnp.exp(tops - jnp.max(tops, axis=0, keepdims=True))
        w = e / jnp.sum(e, axis=0, keepdims=True)
        r0 = pl.multiple_of(h * k, k)
        s1_scr[pl.ds(r0, k), :] = i1s
        s2_scr[pl.ds(r0, k), :] = i2s
        sw_scr[pl.ds(r0, k), :] = w
        return carry

    lax.fori_loop(0, PEER_HEADS, head, 0, unroll=2)
    i1_ref[...] = s1_scr[...].T
    i2_ref[...] = s2_scr[...].T
    gw_ref[...] = sw_scr[...].T


def _route(h2, wq, keys):
    nsel = PEER_HEADS * PEER_TOPK
    row = lambda i: (i, 0)
    return pl.pallas_call(
        _route_body,
        grid=(TOKENS // TL_ROUTE,),
        in_specs=[pl.BlockSpec((TL_ROUTE, D_MODEL), row),
                  pl.BlockSpec(wq.shape, lambda i: (0, 0)),
                  pl.BlockSpec(keys.shape, lambda i: (0, 0, 0))],
        out_specs=[pl.BlockSpec((TL_ROUTE, nsel), row)] * 3,
        out_shape=[jax.ShapeDtypeStruct((TOKENS, nsel), F32)] * 3,
        scratch_shapes=[pltpu.VMEM((TL_ROUTE, PEER_HEADS * PEER_QDIM), F32)]
                       + [pltpu.VMEM((nsel, TL_ROUTE), F32)] * 3,
        compiler_params=_cparams(("parallel",)),
        name="peer_route",
    )(h2, wq, keys)


def _peer_body(h_ref, u_ref, v_ref, i1_ref, i2_ref, gw_ref, x1_ref, mod_ref, o_ref, g_scr):
    e = pl.program_id(1)
    n = PEER_NKEYS

    @pl.when(e == 0)
    def _():
        o_ref[...] = jnp.zeros(o_ref.shape, F32)
        sub = lax.broadcasted_iota(jnp.int32, (n, n), 0).astype(F32)

        def tok(t, carry):
            w = gw_ref[pl.ds(t, 1), :]
            w_hi = w.astype(BF16).astype(F32)
            m1 = sub == i1_ref[pl.ds(t, 1), :]
            x1 = jnp.concatenate([jnp.where(m1, w_hi, 0.0), jnp.where(m1, w - w_hi, 0.0)], axis=1).astype(BF16)
            x2h = jnp.where(sub == i2_ref[pl.ds(t, 1), :], 1.0, 0.0)
            x2 = jnp.concatenate([x2h, x2h], axis=1).astype(BF16)
            g = lax.dot_general(x1, x2, _NT, preferred_element_type=F32)
            g_scr[pl.ds(pl.multiple_of(t * G_ROW_STRIDE, 8), n), :] = g
            return carry

        lax.fori_loop(0, TT_PEER, tok, 0, unroll=16)

    h = h_ref[...]
    parts = []
    for s in range(ET_PEER // ES_PEER):
        z = lax.dot_general(h, u_ref[s * ES_PEER:(s + 1) * ES_PEER, :], _NT,
                            preferred_element_type=F32)
        act = 0.5 * z * (1.0 + lax.erf(z * (2.0 ** -0.5)))
        for c in range(ES_PEER // n):
            i1 = e * (ET_PEER // n) + s * (ES_PEER // n) + c
            gc = g_scr[pl.ds(i1, TT_PEER, stride=G_ROW_STRIDE), :]
            parts.append((act[:, c * n:(c + 1) * n] * gc).astype(BF16))
    o_ref[...] += jnp.dot(jnp.concatenate(parts, axis=1), v_ref[...], preferred_element_type=F32)

    @pl.when(e == pl.num_programs(1) - 1)
    def _():
        o_ref[...] = x1_ref[...] + mod_ref[0, 5:6, :] * o_ref[...]


def _peer(h2, u, v, i1, i2, gw, x1, mod3):
    tiles_per_batch = SEQ // TT_PEER
    nsel = PEER_HEADS * PEER_TOPK
    tok = lambda i, e: (i, 0)
    exp = lambda i, e: (e, 0)
    once = pl.Buffered(1)
    return pl.pallas_call(
        _peer_body,
        grid=(TOKENS // TT_PEER, PEER_EXPERTS // ET_PEER),
        in_specs=[pl.BlockSpec((TT_PEER, D_MODEL), tok, pipeline_mode=once),
                  pl.BlockSpec((ET_PEER, D_MODEL), exp),
                  pl.BlockSpec((ET_PEER, D_MODEL), exp),
                  pl.BlockSpec((TT_PEER, nsel), tok, pipeline_mode=once),
                  pl.BlockSpec((TT_PEER, nsel), tok, pipeline_mode=once),
                  pl.BlockSpec((TT_PEER, nsel), tok, pipeline_mode=once),
                  pl.BlockSpec((TT_PEER, D_MODEL), tok, pipeline_mode=once),
                  pl.BlockSpec((1, 6, D_MODEL), lambda i, e: (i // tiles_per_batch, 0, 0))],
        out_specs=pl.BlockSpec((TT_PEER, D_MODEL), tok),
        out_shape=jax.ShapeDtypeStruct((TOKENS, D_MODEL), F32),
        scratch_shapes=[pltpu.VMEM((TT_PEER * G_ROW_STRIDE, PEER_NKEYS), F32)],
        compiler_params=pltpu.CompilerParams(dimension_semantics=("parallel", "arbitrary"),
                                             vmem_limit_bytes=VMEM_LIMIT_PEER),
        name="peer_experts",
    )(h2, u, v, i1, i2, gw, x1, mod3)


def _overlap_matrix():
    start = np.arange(N_CMP)[None, :] * CMP_STRIDE
    sel = np.arange(N_SEL)[:, None] * SEL_LEN
    ov = np.clip(np.minimum(start + CMP_LEN, sel + SEL_LEN) - np.maximum(start, sel), 0, None) / CMP_LEN
    out = np.zeros((LANES, LANES), np.float32)
    out[:N_SEL, :N_CMP] = ov
    return out


def _block_expand_matrix():
    out = np.zeros((SEQ, LANES), np.float32)
    out[np.arange(SEQ), np.arange(SEQ) // SEL_LEN] = 1.0
    return out


def _cmp_weights(wk, wv, first):
    width = 2 * NSA_GROUPS * HEAD_DIM
    blocks = [w[first:first + CMP_STRIDE].astype(BF16) for w in (wk, wv) for _ in range(NSA_GROUPS)]
    zero = jnp.zeros_like(blocks[0])
    rows = [jnp.concatenate([blk if i == j else zero for j in range(len(blocks))], axis=2)
            for i, blk in enumerate(blocks)]
    return jnp.concatenate(rows, axis=1).reshape(CMP_STRIDE * width, width)


def _dup(v):
    return jnp.concatenate([v, v])[None, :]


def kernel(x, c, w_ada, b_ada, norm_g, w_in, w_out, cmp_pe_k, cmp_pe_v, w_cmp_k, w_cmp_v, qk_norm_g,
           dw_w, dw_b, conv_ln_g, conv_ln_b, peer_wq, peer_sub_keys, peer_u, peer_v):
    assert x.shape == (BATCH, SEQ, D_MODEL) and w_ada.shape[0] == DEPTH
    ovt = jnp.asarray(_overlap_matrix(), BF16)
    emat = jnp.asarray(_block_expand_matrix(), BF16)
    o_kv = NSA_WIDTH
    o_gate = o_kv + 6 * NSA_GROUPS * HEAD_DIM
    o_glu = o_gate + 3 * NSA_HEADS
    n_cmp_cols = 2 * NSA_GROUPS * HEAD_DIM
    xf = x.reshape(TOKENS, D_MODEL)
    for l in range(DEPTH):
        mod3 = _ada(c, w_ada[l], b_ada[l][None, :]).reshape(BATCH, 6, D_MODEL)
        wi = w_in[l]
        gate_pad = jnp.zeros((D_MODEL, LANES - 3 * GQA_REP), F32)
        wg = jnp.concatenate(
            [part for g in range(NSA_GROUPS)
             for part in (wi[:, o_gate + 3 * GQA_REP * g:o_gate + 3 * GQA_REP * (g + 1)], gate_pad)], axis=1)
        q, kvc, kvs, gl, glu = _inproj(
            xf, mod3, norm_g[l, 0][None, :],
            wi[:, :o_kv].astype(BF16), wi[:, o_kv:o_kv + n_cmp_cols].astype(BF16),
            wi[:, o_kv + n_cmp_cols:o_gate].astype(BF16), wg.astype(BF16), wi[:, o_glu:].astype(BF16))
        pe = jnp.concatenate([cmp_pe_k[l], cmp_pe_k[l], cmp_pe_v[l], cmp_pe_v[l]], axis=1)
        kv_parts = _kvprep(
            kvc.reshape(BATCH, SEQ // CMP_STRIDE, CMP_STRIDE * n_cmp_cols), kvs,
            _cmp_weights(w_cmp_k[l], w_cmp_v[l], 0), _cmp_weights(w_cmp_k[l], w_cmp_v[l], CMP_STRIDE),
            pe[:CMP_STRIDE].reshape(1, -1), pe[CMP_STRIDE:].reshape(1, -1),
            _dup(qk_norm_g[l, 1]), _dup(qk_norm_g[l, 2]), _dup(qk_norm_g[l, 3]))
        a_out = _nsa(q, gl, *kv_parts, _dup(qk_norm_g[l, 0]), ovt, emat)
        c_out = _conv(glu, dw_w[l], dw_b[l][None, :], conv_ln_g[l][None, :], conv_ln_b[l][None, :])
        x1, h2 = _outproj(a_out, c_out, xf, mod3, norm_g[l, 1][None, :],
                          w_out[l, :NSA_WIDTH].astype(BF16), w_out[l, NSA_WIDTH:].astype(BF16))
        i1, i2, gw = _route(h2, peer_wq[l].astype(BF16),
                            peer_sub_keys[l].reshape(2 * PEER_HEADS, PEER_NKEYS, PEER_QDIM // 2).astype(BF16))
        xf = _peer(h2, peer_u[l].astype(BF16), peer_v[l].astype(BF16), i1, i2, gw, x1, mod3)
    return xf.reshape(BATCH, SEQ, D_MODEL)
```

```python
import functools

import numpy as np
import jax
import jax.numpy as jnp
from jax import lax
from jax.experimental import pallas as pl
from jax.experimental.pallas import tpu as pltpu

F32 = jnp.float32
BF16 = jnp.bfloat16

D_MODEL = 1024
BATCH = 8
SEQ = 2048
DEPTH = 1
TOKENS = BATCH * SEQ

HEAD_DIM = 64
NSA_HEADS = 8
NSA_GROUPS = 2
GQA_REP = NSA_HEADS // NSA_GROUPS
NSA_WIDTH = NSA_HEADS * HEAD_DIM
CONV_WIDTH = 512
CMP_LEN = 32
CMP_STRIDE = 16
N_CMP = (SEQ - CMP_LEN) // CMP_STRIDE + 1
SEL_LEN = 64
N_SEL = SEQ // SEL_LEN
SEL_TOPN = 16
WINDOW = 512
FORCE_BONUS = 1.0e4
CONV_TAPS = 31
PEER_HEADS = 8
PEER_NKEYS = 128
PEER_EXPERTS = PEER_NKEYS * PEER_NKEYS
PEER_QDIM = 256
PEER_TOPK = 16
NORM_EPS = 1e-6
NEG_INF = -1e30
NEG_BIG = -3.0e38
IDX_BIG = 1.0e9

LANES = 128
VMEM_V7X = 64 * 1024 * 1024
VMEM_LIMIT = VMEM_V7X * 3 // 4
VMEM_LIMIT_PEER = VMEM_V7X * 7 // 8

TM_PROJ = 512
TQ = 256
TK = 256
WIN_KEYS = WINDOW + TQ
V_ROWS = HEAD_DIM + 16
TS_CONV = 64
CONV_PAD = 32
TL_ROUTE = 256
TT_PEER = 512
ET_PEER = 1024
ES_PEER = 256
G_ROW_STRIDE = PEER_NKEYS + 8

_NT = (((1,), (1,)), ((), ()))


def _cparams(sem):
    return pltpu.CompilerParams(dimension_semantics=sem, vmem_limit_bytes=VMEM_LIMIT)


def _ada_body(c_ref, w_ref, b_ref, o_ref):
    c = c_ref[...]
    sc = (c * jax.nn.sigmoid(c)).astype(BF16)
    o_ref[...] = jnp.dot(sc, w_ref[...].astype(BF16), preferred_element_type=F32) + b_ref[...]


def _ada(c, w, b):
    n = w.shape[1]
    tn = 1536
    return pl.pallas_call(
        _ada_body,
        grid=(n // tn,),
        in_specs=[pl.BlockSpec((BATCH, D_MODEL), lambda j: (0, 0)),
                  pl.BlockSpec((D_MODEL, tn), lambda j: (0, j)),
                  pl.BlockSpec((1, tn), lambda j: (0, j))],
        out_specs=pl.BlockSpec((BATCH, tn), lambda j: (0, j)),
        out_shape=jax.ShapeDtypeStruct((BATCH, n), F32),
        compiler_params=_cparams(("arbitrary",)),
        name="ada_mod",
    )(c, w, b)


def _norm_mod(x, g, shift, scale):
    ms = jnp.mean(x * x, axis=-1, keepdims=True)
    y = x * lax.rsqrt(ms + NORM_EPS) * g
    return y * (1.0 + scale) + shift


def _inproj_body(x_ref, mod_ref, g_ref, wq_ref, wkc_ref, wks_ref, wg_ref, wglu_ref,
                 q_ref, kc_ref, ks_ref, gl_ref, glu_ref):
    h = _norm_mod(x_ref[...], g_ref[...], mod_ref[0, 0:1, :], mod_ref[0, 1:2, :]).astype(BF16)
    for w_ref, o_ref in ((wq_ref, q_ref), (wkc_ref, kc_ref), (wks_ref, ks_ref),
                         (wg_ref, gl_ref), (wglu_ref, glu_ref)):
        o_ref[...] = jnp.dot(h, w_ref[...], preferred_element_type=F32)


def _inproj(xf, mod3, g, wq, wkc, wks, wg, wglu):
    tiles_per_batch = SEQ // TM_PROJ
    ws = (wq, wkc, wks, wg, wglu)
    row = lambda i: (i, 0)
    return pl.pallas_call(
        _inproj_body,
        grid=(TOKENS // TM_PROJ,),
        in_specs=[pl.BlockSpec((TM_PROJ, D_MODEL), row),
                  pl.BlockSpec((1, 6, D_MODEL), lambda i: (i // tiles_per_batch, 0, 0)),
                  pl.BlockSpec((1, D_MODEL), lambda i: (0, 0))]
                 + [pl.BlockSpec(w.shape, lambda i: (0, 0)) for w in ws],
        out_specs=[pl.BlockSpec((TM_PROJ, w.shape[1]), row) for w in ws],
        out_shape=[jax.ShapeDtypeStruct((TOKENS, w.shape[1]), F32) for w in ws],
        compiler_params=_cparams(("parallel",)),
        name="in_proj",
    )(xf, mod3, g, *ws)


def _rms_pair(x, gdup, lo):
    x2 = x * x
    s_lo = jnp.sum(jnp.where(lo, x2, 0.0), axis=-1, keepdims=True)
    s_hi = jnp.sum(jnp.where(lo, 0.0, x2), axis=-1, keepdims=True)
    rs = jnp.where(lo, lax.rsqrt(s_lo * (1.0 / HEAD_DIM) + NORM_EPS),
                   lax.rsqrt(s_hi * (1.0 / HEAD_DIM) + NORM_EPS))
    return x * rs * gdup


def _key_ext(kn, lo, lane, pos):
    ext = jnp.where(lane == HEAD_DIM, (pos >> 6).astype(F32),
                    jnp.where(lane == HEAD_DIM + 1, (pos & (SEL_LEN - 1)).astype(F32),
                              jnp.where(lane == HEAD_DIM + 2, 1.0, 0.0)))
    return jnp.where(lo, kn, ext), jnp.where(lo, pltpu.roll(kn, HEAD_DIM, 1), ext)


def _kvprep_body(r_ref, kvs_ref, wa_ref, wb_ref, pea_ref, peb_ref, g1_ref, g2_ref, g3_ref,
                 kce_ref, vct_ref, kse_ref, vst_ref, kwe_ref, vwt_ref):
    lane = lax.broadcasted_iota(jnp.int32, (1, LANES), 1)
    lo = lane < HEAD_DIM
    nrow = SEQ // CMP_STRIDE
    r = r_ref[0]
    a = jnp.dot((r + pea_ref[...]).astype(BF16), wa_ref[...], preferred_element_type=F32)
    b = jnp.dot((r + peb_ref[...]).astype(BF16), wb_ref[...], preferred_element_type=F32)
    c = a + pltpu.roll(b, nrow - 1, 0)
    end = lax.broadcasted_iota(jnp.int32, (nrow, 1), 0) * CMP_STRIDE + (CMP_LEN - 1)
    kc0, kc1 = _key_ext(_rms_pair(c[:, :LANES], g1_ref[...], lo), lo, lane, end)
    kce_ref[0, 0] = kc0.astype(BF16)
    kce_ref[0, 1] = kc1.astype(BF16)
    vct = c[:, LANES:].T.astype(BF16)
    vct_ref[0, 0] = vct[:HEAD_DIM]
    vct_ref[0, 1] = vct[HEAD_DIM:]

    rows = 256
    ones = jnp.ones((V_ROWS - HEAD_DIM, rows), F32)

    def chunk(i, carry):
        r0 = pl.multiple_of(i * rows, rows)
        blk = kvs_ref[pl.ds(r0, rows), :]
        pos = r0 + lax.broadcasted_iota(jnp.int32, (rows, 1), 0)
        for k_ref, v_ref, gain, off in ((kse_ref, vst_ref, g2_ref, 0), (kwe_ref, vwt_ref, g3_ref, 2 * LANES)):
            k0, k1 = _key_ext(_rms_pair(blk[:, off:off + LANES], gain[...], lo), lo, lane, pos)
            k_ref[0, 0, pl.ds(r0, rows), :] = k0.astype(BF16)
            k_ref[0, 1, pl.ds(r0, rows), :] = k1.astype(BF16)
            vt = blk[:, off + LANES:off + 2 * LANES].T
            for g in range(NSA_GROUPS):
                v_ref[0, g, :, pl.ds(r0, rows)] = jnp.concatenate(
                    [vt[g * HEAD_DIM:(g + 1) * HEAD_DIM], ones], axis=0).astype(BF16)
        return carry

    lax.fori_loop(0, SEQ // rows, chunk, 0)


def _kvprep(rmat, kvs, wa, wb, pea, peb, g1, g2, g3):
    nrow = SEQ // CMP_STRIDE
    const2 = lambda b: (0, 0)
    per_b = lambda b: (b, 0, 0, 0)
    shapes = [(nrow, LANES), (HEAD_DIM, nrow), (SEQ, LANES), (V_ROWS, SEQ), (SEQ, LANES), (V_ROWS, SEQ)]
    return pl.pallas_call(
        _kvprep_body,
        grid=(BATCH,),
        in_specs=[pl.BlockSpec((1, nrow, rmat.shape[2]), lambda b: (b, 0, 0)),
                  pl.BlockSpec((SEQ, kvs.shape[1]), lambda b: (b, 0)),
                  pl.BlockSpec(wa.shape, const2), pl.BlockSpec(wb.shape, const2),
                  pl.BlockSpec(pea.shape, const2), pl.BlockSpec(peb.shape, const2),
                  pl.BlockSpec((1, LANES), const2), pl.BlockSpec((1, LANES), const2),
                  pl.BlockSpec((1, LANES), const2)],
        out_specs=[pl.BlockSpec((1, NSA_GROUPS) + s, per_b) for s in shapes],
        out_shape=[jax.ShapeDtypeStruct((BATCH, NSA_GROUPS) + s, BF16) for s in shapes],
        compiler_params=_cparams(("parallel",)),
        name="kv_prep",
    )(rmat, kvs, wa, wb, pea, peb, g1, g2, g3)


def _split3(x):
    p1 = x.astype(BF16)
    r1 = x - p1.astype(F32)
    p2 = r1.astype(BF16)
    p3 = (r1 - p2.astype(F32)).astype(BF16)
    return p1, p2, p3


def _nsa_body(q_ref, gl_ref, kce_ref, vct_ref, kse_ref, vst_ref, kwe_ref, vwt_ref,
              g0_ref, ovt_ref, et_ref, o_ref, s_scr):
    qi = pl.program_id(1)
    q0 = qi * TQ
    lane = lax.broadcasted_iota(jnp.int32, (1, LANES), 1)
    lo = lane < HEAD_DIM
    t_idx = q0 + lax.broadcasted_iota(jnp.int32, (1, TQ), 1)
    q0f = q0.astype(F32)

    n_sub = lax.broadcasted_iota(jnp.int32, (SEQ // CMP_STRIDE, 1), 0)
    cmask = (t_idx >= n_sub * CMP_STRIDE + (CMP_LEN - 1)) & (n_sub < N_CMP)
    j = lax.broadcasted_iota(jnp.int32, (N_SEL, 1), 0)
    tb = t_idx >> 6
    bonus = jnp.where((j == 0) | (j == tb) | (j == tb - 1), FORCE_BONUS, 0.0)
    start = pl.multiple_of(jnp.maximum(q0 - WINDOW, 0), LANES)
    wd = t_idx - (start + lax.broadcasted_iota(jnp.int32, (WIN_KEYS, 1), 0))
    wbias = jnp.where((wd >= 0) & (wd < WINDOW), 0.0, NEG_INF)

    qe, o_cmp, o_win, sels = [], [], [], []
    for g in range(NSA_GROUPS):
        for p in range(GQA_REP // 2):
            c0 = (g * GQA_REP // 2 + p) * LANES
            qn = _rms_pair(q_ref[:, c0:c0 + LANES], g0_ref[...], lo) * (HEAD_DIM ** -0.5)
            for half, base in ((0, qn), (1, pltpu.roll(qn, HEAD_DIM, 1))):
                slope = 2.0 ** -(g * GQA_REP + 2 * p + half + 1)
                ext = jnp.where(lane == HEAD_DIM, SEL_LEN * slope,
                                jnp.where(lane == HEAD_DIM + 1, slope,
                                          jnp.where(lane == HEAD_DIM + 2, -slope * q0f, 0.0)))
                qe.append(jnp.where(lo, base, ext).astype(BF16))
        heads = range(g * GQA_REP, (g + 1) * GQA_REP)

        kce = kce_ref[0, g]
        vct = vct_ref[0, g]
        psum = jnp.zeros((SEQ // CMP_STRIDE, TQ), F32)
        for h in heads:
            s = jnp.where(cmask, lax.dot_general(kce, qe[h], _NT, preferred_element_type=F32), NEG_INF)
            e = jnp.where(cmask, jnp.exp(s - jnp.max(s, axis=0, keepdims=True)), 0.0)
            l = jnp.sum(e, axis=0, keepdims=True)
            p = e / jnp.where(l > 0.0, l, 1.0)
            psum = psum + p
            o_cmp.append(jnp.dot(vct, p.astype(BF16), preferred_element_type=F32))

        imp = jnp.zeros((LANES, TQ), F32)
        for part in _split3(psum):
            imp = imp + jnp.dot(ovt_ref[...], part, preferred_element_type=F32)
        imp = jnp.where(j <= tb, imp[:N_SEL] + bonus, -1.0)
        rank = jnp.zeros((N_SEL, TQ), F32)
        for i in range(N_SEL):
            ci = imp[i:i + 1, :]
            ahead = (ci > imp) | ((ci == imp) & (j > i))
            rank = rank + jnp.where(ahead, 1.0, 0.0)
        sels.append(jnp.concatenate([jnp.where(rank < float(SEL_TOPN), 1.0, 0.0),
                                     jnp.zeros((LANES - N_SEL, TQ), F32)], axis=0).astype(BF16))

        kw = kwe_ref[0, g, pl.ds(start, WIN_KEYS), :]
        vwt = vwt_ref[0, g, :, pl.ds(start, WIN_KEYS)]
        for h in heads:
            s = lax.dot_general(kw, qe[h], _NT, preferred_element_type=F32) + wbias
            e = jnp.exp(s - jnp.max(s, axis=0, keepdims=True))
            ov = jnp.dot(vwt, e.astype(BF16), preferred_element_type=F32)
            o_win.append(ov[:HEAD_DIM] / ov[HEAD_DIM:HEAD_DIM + 1])

    key_sub = lax.broadcasted_iota(jnp.int32, (TK, 1), 0)
    n_kb = (q0 + TQ - 1) // TK + 1

    def score_step(kb, ms):
        k0 = pl.multiple_of(kb * TK, TK)
        causal = k0 + key_sub <= t_idx
        expand = et_ref[pl.ds(k0, TK), :]
        out = []
        for g in range(NSA_GROUPS):
            kblk = kse_ref[0, g, pl.ds(k0, TK), :]
            chosen = jnp.dot(expand, sels[g], preferred_element_type=F32) > 0.5
            bias = jnp.where(chosen & causal, 0.0, NEG_INF)
            for h in range(g * GQA_REP, (g + 1) * GQA_REP):
                s = lax.dot_general(kblk, qe[h], _NT, preferred_element_type=F32) + bias
                s_scr[h, pl.ds(k0, TK), :] = s
                out.append(jnp.maximum(ms[h], jnp.max(s, axis=0, keepdims=True)))
        return tuple(out)

    ms = lax.fori_loop(0, n_kb, score_step,
                       tuple(jnp.full((1, TQ), NEG_INF, F32) for _ in range(NSA_HEADS)))

    def value_step(kb, accs):
        k0 = pl.multiple_of(kb * TK, TK)
        out = []
        for g in range(NSA_GROUPS):
            vt = vst_ref[0, g, :, pl.ds(k0, TK)]
            for h in range(g * GQA_REP, (g + 1) * GQA_REP):
                e = jnp.exp(s_scr[h, pl.ds(k0, TK), :] - ms[h])
                out.append(accs[h] + jnp.dot(vt, e.astype(BF16), preferred_element_type=F32))
        return tuple(out)

    accs = lax.fori_loop(0, n_kb, value_step,
                         tuple(jnp.zeros((V_ROWS, TQ), F32) for _ in range(NSA_HEADS)))
    o_slc = [a[:HEAD_DIM] / a[HEAD_DIM:HEAD_DIM + 1] for a in accs]

    for g in range(NSA_GROUPS):
        sg = jax.nn.sigmoid(gl_ref[:, g * LANES:(g + 1) * LANES].T[:4 * GQA_REP])
        for p in range(GQA_REP // 2):
            pair = []
            for r in (2 * p, 2 * p + 1):
                h = g * GQA_REP + r
                pair.append(sg[3 * r:3 * r + 1] * o_cmp[h] + sg[3 * r + 1:3 * r + 2] * o_slc[h]
                            + sg[3 * r + 2:3 * r + 3] * o_win[h])
            c0 = (g * GQA_REP // 2 + p) * LANES
            o_ref[:, c0:c0 + LANES] = jnp.concatenate(pair, axis=0).T.astype(o_ref.dtype)


def _nsa(q, gl, kce, vct, kse, vst, kwe, vwt, g0, ovt, et):
    nq = SEQ // TQ
    tile = lambda b, i: (b * nq + i, 0)
    per_b = lambda b, i: (b, 0, 0, 0)
    const2 = lambda b, i: (0, 0)
    return pl.pallas_call(
        _nsa_body,
        grid=(BATCH, nq),
        in_specs=[pl.BlockSpec((TQ, NSA_WIDTH), tile), pl.BlockSpec((TQ, NSA_GROUPS * LANES), tile)]
                 + [pl.BlockSpec((1,) + a.shape[1:], per_b) for a in (kce, vct, kse, vst, kwe, vwt)]
                 + [pl.BlockSpec((1, LANES), const2), pl.BlockSpec(ovt.shape, const2),
                    pl.BlockSpec(et.shape, const2)],
        out_specs=pl.BlockSpec((TQ, NSA_WIDTH), tile),
        out_shape=jax.ShapeDtypeStruct((TOKENS, NSA_WIDTH), BF16),
        scratch_shapes=[pltpu.VMEM((NSA_HEADS, SEQ, TQ), F32)],
        compiler_params=_cparams(("parallel", "arbitrary")),
        name="nsa_attention",
    )(q, gl, kce, vct, kse, vst, kwe, vwt, g0, ovt, et)


def _conv_body(glu_ref, w_ref, b_ref, lg_ref, lb_ref, o_ref, u_scr, sh_scr):
    u_scr[0:CONV_PAD, :] = jnp.zeros((CONV_PAD, CONV_WIDTH), F32)
    rows = 256

    def fill(i, carry):
        r0 = pl.multiple_of(i * rows, rows)
        blk = glu_ref[pl.ds(r0, rows), :]
        u_scr[pl.ds(CONV_PAD + r0, rows), :] = blk[:, :CONV_WIDTH] * jax.nn.sigmoid(blk[:, CONV_WIDTH:])
        return carry

    lax.fori_loop(0, SEQ // rows, fill, 0)
    first = CONV_PAD - (CONV_TAPS - 1)

    def tile(i, carry):
        r0 = pl.multiple_of(i * TS_CONV, TS_CONV)
        win = u_scr[pl.ds(r0, TS_CONV + CONV_PAD), :]
        span = TS_CONV + CONV_PAD - 8
        for s in range(1, 8):
            sh_scr[s - 1] = win[s:s + span, :]
        acc = jnp.zeros((TS_CONV, CONV_WIDTH), F32) + b_ref[...]
        for k in range(CONV_TAPS):
            s = (first + k) % 8
            base = first + k - s
            tap = win[base:base + TS_CONV, :] if s == 0 else sh_scr[s - 1, base:base + TS_CONV, :]
            acc = acc + tap * w_ref[k:k + 1, :]
        mu = jnp.mean(acc, axis=-1, keepdims=True)
        d = acc - mu
        var = jnp.mean(d * d, axis=-1, keepdims=True)
        yn = d * lax.rsqrt(var + NORM_EPS) * lg_ref[...] + lb_ref[...]
        o_ref[pl.ds(r0, TS_CONV), :] = (yn * jax.nn.sigmoid(yn)).astype(o_ref.dtype)
        return carry

    lax.fori_loop(0, SEQ // TS_CONV, tile, 0)


def _conv(glu, w, b, lg, lb):
    const2 = lambda i: (0, 0)
    return pl.pallas_call(
        _conv_body,
        grid=(BATCH,),
        in_specs=[pl.BlockSpec((SEQ, 2 * CONV_WIDTH), lambda i: (i, 0)),
                  pl.BlockSpec(w.shape, const2), pl.BlockSpec(b.shape, const2),
                  pl.BlockSpec(lg.shape, const2), pl.BlockSpec(lb.shape, const2)],
        out_specs=pl.BlockSpec((SEQ, CONV_WIDTH), lambda i: (i, 0)),
        out_shape=jax.ShapeDtypeStruct((TOKENS, CONV_WIDTH), BF16),
        scratch_shapes=[pltpu.VMEM((CONV_PAD + SEQ, CONV_WIDTH), F32),
                        pltpu.VMEM((7, TS_CONV + CONV_PAD - 8, CONV_WIDTH), F32)],
        compiler_params=_cparams(("parallel",)),
        name="conv_mixer",
    )(glu, w, b, lg, lb)


def _outproj_body(a_ref, c_ref, x_ref, mod_ref, g_ref, wa_ref, wc_ref, x1_ref, h2_ref):
    mix = (jnp.dot(a_ref[...], wa_ref[...], preferred_element_type=F32)
           + jnp.dot(c_ref[...], wc_ref[...], preferred_element_type=F32))
    x1 = x_ref[...] + mod_ref[0, 2:3, :] * mix
    x1_ref[...] = x1
    h2_ref[...] = _norm_mod(x1, g_ref[...], mod_ref[0, 3:4, :], mod_ref[0, 4:5, :]).astype(BF16)


def _outproj(a, c, xf, mod3, g, wa, wc):
    tiles_per_batch = SEQ // TM_PROJ
    row = lambda i: (i, 0)
    const2 = lambda i: (0, 0)
    return pl.pallas_call(
        _outproj_body,
        grid=(TOKENS // TM_PROJ,),
        in_specs=[pl.BlockSpec((TM_PROJ, NSA_WIDTH), row),
                  pl.BlockSpec((TM_PROJ, CONV_WIDTH), row),
                  pl.BlockSpec((TM_PROJ, D_MODEL), row),
                  pl.BlockSpec((1, 6, D_MODEL), lambda i: (i // tiles_per_batch, 0, 0)),
                  pl.BlockSpec((1, D_MODEL), const2),
                  pl.BlockSpec(wa.shape, const2), pl.BlockSpec(wc.shape, const2)],
        out_specs=[pl.BlockSpec((TM_PROJ, D_MODEL), row), pl.BlockSpec((TM_PROJ, D_MODEL), row)],
        out_shape=[jax.ShapeDtypeStruct((TOKENS, D_MODEL), F32),
                   jax.ShapeDtypeStruct((TOKENS, D_MODEL), BF16)],
        compiler_params=_cparams(("parallel",)),
        name="out_proj",
    )(a, c, xf, mod3, g, wa, wc)


_SORT4 = ((0, 1), (2, 3), (0, 2), (1, 3), (1, 2))


def _topk_rows(x, k):
    n, cols = x.shape
    q = n // 4
    row = lax.broadcasted_iota(jnp.int32, (q, cols), 0).astype(F32)
    vals = [x[i * q:(i + 1) * q] for i in range(4)]
    idxs = [row + float(i * q) for i in range(4)]
    for i, j in _SORT4:
        swap = (vals[j] > vals[i]) | ((vals[j] == vals[i]) & (idxs[j] < idxs[i]))
        vals[i], vals[j] = jnp.where(swap, vals[j], vals[i]), jnp.where(swap, vals[i], vals[j])
        idxs[i], idxs[j] = jnp.where(swap, idxs[j], idxs[i]), jnp.where(swap, idxs[i], idxs[j])
    slot = lax.broadcasted_iota(jnp.int32, (k, cols), 0)
    out_v = jnp.zeros((k, cols), F32)
    out_i = jnp.zeros((k, cols), F32)
    for it in range(k):
        m = jnp.max(vals[0], axis=0, keepdims=True)
        idx = jnp.min(jnp.where(vals[0] == m, idxs[0], IDX_BIG), axis=0, keepdims=True)
        hit = idxs[0] == idx
        for lvl in range(3):
            vals[lvl] = jnp.where(hit, vals[lvl + 1], vals[lvl])
            idxs[lvl] = jnp.where(hit, idxs[lvl + 1], idxs[lvl])
        vals[3] = jnp.where(hit, NEG_BIG, vals[3])
        out_v = jnp.where(slot == it, m, out_v)
        out_i = jnp.where(slot == it, idx, out_i)
    return out_v, out_i


def _pair_topk(v1, v2):
    k, cols = v1.shape
    half = k // 2
    alo = lax.broadcasted_iota(jnp.int32, (half, cols), 0).astype(F32)
    ahi = alo + float(half)
    levels = []
    for b in range(k):
        lvl = v1[:half] + jnp.broadcast_to(v2[b:b + 1, :], (half, cols))
        if k // (b + 1) < half:
            lvl = jnp.where(alo < float(k // (b + 1)), lvl, NEG_BIG)
        levels.append(lvl)
    top_hi = v1[half:] + jnp.broadcast_to(v2[0:1, :], (half, cols))
    depth = jnp.zeros((half, cols), F32)
    slot = lax.broadcasted_iota(jnp.int32, (k, cols), 0)
    tops = jnp.zeros((k, cols), F32)
    a_out = jnp.zeros((k, cols), F32)
    b_out = jnp.zeros((k, cols), F32)
    for it in range(k):
        m = jnp.max(jnp.maximum(levels[0], top_hi), axis=0, keepdims=True)
        a_sel = jnp.min(jnp.minimum(jnp.where(levels[0] == m, alo, IDX_BIG),
                                    jnp.where(top_hi == m, ahi, IDX_BIG)), axis=0, keepdims=True)
        hit = alo == a_sel
        b_sel = jnp.sum(jnp.where(hit, depth, 0.0), axis=0, keepdims=True)
        depth = jnp.where(hit, depth + 1.0, depth)
        for b in range(k - 1):
            levels[b] = jnp.where(hit, levels[b + 1], levels[b])
        levels[k - 1] = jnp.where(hit, NEG_BIG, levels[k - 1])
        top_hi = jnp.where(ahi == a_sel, NEG_BIG, top_hi)
        tops = jnp.where(slot == it, m, tops)
        a_out = jnp.where(slot == it, a_sel, a_out)
        b_out = jnp.where(slot == it, b_sel, b_out)
    return tops, a_out, b_out


def _route_body(h_ref, wq_ref, keys_ref, i1_ref, i2_ref, gw_ref, qp_scr, s1_scr, s2_scr, sw_scr):
    qp_scr[...] = jnp.dot(h_ref[...], wq_ref[...], preferred_element_type=F32)
    k = PEER_TOPK

    def head(h, carry):
        c0 = pl.multiple_of(h * PEER_QDIM, PEER_QDIM)
        tv, ti = [], []
        for c in range(2):
            qs = qp_scr[:, pl.ds(c0 + c * LANES, LANES)].astype(BF16)
            st = lax.dot_general(keys_ref[2 * h + c], qs, _NT, preferred_element_type=F32)
            v, i = _topk_rows(st, k)
            tv.append(v)
            ti.append(i)
        tops, a_sel, b_sel = _pair_topk(tv[0], tv[1])
        i1s = jnp.zeros_like(tops)
        i2s = jnp.zeros_like(tops)
        for a in range(k):
            i1s = i1s + jnp.where(a_sel == float(a), jnp.broadcast_to(ti[0][a:a + 1, :], tops.shape), 0.0)
            i2s = i2s + jnp.where(b_sel == float(a), jnp.broadcast_to(ti[1][a:a + 1, :], tops.shape), 0.0)
        e = jnp.exp(tops - jnp.max(tops, axis=0, keepdims=True))
        w = e / jnp.sum(e, axis=0, keepdims=True)
        r0 = pl.multiple_of(h * k, k)
        s1_scr[pl.ds(r0, k), :] = i1s
        s2_scr[pl.ds(r0, k), :] = i2s
        sw_scr[pl.ds(r0, k), :] = w
        return carry

    lax.fori_loop(0, PEER_HEADS, head, 0, unroll=2)
    i1_ref[...] = s1_scr[...].T
    i2_ref[...] = s2_scr[...].T
    gw_ref[...] = sw_scr[...].T


def _route(h2, wq, keys):
    nsel = PEER_HEADS * PEER_TOPK
    row = lambda i: (i, 0)
    return pl.pallas_call(
        _route_body,
        grid=(TOKENS // TL_ROUTE,),
        in_specs=[pl.BlockSpec((TL_ROUTE, D_MODEL), row),
                  pl.BlockSpec(wq.shape, lambda i: (0, 0)),
                  pl.BlockSpec(keys.shape, lambda i: (0, 0, 0))],
        out_specs=[pl.BlockSpec((TL_ROUTE, nsel), row)] * 3,
        out_shape=[jax.ShapeDtypeStruct((TOKENS, nsel), F32)] * 3,
        scratch_shapes=[pltpu.VMEM((TL_ROUTE, PEER_HEADS * PEER_QDIM), F32)]
                       + [pltpu.VMEM((nsel, TL_ROUTE), F32)] * 3,
        compiler_params=_cparams(("parallel",)),
        name="peer_route",
    )(h2, wq, keys)


def _peer_body(h_ref, u_ref, v_ref, i1_ref, i2_ref, gw_ref, x1_ref, mod_ref, o_ref, g_scr):
    e = pl.program_id(1)
    n = PEER_NKEYS

    @pl.when(e == 0)
    def _():
        o_ref[...] = jnp.zeros(o_ref.shape, F32)
        sub = lax.broadcasted_iota(jnp.int32, (n, n), 0).astype(F32)

        def tok(t, carry):
            w = gw_ref[pl.ds(t, 1), :]
            w_hi = w.astype(BF16).astype(F32)
            m1 = sub == i1_ref[pl.ds(t, 1), :]
            x1 = jnp.concatenate([jnp.where(m1, w_hi, 0.0), jnp.where(m1, w - w_hi, 0.0)], axis=1).astype(BF16)
            x2h = jnp.where(sub == i2_ref[pl.ds(t, 1), :], 1.0, 0.0)
            x2 = jnp.concatenate([x2h, x2h], axis=1).astype(BF16)
            g = lax.dot_general(x1, x2, _NT, preferred_element_type=F32)
            g_scr[pl.ds(pl.multiple_of(t * G_ROW_STRIDE, 8), n), :] = g
            return carry

        lax.fori_loop(0, TT_PEER, tok, 0, unroll=64)

    h = h_ref[...]
    parts = []
    for s in range(ET_PEER // ES_PEER):
        z = lax.dot_general(h, u_ref[s * ES_PEER:(s + 1) * ES_PEER, :], _NT,
                            preferred_element_type=F32)
        act = 0.5 * z * (1.0 + lax.erf(z * (2.0 ** -0.5)))
        for c in range(ES_PEER // n):
            i1 = e * (ET_PEER // n) + s * (ES_PEER // n) + c
            gc = g_scr[pl.ds(i1, TT_PEER, stride=G_ROW_STRIDE), :]
            parts.append((act[:, c * n:(c + 1) * n] * gc).astype(BF16))
    o_ref[...] += jnp.dot(jnp.concatenate(parts, axis=1), v_ref[...], preferred_element_type=F32)

    @pl.when(e == pl.num_programs(1) - 1)
    def _():
        o_ref[...] = x1_ref[...] + mod_ref[0, 5:6, :] * o_ref[...]


def _peer(h2, u, v, i1, i2, gw, x1, mod3):
    tiles_per_batch = SEQ // TT_PEER
    nsel = PEER_HEADS * PEER_TOPK
    tok = lambda i, e: (i, 0)
    exp = lambda i, e: (e, 0)
    once = pl.Buffered(1)
    return pl.pallas_call(
        _peer_body,
        grid=(TOKENS // TT_PEER, PEER_EXPERTS // ET_PEER),
        in_specs=[pl.BlockSpec((TT_PEER, D_MODEL), tok, pipeline_mode=once),
                  pl.BlockSpec((ET_PEER, D_MODEL), exp),
                  pl.BlockSpec((ET_PEER, D_MODEL), exp),
                  pl.BlockSpec((TT_PEER, nsel), tok, pipeline_mode=once),
                  pl.BlockSpec((TT_PEER, nsel), tok, pipeline_mode=once),
                  pl.BlockSpec((TT_PEER, nsel), tok, pipeline_mode=once),
                  pl.BlockSpec((TT_PEER, D_MODEL), tok, pipeline_mode=once),
                  pl.BlockSpec((1, 6, D_MODEL), lambda i, e: (i // tiles_per_batch, 0, 0))],
        out_specs=pl.BlockSpec((TT_PEER, D_MODEL), tok),
        out_shape=jax.ShapeDtypeStruct((TOKENS, D_MODEL), F32),
        scratch_shapes=[pltpu.VMEM((TT_PEER * G_ROW_STRIDE, PEER_NKEYS), F32)],
        compiler_params=pltpu.CompilerParams(dimension_semantics=("parallel", "arbitrary"),
                                             vmem_limit_bytes=VMEM_LIMIT_PEER),
        name="peer_experts",
    )(h2, u, v, i1, i2, gw, x1, mod3)


def _overlap_matrix():
    start = np.arange(N_CMP)[None, :] * CMP_STRIDE
    sel = np.arange(N_SEL)[:, None] * SEL_LEN
    ov = np.clip(np.minimum(start + CMP_LEN, sel + SEL_LEN) - np.maximum(start, sel), 0, None) / CMP_LEN
    out = np.zeros((LANES, LANES), np.float32)
    out[:N_SEL, :N_CMP] = ov
    return out


def _block_expand_matrix():
    out = np.zeros((SEQ, LANES), np.float32)
    out[np.arange(SEQ), np.arange(SEQ) // SEL_LEN] = 1.0
    return out


def _cmp_weights(wk, wv, first):
    width = 2 * NSA_GROUPS * HEAD_DIM
    blocks = [w[first:first + CMP_STRIDE].astype(BF16) for w in (wk, wv) for _ in range(NSA_GROUPS)]
    zero = jnp.zeros_like(blocks[0])
    rows = [jnp.concatenate([blk if i == j else zero for j in range(len(blocks))], axis=2)
            for i, blk in enumerate(blocks)]
    return jnp.concatenate(rows, axis=1).reshape(CMP_STRIDE * width, width)


def _dup(v):
    return jnp.concatenate([v, v])[None, :]


def kernel(x, c, w_ada, b_ada, norm_g, w_in, w_out, cmp_pe_k, cmp_pe_v, w_cmp_k, w_cmp_v, qk_norm_g,
           dw_w, dw_b, conv_ln_g, conv_ln_b, peer_wq, peer_sub_keys, peer_u, peer_v):
    assert x.shape == (BATCH, SEQ, D_MODEL) and w_ada.shape[0] == DEPTH
    ovt = jnp.asarray(_overlap_matrix(), BF16)
    emat = jnp.asarray(_block_expand_matrix(), BF16)
    o_kv = NSA_WIDTH
    o_gate = o_kv + 6 * NSA_GROUPS * HEAD_DIM
    o_glu = o_gate + 3 * NSA_HEADS
    n_cmp_cols = 2 * NSA_GROUPS * HEAD_DIM
    xf = x.reshape(TOKENS, D_MODEL)
    for l in range(DEPTH):
        mod3 = _ada(c, w_ada[l], b_ada[l][None, :]).reshape(BATCH, 6, D_MODEL)
        wi = w_in[l]
        gate_pad = jnp.zeros((D_MODEL, LANES - 3 * GQA_REP), F32)
        wg = jnp.concatenate(
            [part for g in range(NSA_GROUPS)
             for part in (wi[:, o_gate + 3 * GQA_REP * g:o_gate + 3 * GQA_REP * (g + 1)], gate_pad)], axis=1)
        q, kvc, kvs, gl, glu = _inproj(
            xf, mod3, norm_g[l, 0][None, :],
            wi[:, :o_kv].astype(BF16), wi[:, o_kv:o_kv + n_cmp_cols].astype(BF16),
            wi[:, o_kv + n_cmp_cols:o_gate].astype(BF16), wg.astype(BF16), wi[:, o_glu:].astype(BF16))
        pe = jnp.concatenate([cmp_pe_k[l], cmp_pe_k[l], cmp_pe_v[l], cmp_pe_v[l]], axis=1)
        kv_parts = _kvprep(
            kvc.reshape(BATCH, SEQ // CMP_STRIDE, CMP_STRIDE * n_cmp_cols), kvs,
            _cmp_weights(w_cmp_k[l], w_cmp_v[l], 0), _cmp_weights(w_cmp_k[l], w_cmp_v[l], CMP_STRIDE),
            pe[:CMP_STRIDE].reshape(1, -1), pe[CMP_STRIDE:].reshape(1, -1),
            _dup(qk_norm_g[l, 1]), _dup(qk_norm_g[l, 2]), _dup(qk_norm_g[l, 3]))
        a_out = _nsa(q, gl, *kv_parts, _dup(qk_norm_g[l, 0]), ovt, emat)
        c_out = _conv(glu, dw_w[l], dw_b[l][None, :], conv_ln_g[l][None, :], conv_ln_b[l][None, :])
        x1, h2 = _outproj(a_out, c_out, xf, mod3, norm_g[l, 1][None, :],
                          w_out[l, :NSA_WIDTH].astype(BF16), w_out[l, NSA_WIDTH:].astype(BF16))
        i1, i2, gw = _route(h2, peer_wq[l].astype(BF16),
                            peer_sub_keys[l].reshape(2 * PEER_HEADS, PEER_NKEYS, PEER_QDIM // 2).astype(BF16))
        xf = _peer(h2, peer_u[l].astype(BF16), peer_v[l].astype(BF16), i1, i2, gw, x1, mod3)
    return xf.reshape(BATCH, SEQ, D_MODEL)
```

```python
import functools

import numpy as np
import jax
import jax.numpy as jnp
from jax import lax
from jax.experimental import pallas as pl
from jax.experimental.pallas import tpu as pltpu

F32 = jnp.float32
BF16 = jnp.bfloat16

D_MODEL = 1024
BATCH = 8
SEQ = 2048
DEPTH = 1
TOKENS = BATCH * SEQ

HEAD_DIM = 64
NSA_HEADS = 8
NSA_GROUPS = 2
GQA_REP = NSA_HEADS // NSA_GROUPS
NSA_WIDTH = NSA_HEADS * HEAD_DIM
CONV_WIDTH = 512
CMP_LEN = 32
CMP_STRIDE = 16
N_CMP = (SEQ - CMP_LEN) // CMP_STRIDE + 1
SEL_LEN = 64
N_SEL = SEQ // SEL_LEN
SEL_TOPN = 16
WINDOW = 512
FORCE_BONUS = 1.0e4
CONV_TAPS = 31
PEER_HEADS = 8
PEER_NKEYS = 128
PEER_EXPERTS = PEER_NKEYS * PEER_NKEYS
PEER_QDIM = 256
PEER_TOPK = 16
NORM_EPS = 1e-6
NEG_INF = -1e30
NEG_BIG = -3.0e38
IDX_BIG = 1.0e9

LANES = 128
VMEM_V7X = 64 * 1024 * 1024
VMEM_LIMIT = VMEM_V7X * 3 // 4
VMEM_LIMIT_PEER = VMEM_V7X * 7 // 8

TM_PROJ = 512
TQ = 256
TK = 256
WIN_KEYS = WINDOW + TQ
V_ROWS = HEAD_DIM + 16
TS_CONV = 64
CONV_PAD = 32
TL_ROUTE = 256
TT_PEER = 512
ET_PEER = 1024
ES_PEER = 256
G_ROW_STRIDE = PEER_NKEYS + 8

_NT = (((1,), (1,)), ((), ()))


def _cparams(sem):
    return pltpu.CompilerParams(dimension_semantics=sem, vmem_limit_bytes=VMEM_LIMIT)


def _ada_body(c_ref, w_ref, b_ref, o_ref):
    c = c_ref[...]
    sc = (c * jax.nn.sigmoid(c)).astype(BF16)
    o_ref[...] = jnp.dot(sc, w_ref[...].astype(BF16), preferred_element_type=F32) + b_ref[...]


def _ada(c, w, b):
    n = w.shape[1]
    tn = 1536
    return pl.pallas_call(
        _ada_body,
        grid=(n // tn,),
        in_specs=[pl.BlockSpec((BATCH, D_MODEL), lambda j: (0, 0)),
                  pl.BlockSpec((D_MODEL, tn), lambda j: (0, j)),
                  pl.BlockSpec((1, tn), lambda j: (0, j))],
        out_specs=pl.BlockSpec((BATCH, tn), lambda j: (0, j)),
        out_shape=jax.ShapeDtypeStruct((BATCH, n), F32),
        compiler_params=_cparams(("arbitrary",)),
        name="ada_mod",
    )(c, w, b)


def _norm_mod(x, g, shift, scale):
    ms = jnp.mean(x * x, axis=-1, keepdims=True)
    y = x * lax.rsqrt(ms + NORM_EPS) * g
    return y * (1.0 + scale) + shift


def _inproj_body(x_ref, mod_ref, g_ref, wq_ref, wkc_ref, wks_ref, wg_ref, wglu_ref,
                 q_ref, kc_ref, ks_ref, gl_ref, glu_ref):
    h = _norm_mod(x_ref[...], g_ref[...], mod_ref[0, 0:1, :], mod_ref[0, 1:2, :]).astype(BF16)
    for w_ref, o_ref in ((wq_ref, q_ref), (wkc_ref, kc_ref), (wks_ref, ks_ref),
                         (wg_ref, gl_ref), (wglu_ref, glu_ref)):
        o_ref[...] = jnp.dot(h, w_ref[...], preferred_element_type=F32)


def _inproj(xf, mod3, g, wq, wkc, wks, wg, wglu):
    tiles_per_batch = SEQ // TM_PROJ
    ws = (wq, wkc, wks, wg, wglu)
    row = lambda i: (i, 0)
    return pl.pallas_call(
        _inproj_body,
        grid=(TOKENS // TM_PROJ,),
        in_specs=[pl.BlockSpec((TM_PROJ, D_MODEL), row),
                  pl.BlockSpec((1, 6, D_MODEL), lambda i: (i // tiles_per_batch, 0, 0)),
                  pl.BlockSpec((1, D_MODEL), lambda i: (0, 0))]
                 + [pl.BlockSpec(w.shape, lambda i: (0, 0)) for w in ws],
        out_specs=[pl.BlockSpec((TM_PROJ, w.shape[1]), row) for w in ws],
        out_shape=[jax.ShapeDtypeStruct((TOKENS, w.shape[1]), F32) for w in ws],
        compiler_params=_cparams(("parallel",)),
        name="in_proj",
    )(xf, mod3, g, *ws)


def _rms_pair(x, gdup, lo):
    x2 = x * x
    s_lo = jnp.sum(jnp.where(lo, x2, 0.0), axis=-1, keepdims=True)
    s_hi = jnp.sum(jnp.where(lo, 0.0, x2), axis=-1, keepdims=True)
    rs = jnp.where(lo, lax.rsqrt(s_lo * (1.0 / HEAD_DIM) + NORM_EPS),
                   lax.rsqrt(s_hi * (1.0 / HEAD_DIM) + NORM_EPS))
    return x * rs * gdup


def _key_ext(kn, lo, lane, pos):
    ext = jnp.where(lane == HEAD_DIM, (pos >> 6).astype(F32),
                    jnp.where(lane == HEAD_DIM + 1, (pos & (SEL_LEN - 1)).astype(F32),
                              jnp.where(lane == HEAD_DIM + 2, 1.0, 0.0)))
    return jnp.where(lo, kn, ext), jnp.where(lo, pltpu.roll(kn, HEAD_DIM, 1), ext)


def _kvprep_body(r_ref, kvs_ref, wa_ref, wb_ref, pea_ref, peb_ref, g1_ref, g2_ref, g3_ref,
                 kce_ref, vct_ref, kse_ref, vst_ref, kwe_ref, vwt_ref):
    lane = lax.broadcasted_iota(jnp.int32, (1, LANES), 1)
    lo = lane < HEAD_DIM
    nrow = SEQ // CMP_STRIDE
    r = r_ref[0]
    a = jnp.dot((r + pea_ref[...]).astype(BF16), wa_ref[...], preferred_element_type=F32)
    b = jnp.dot((r + peb_ref[...]).astype(BF16), wb_ref[...], preferred_element_type=F32)
    c = a + pltpu.roll(b, nrow - 1, 0)
    end = lax.broadcasted_iota(jnp.int32, (nrow, 1), 0) * CMP_STRIDE + (CMP_LEN - 1)
    kc0, kc1 = _key_ext(_rms_pair(c[:, :LANES], g1_ref[...], lo), lo, lane, end)
    kce_ref[0, 0] = kc0.astype(BF16)
    kce_ref[0, 1] = kc1.astype(BF16)
    vct = c[:, LANES:].T.astype(BF16)
    vct_ref[0, 0] = vct[:HEAD_DIM]
    vct_ref[0, 1] = vct[HEAD_DIM:]

    rows = 256
    ones = jnp.ones((V_ROWS - HEAD_DIM, rows), F32)

    def chunk(i, carry):
        r0 = pl.multiple_of(i * rows, rows)
        blk = kvs_ref[pl.ds(r0, rows), :]
        pos = r0 + lax.broadcasted_iota(jnp.int32, (rows, 1), 0)
        for k_ref, v_ref, gain, off in ((kse_ref, vst_ref, g2_ref, 0), (kwe_ref, vwt_ref, g3_ref, 2 * LANES)):
            k0, k1 = _key_ext(_rms_pair(blk[:, off:off + LANES], gain[...], lo), lo, lane, pos)
            k_ref[0, 0, pl.ds(r0, rows), :] = k0.astype(BF16)
            k_ref[0, 1, pl.ds(r0, rows), :] = k1.astype(BF16)
            vt = blk[:, off + LANES:off + 2 * LANES].T
            for g in range(NSA_GROUPS):
                v_ref[0, g, :, pl.ds(r0, rows)] = jnp.concatenate(
                    [vt[g * HEAD_DIM:(g + 1) * HEAD_DIM], ones], axis=0).astype(BF16)
        return carry

    lax.fori_loop(0, SEQ // rows, chunk, 0)


def _kvprep(rmat, kvs, wa, wb, pea, peb, g1, g2, g3):
    nrow = SEQ // CMP_STRIDE
    const2 = lambda b: (0, 0)
    per_b = lambda b: (b, 0, 0, 0)
    shapes = [(nrow, LANES), (HEAD_DIM, nrow), (SEQ, LANES), (V_ROWS, SEQ), (SEQ, LANES), (V_ROWS, SEQ)]
    return pl.pallas_call(
        _kvprep_body,
        grid=(BATCH,),
        in_specs=[pl.BlockSpec((1, nrow, rmat.shape[2]), lambda b: (b, 0, 0)),
                  pl.BlockSpec((SEQ, kvs.shape[1]), lambda b: (b, 0)),
                  pl.BlockSpec(wa.shape, const2), pl.BlockSpec(wb.shape, const2),
                  pl.BlockSpec(pea.shape, const2), pl.BlockSpec(peb.shape, const2),
                  pl.BlockSpec((1, LANES), const2), pl.BlockSpec((1, LANES), const2),
                  pl.BlockSpec((1, LANES), const2)],
        out_specs=[pl.BlockSpec((1, NSA_GROUPS) + s, per_b) for s in shapes],
        out_shape=[jax.ShapeDtypeStruct((BATCH, NSA_GROUPS) + s, BF16) for s in shapes],
        compiler_params=_cparams(("parallel",)),
        name="kv_prep",
    )(rmat, kvs, wa, wb, pea, peb, g1, g2, g3)


def _split3(x):
    p1 = x.astype(BF16)
    r1 = x - p1.astype(F32)
    p2 = r1.astype(BF16)
    p3 = (r1 - p2.astype(F32)).astype(BF16)
    return p1, p2, p3


def _nsa_body(q_ref, gl_ref, kce_ref, vct_ref, kse_ref, vst_ref, kwe_ref, vwt_ref,
              g0_ref, ovt_ref, et_ref, o_ref, s_scr):
    qi = pl.program_id(1)
    q0 = qi * TQ
    lane = lax.broadcasted_iota(jnp.int32, (1, LANES), 1)
    lo = lane < HEAD_DIM
    t_idx = q0 + lax.broadcasted_iota(jnp.int32, (1, TQ), 1)
    q0f = q0.astype(F32)

    n_sub = lax.broadcasted_iota(jnp.int32, (SEQ // CMP_STRIDE, 1), 0)
    cmask = (t_idx >= n_sub * CMP_STRIDE + (CMP_LEN - 1)) & (n_sub < N_CMP)
    j = lax.broadcasted_iota(jnp.int32, (N_SEL, 1), 0)
    tb = t_idx >> 6
    bonus = jnp.where((j == 0) | (j == tb) | (j == tb - 1), FORCE_BONUS, 0.0)

    qe, o_cmp, sels = [], [], []
    for g in range(NSA_GROUPS):
        for p in range(GQA_REP // 2):
            c0 = (g * GQA_REP // 2 + p) * LANES
            qn = _rms_pair(q_ref[:, c0:c0 + LANES], g0_ref[...], lo) * (HEAD_DIM ** -0.5)
            for half, base in ((0, qn), (1, pltpu.roll(qn, HEAD_DIM, 1))):
                slope = 2.0 ** -(g * GQA_REP + 2 * p + half + 1)
                ext = jnp.where(lane == HEAD_DIM, SEL_LEN * slope,
                                jnp.where(lane == HEAD_DIM + 1, slope,
                                          jnp.where(lane == HEAD_DIM + 2, -slope * q0f, 0.0)))
                qe.append(jnp.where(lo, base, ext).astype(BF16))
        heads = range(g * GQA_REP, (g + 1) * GQA_REP)

        kce = kce_ref[0, g]
        vct = vct_ref[0, g]
        psum = jnp.zeros((SEQ // CMP_STRIDE, TQ), F32)
        for h in heads:
            s = jnp.where(cmask, lax.dot_general(kce, qe[h], _NT, preferred_element_type=F32), NEG_INF)
            e = jnp.where(cmask, jnp.exp(s - jnp.max(s, axis=0, keepdims=True)), 0.0)
            l = jnp.sum(e, axis=0, keepdims=True)
            p = e / jnp.where(l > 0.0, l, 1.0)
            psum = psum + p
            o_cmp.append(jnp.dot(vct, p.astype(BF16), preferred_element_type=F32))

        imp = jnp.zeros((LANES, TQ), F32)
        for part in _split3(psum):
            imp = imp + jnp.dot(ovt_ref[...], part, preferred_element_type=F32)
        imp = jnp.where(j <= tb, imp[:N_SEL] + bonus, -1.0)
        rank = jnp.zeros((N_SEL, TQ), F32)
        for i in range(N_SEL):
            ci = imp[i:i + 1, :]
            ahead = (ci > imp) | ((ci == imp) & (j > i))
            rank = rank + jnp.where(ahead, 1.0, 0.0)
        sels.append(jnp.concatenate([jnp.where(rank < float(SEL_TOPN), 1.0, 0.0),
                                     jnp.zeros((LANES - N_SEL, TQ), F32)], axis=0).astype(BF16))

    key_sub = lax.broadcasted_iota(jnp.int32, (TK, 1), 0)
    neg_row = tuple(jnp.full((1, TQ), NEG_INF, F32) for _ in range(NSA_HEADS))
    zero_acc = tuple(jnp.zeros((V_ROWS, TQ), F32) for _ in range(NSA_HEADS))

    def two_pass(n_tiles, first_key, k_ref, v_ref, bias_fn):
        def score_step(i, ms):
            k0 = pl.multiple_of(first_key + i * TK, LANES)
            r0 = pl.multiple_of(i * TK, TK)
            out = []
            for g in range(NSA_GROUPS):
                kblk = k_ref[0, g, pl.ds(k0, TK), :]
                bias = bias_fn(g, k0)
                for h in range(g * GQA_REP, (g + 1) * GQA_REP):
                    s = lax.dot_general(kblk, qe[h], _NT, preferred_element_type=F32) + bias
                    s_scr[h, pl.ds(r0, TK), :] = s
                    out.append(jnp.maximum(ms[h], jnp.max(s, axis=0, keepdims=True)))
            return tuple(out)

        ms = lax.fori_loop(0, n_tiles, score_step, neg_row)

        def value_step(i, accs):
            k0 = pl.multiple_of(first_key + i * TK, LANES)
            r0 = pl.multiple_of(i * TK, TK)
            out = []
            for g in range(NSA_GROUPS):
                vt = v_ref[0, g, :, pl.ds(k0, TK)]
                for h in range(g * GQA_REP, (g + 1) * GQA_REP):
                    e = jnp.exp(s_scr[h, pl.ds(r0, TK), :] - ms[h])
                    out.append(accs[h] + jnp.dot(vt, e.astype(BF16), preferred_element_type=F32))
            return tuple(out)

        accs = lax.fori_loop(0, n_tiles, value_step, zero_acc)
        return [a[:HEAD_DIM] / a[HEAD_DIM:HEAD_DIM + 1] for a in accs]

    start = pl.multiple_of(jnp.maximum(q0 - WINDOW, 0), LANES)

    def window_bias(g, k0):
        wd = t_idx - (k0 + key_sub)
        return jnp.where((wd >= 0) & (wd < WINDOW), 0.0, NEG_INF)

    o_win = two_pass(WIN_KEYS // TK, start, kwe_ref, vwt_ref, window_bias)

    def selected_bias(g, k0):
        chosen = jnp.dot(et_ref[pl.ds(k0, TK), :], sels[g], preferred_element_type=F32) > 0.5
        return jnp.where(chosen & (k0 + key_sub <= t_idx), 0.0, NEG_INF)

    o_slc = two_pass((q0 + TQ - 1) // TK + 1, 0, kse_ref, vst_ref, selected_bias)

    for g in range(NSA_GROUPS):
        sg = jax.nn.sigmoid(gl_ref[:, g * LANES:(g + 1) * LANES].T[:4 * GQA_REP])
        for p in range(GQA_REP // 2):
            pair = []
            for r in (2 * p, 2 * p + 1):
                h = g * GQA_REP + r
                pair.append(sg[3 * r:3 * r + 1] * o_cmp[h] + sg[3 * r + 1:3 * r + 2] * o_slc[h]
                            + sg[3 * r + 2:3 * r + 3] * o_win[h])
            c0 = (g * GQA_REP // 2 + p) * LANES
            o_ref[:, c0:c0 + LANES] = jnp.concatenate(pair, axis=0).T.astype(o_ref.dtype)


def _nsa(q, gl, kce, vct, kse, vst, kwe, vwt, g0, ovt, et):
    nq = SEQ // TQ
    tile = lambda b, i: (b * nq + i, 0)
    per_b = lambda b, i: (b, 0, 0, 0)
    const2 = lambda b, i: (0, 0)
    return pl.pallas_call(
        _nsa_body,
        grid=(BATCH, nq),
        in_specs=[pl.BlockSpec((TQ, NSA_WIDTH), tile), pl.BlockSpec((TQ, NSA_GROUPS * LANES), tile)]
                 + [pl.BlockSpec((1,) + a.shape[1:], per_b) for a in (kce, vct, kse, vst, kwe, vwt)]
                 + [pl.BlockSpec((1, LANES), const2), pl.BlockSpec(ovt.shape, const2),
                    pl.BlockSpec(et.shape, const2)],
        out_specs=pl.BlockSpec((TQ, NSA_WIDTH), tile),
        out_shape=jax.ShapeDtypeStruct((TOKENS, NSA_WIDTH), BF16),
        scratch_shapes=[pltpu.VMEM((NSA_HEADS, SEQ, TQ), F32)],
        compiler_params=_cparams(("parallel", "arbitrary")),
        name="nsa_attention",
    )(q, gl, kce, vct, kse, vst, kwe, vwt, g0, ovt, et)


def _conv_body(glu_ref, w_ref, b_ref, lg_ref, lb_ref, o_ref, u_scr, sh_scr):
    u_scr[0:CONV_PAD, :] = jnp.zeros((CONV_PAD, CONV_WIDTH), F32)
    rows = 256

    def fill(i, carry):
        r0 = pl.multiple_of(i * rows, rows)
        blk = glu_ref[pl.ds(r0, rows), :]
        u_scr[pl.ds(CONV_PAD + r0, rows), :] = blk[:, :CONV_WIDTH] * jax.nn.sigmoid(blk[:, CONV_WIDTH:])
        return carry

    lax.fori_loop(0, SEQ // rows, fill, 0)
    first = CONV_PAD - (CONV_TAPS - 1)

    def tile(i, carry):
        r0 = pl.multiple_of(i * TS_CONV, TS_CONV)
        win = u_scr[pl.ds(r0, TS_CONV + CONV_PAD), :]
        span = TS_CONV + CONV_PAD - 8
        for s in range(1, 8):
            sh_scr[s - 1] = win[s:s + span, :]
        acc = jnp.zeros((TS_CONV, CONV_WIDTH), F32) + b_ref[...]
        for k in range(CONV_TAPS):
            s = (first + k) % 8
            base = first + k - s
            tap = win[base:base + TS_CONV, :] if s == 0 else sh_scr[s - 1, base:base + TS_CONV, :]
            acc = acc + tap * w_ref[k:k + 1, :]
        mu = jnp.mean(acc, axis=-1, keepdims=True)
        d = acc - mu
        var = jnp.mean(d * d, axis=-1, keepdims=True)
        yn = d * lax.rsqrt(var + NORM_EPS) * lg_ref[...] + lb_ref[...]
        o_ref[pl.ds(r0, TS_CONV), :] = (yn * jax.nn.sigmoid(yn)).astype(o_ref.dtype)
        return carry

    lax.fori_loop(0, SEQ // TS_CONV, tile, 0)


def _conv(glu, w, b, lg, lb):
    const2 = lambda i: (0, 0)
    return pl.pallas_call(
        _conv_body,
        grid=(BATCH,),
        in_specs=[pl.BlockSpec((SEQ, 2 * CONV_WIDTH), lambda i: (i, 0)),
                  pl.BlockSpec(w.shape, const2), pl.BlockSpec(b.shape, const2),
                  pl.BlockSpec(lg.shape, const2), pl.BlockSpec(lb.shape, const2)],
        out_specs=pl.BlockSpec((SEQ, CONV_WIDTH), lambda i: (i, 0)),
        out_shape=jax.ShapeDtypeStruct((TOKENS, CONV_WIDTH), BF16),
        scratch_shapes=[pltpu.VMEM((CONV_PAD + SEQ, CONV_WIDTH), F32),
                        pltpu.VMEM((7, TS_CONV + CONV_PAD - 8, CONV_WIDTH), F32)],
        compiler_params=_cparams(("parallel",)),
        name="conv_mixer",
    )(glu, w, b, lg, lb)


def _outproj_body(a_ref, c_ref, x_ref, mod_ref, g_ref, wa_ref, wc_ref, x1_ref, h2_ref):
    mix = (jnp.dot(a_ref[...], wa_ref[...], preferred_element_type=F32)
           + jnp.dot(c_ref[...], wc_ref[...], preferred_element_type=F32))
    x1 = x_ref[...] + mod_ref[0, 2:3, :] * mix
    x1_ref[...] = x1
    h2_ref[...] = _norm_mod(x1, g_ref[...], mod_ref[0, 3:4, :], mod_ref[0, 4:5, :]).astype(BF16)


def _outproj(a, c, xf, mod3, g, wa, wc):
    tiles_per_batch = SEQ // TM_PROJ
    row = lambda i: (i, 0)
    const2 = lambda i: (0, 0)
    return pl.pallas_call(
        _outproj_body,
        grid=(TOKENS // TM_PROJ,),
        in_specs=[pl.BlockSpec((TM_PROJ, NSA_WIDTH), row),
                  pl.BlockSpec((TM_PROJ, CONV_WIDTH), row),
                  pl.BlockSpec((TM_PROJ, D_MODEL), row),
                  pl.BlockSpec((1, 6, D_MODEL), lambda i: (i // tiles_per_batch, 0, 0)),
                  pl.BlockSpec((1, D_MODEL), const2),
                  pl.BlockSpec(wa.shape, const2), pl.BlockSpec(wc.shape, const2)],
        out_specs=[pl.BlockSpec((TM_PROJ, D_MODEL), row), pl.BlockSpec((TM_PROJ, D_MODEL), row)],
        out_shape=[jax.ShapeDtypeStruct((TOKENS, D_MODEL), F32),
                   jax.ShapeDtypeStruct((TOKENS, D_MODEL), BF16)],
        compiler_params=_cparams(("parallel",)),
        name="out_proj",
    )(a, c, xf, mod3, g, wa, wc)


_SORT4 = ((0, 1), (2, 3), (0, 2), (1, 3), (1, 2))


def _topk_rows(x, k):
    n, cols = x.shape
    q = n // 4
    row = lax.broadcasted_iota(jnp.int32, (q, cols), 0).astype(F32)
    vals = [x[i * q:(i + 1) * q] for i in range(4)]
    idxs = [row + float(i * q) for i in range(4)]
    for i, j in _SORT4:
        swap = (vals[j] > vals[i]) | ((vals[j] == vals[i]) & (idxs[j] < idxs[i]))
        vals[i], vals[j] = jnp.where(swap, vals[j], vals[i]), jnp.where(swap, vals[i], vals[j])
        idxs[i], idxs[j] = jnp.where(swap, idxs[j], idxs[i]), jnp.where(swap, idxs[i], idxs[j])
    slot = lax.broadcasted_iota(jnp.int32, (k, cols), 0)
    out_v = jnp.zeros((k, cols), F32)
    out_i = jnp.zeros((k, cols), F32)
    for it in range(k):
        m = jnp.max(vals[0], axis=0, keepdims=True)
        idx = jnp.min(jnp.where(vals[0] == m, idxs[0], IDX_BIG), axis=0, keepdims=True)
        hit = idxs[0] == idx
        for lvl in range(3):
            vals[lvl] = jnp.where(hit, vals[lvl + 1], vals[lvl])
            idxs[lvl] = jnp.where(hit, idxs[lvl + 1], idxs[lvl])
        vals[3] = jnp.where(hit, NEG_BIG, vals[3])
        out_v = jnp.where(slot == it, m, out_v)
        out_i = jnp.where(slot == it, idx, out_i)
    return out_v, out_i


def _pair_topk(v1, v2):
    k, cols = v1.shape
    half = k // 2
    alo = lax.broadcasted_iota(jnp.int32, (half, cols), 0).astype(F32)
    ahi = alo + float(half)
    levels = []
    for b in range(k):
        lvl = v1[:half] + jnp.broadcast_to(v2[b:b + 1, :], (half, cols))
        if k // (b + 1) < half:
            lvl = jnp.where(alo < float(k // (b + 1)), lvl, NEG_BIG)
        levels.append(lvl)
    top_hi = v1[half:] + jnp.broadcast_to(v2[0:1, :], (half, cols))
    depth = jnp.zeros((half, cols), F32)
    slot = lax.broadcasted_iota(jnp.int32, (k, cols), 0)
    tops = jnp.zeros((k, cols), F32)
    a_out = jnp.zeros((k, cols), F32)
    b_out = jnp.zeros((k, cols), F32)
    for it in range(k):
        m = jnp.max(jnp.maximum(levels[0], top_hi), axis=0, keepdims=True)
        a_sel = jnp.min(jnp.minimum(jnp.where(levels[0] == m, alo, IDX_BIG),
                                    jnp.where(top_hi == m, ahi, IDX_BIG)), axis=0, keepdims=True)
        hit = alo == a_sel
        b_sel = jnp.sum(jnp.where(hit, depth, 0.0), axis=0, keepdims=True)
        depth = jnp.where(hit, depth + 1.0, depth)
        for b in range(k - 1):
            levels[b] = jnp.where(hit, levels[b + 1], levels[b])
        levels[k - 1] = jnp.where(hit, NEG_BIG, levels[k - 1])
        top_hi = jnp.where(ahi == a_sel, NEG_BIG, top_hi)
        tops = jnp.where(slot == it, m, tops)
        a_out = jnp.where(slot == it, a_sel, a_out)
        b_out = jnp.where(slot == it, b_sel, b_out)
    return tops, a_out, b_out


def _route_body(h_ref, wq_ref, keys_ref, i1_ref, i2_ref, gw_ref, qp_scr, s1_scr, s2_scr, sw_scr):
    qp_scr[...] = jnp.dot(h_ref[...], wq_ref[...], preferred_element_type=F32)
    k = PEER_TOPK

    def head(h, carry):
        c0 = pl.multiple_of(h * PEER_QDIM, PEER_QDIM)
        tv, ti = [], []
        for c in range(2):
            qs = qp_scr[:, pl.ds(c0 + c * LANES, LANES)].astype(BF16)
            st = lax.dot_general(keys_ref[2 * h + c], qs, _NT, preferred_element_type=F32)
            v, i = _topk_rows(st, k)
            tv.append(v)
            ti.append(i)
        tops, a_sel, b_sel = _pair_topk(tv[0], tv[1])
        i1s = jnp.zeros_like(tops)
        i2s = jnp.zeros_like(tops)
        for a in range(k):
            i1s = i1s + jnp.where(a_sel == float(a), jnp.broadcast_to(ti[0][a:a + 1, :], tops.shape), 0.0)
            i2s = i2s + jnp.where(b_sel == float(a), jnp.broadcast_to(ti[1][a:a + 1, :], tops.shape), 0.0)
        e = jnp.exp(tops - jnp.max(tops, axis=0, keepdims=True))
        w = e / jnp.sum(e, axis=0, keepdims=True)
        r0 = pl.multiple_of(h * k, k)
        s1_scr[pl.ds(r0, k), :] = i1s
        s2_scr[pl.ds(r0, k), :] = i2s
        sw_scr[pl.ds(r0, k), :] = w
        return carry

    lax.fori_loop(0, PEER_HEADS, head, 0, unroll=4)
    i1_ref[...] = s1_scr[...].T
    i2_ref[...] = s2_scr[...].T
    gw_ref[...] = sw_scr[...].T


def _route(h2, wq, keys):
    nsel = PEER_HEADS * PEER_TOPK
    row = lambda i: (i, 0)
    return pl.pallas_call(
        _route_body,
        grid=(TOKENS // TL_ROUTE,),
        in_specs=[pl.BlockSpec((TL_ROUTE, D_MODEL), row),
                  pl.BlockSpec(wq.shape, lambda i: (0, 0)),
                  pl.BlockSpec(keys.shape, lambda i: (0, 0, 0))],
        out_specs=[pl.BlockSpec((TL_ROUTE, nsel), row)] * 3,
        out_shape=[jax.ShapeDtypeStruct((TOKENS, nsel), F32)] * 3,
        scratch_shapes=[pltpu.VMEM((TL_ROUTE, PEER_HEADS * PEER_QDIM), F32)]
                       + [pltpu.VMEM((nsel, TL_ROUTE), F32)] * 3,
        compiler_params=_cparams(("parallel",)),
        name="peer_route",
    )(h2, wq, keys)


def _peer_body(h_ref, u_ref, v_ref, i1_ref, i2_ref, gw_ref, x1_ref, mod_ref, o_ref, g_scr):
    e = pl.program_id(1)
    n = PEER_NKEYS

    @pl.when(e == 0)
    def _():
        o_ref[...] = jnp.zeros(o_ref.shape, F32)
        sub = lax.broadcasted_iota(jnp.int32, (n, n), 0).astype(F32)

        def tok(t, carry):
            w = gw_ref[pl.ds(t, 1), :]
            w_hi = w.astype(BF16).astype(F32)
            m1 = sub == i1_ref[pl.ds(t, 1), :]
            x1 = jnp.concatenate([jnp.where(m1, w_hi, 0.0), jnp.where(m1, w - w_hi, 0.0)], axis=1).astype(BF16)
            x2h = jnp.where(sub == i2_ref[pl.ds(t, 1), :], 1.0, 0.0)
            x2 = jnp.concatenate([x2h, x2h], axis=1).astype(BF16)
            g = lax.dot_general(x1, x2, _NT, preferred_element_type=F32)
            g_scr[pl.ds(pl.multiple_of(t * G_ROW_STRIDE, 8), n), :] = g
            return carry

        lax.fori_loop(0, TT_PEER, tok, 0, unroll=64)

    h = h_ref[...]
    parts = []
    for s in range(ET_PEER // ES_PEER):
        z = lax.dot_general(h, u_ref[s * ES_PEER:(s + 1) * ES_PEER, :], _NT,
                            preferred_element_type=F32)
        act = 0.5 * z * (1.0 + lax.erf(z * (2.0 ** -0.5)))
        for c in range(ES_PEER // n):
            i1 = e * (ET_PEER // n) + s * (ES_PEER // n) + c
            gc = g_scr[pl.ds(i1, TT_PEER, stride=G_ROW_STRIDE), :]
            parts.append((act[:, c * n:(c + 1) * n] * gc).astype(BF16))
    o_ref[...] += jnp.dot(jnp.concatenate(parts, axis=1), v_ref[...], preferred_element_type=F32)

    @pl.when(e == pl.num_programs(1) - 1)
    def _():
        o_ref[...] = x1_ref[...] + mod_ref[0, 5:6, :] * o_ref[...]


def _peer(h2, u, v, i1, i2, gw, x1, mod3):
    tiles_per_batch = SEQ // TT_PEER
    nsel = PEER_HEADS * PEER_TOPK
    tok = lambda i, e: (i, 0)
    exp = lambda i, e: (e, 0)
    once = pl.Buffered(1)
    return pl.pallas_call(
        _peer_body,
        grid=(TOKENS // TT_PEER, PEER_EXPERTS // ET_PEER),
        in_specs=[pl.BlockSpec((TT_PEER, D_MODEL), tok, pipeline_mode=once),
                  pl.BlockSpec((ET_PEER, D_MODEL), exp),
                  pl.BlockSpec((ET_PEER, D_MODEL), exp),
                  pl.BlockSpec((TT_PEER, nsel), tok, pipeline_mode=once),
                  pl.BlockSpec((TT_PEER, nsel), tok, pipeline_mode=once),
                  pl.BlockSpec((TT_PEER, nsel), tok, pipeline_mode=once),
                  pl.BlockSpec((TT_PEER, D_MODEL), tok, pipeline_mode=once),
                  pl.BlockSpec((1, 6, D_MODEL), lambda i, e: (i // tiles_per_batch, 0, 0))],
        out_specs=pl.BlockSpec((TT_PEER, D_MODEL), tok),
        out_shape=jax.ShapeDtypeStruct((TOKENS, D_MODEL), F32),
        scratch_shapes=[pltpu.VMEM((TT_PEER * G_ROW_STRIDE, PEER_NKEYS), F32)],
        compiler_params=pltpu.CompilerParams(dimension_semantics=("parallel", "arbitrary"),
                                             vmem_limit_bytes=VMEM_LIMIT_PEER),
        name="peer_experts",
    )(h2, u, v, i1, i2, gw, x1, mod3)


def _overlap_matrix():
    start = np.arange(N_CMP)[None, :] * CMP_STRIDE
    sel = np.arange(N_SEL)[:, None] * SEL_LEN
    ov = np.clip(np.minimum(start + CMP_LEN, sel + SEL_LEN) - np.maximum(start, sel), 0, None) / CMP_LEN
    out = np.zeros((LANES, LANES), np.float32)
    out[:N_SEL, :N_CMP] = ov
    return out


def _block_expand_matrix():
    out = np.zeros((SEQ, LANES), np.float32)
    out[np.arange(SEQ), np.arange(SEQ) // SEL_LEN] = 1.0
    return out


def _cmp_weights(wk, wv, first):
    width = 2 * NSA_GROUPS * HEAD_DIM
    blocks = [w[first:first + CMP_STRIDE].astype(BF16) for w in (wk, wv) for _ in range(NSA_GROUPS)]
    zero = jnp.zeros_like(blocks[0])
    rows = [jnp.concatenate([blk if i == j else zero for j in range(len(blocks))], axis=2)
            for i, blk in enumerate(blocks)]
    return jnp.concatenate(rows, axis=1).reshape(CMP_STRIDE * width, width)


def _dup(v):
    return jnp.concatenate([v, v])[None, :]


def kernel(x, c, w_ada, b_ada, norm_g, w_in, w_out, cmp_pe_k, cmp_pe_v, w_cmp_k, w_cmp_v, qk_norm_g,
           dw_w, dw_b, conv_ln_g, conv_ln_b, peer_wq, peer_sub_keys, peer_u, peer_v):
    assert x.shape == (BATCH, SEQ, D_MODEL) and w_ada.shape[0] == DEPTH
    ovt = jnp.asarray(_overlap_matrix(), BF16)
    emat = jnp.asarray(_block_expand_matrix(), BF16)
    o_kv = NSA_WIDTH
    o_gate = o_kv + 6 * NSA_GROUPS * HEAD_DIM
    o_glu = o_gate + 3 * NSA_HEADS
    n_cmp_cols = 2 * NSA_GROUPS * HEAD_DIM
    xf = x.reshape(TOKENS, D_MODEL)
    for l in range(DEPTH):
        mod3 = _ada(c, w_ada[l], b_ada[l][None, :]).reshape(BATCH, 6, D_MODEL)
        wi = w_in[l]
        gate_pad = jnp.zeros((D_MODEL, LANES - 3 * GQA_REP), F32)
        wg = jnp.concatenate(
            [part for g in range(NSA_GROUPS)
             for part in (wi[:, o_gate + 3 * GQA_REP * g:o_gate + 3 * GQA_REP * (g + 1)], gate_pad)], axis=1)
        q, kvc, kvs, gl, glu = _inproj(
            xf, mod3, norm_g[l, 0][None, :],
            wi[:, :o_kv].astype(BF16), wi[:, o_kv:o_kv + n_cmp_cols].astype(BF16),
            wi[:, o_kv + n_cmp_cols:o_gate].astype(BF16), wg.astype(BF16), wi[:, o_glu:].astype(BF16))
        pe = jnp.concatenate([cmp_pe_k[l], cmp_pe_k[l], cmp_pe_v[l], cmp_pe_v[l]], axis=1)
        kv_parts = _kvprep(
            kvc.reshape(BATCH, SEQ // CMP_STRIDE, CMP_STRIDE * n_cmp_cols), kvs,
            _cmp_weights(w_cmp_k[l], w_cmp_v[l], 0), _cmp_weights(w_cmp_k[l], w_cmp_v[l], CMP_STRIDE),
            pe[:CMP_STRIDE].reshape(1, -1), pe[CMP_STRIDE:].reshape(1, -1),
            _dup(qk_norm_g[l, 1]), _dup(qk_norm_g[l, 2]), _dup(qk_norm_g[l, 3]))
        a_out = _nsa(q, gl, *kv_parts, _dup(qk_norm_g[l, 0]), ovt, emat)
        c_out = _conv(glu, dw_w[l], dw_b[l][None, :], conv_ln_g[l][None, :], conv_ln_b[l][None, :])
        x1, h2 = _outproj(a_out, c_out, xf, mod3, norm_g[l, 1][None, :],
                          w_out[l, :NSA_WIDTH].astype(BF16), w_out[l, NSA_WIDTH:].astype(BF16))
        i1, i2, gw = _route(h2, peer_wq[l].astype(BF16),
                            peer_sub_keys[l].reshape(2 * PEER_HEADS, PEER_NKEYS, PEER_QDIM // 2).astype(BF16))
        xf = _peer(h2, peer_u[l].astype(BF16), peer_v[l].astype(BF16), i1, i2, gw, x1, mod3)
    return xf.reshape(BATCH, SEQ, D_MODEL)
```

```python
import functools

import numpy as np
import jax
import jax.numpy as jnp
from jax import lax
from jax.experimental import pallas as pl
from jax.experimental.pallas import tpu as pltpu

F32 = jnp.float32
BF16 = jnp.bfloat16

D_MODEL = 1024
BATCH = 8
SEQ = 2048
DEPTH = 1
TOKENS = BATCH * SEQ

HEAD_DIM = 64
NSA_HEADS = 8
NSA_GROUPS = 2
GQA_REP = NSA_HEADS // NSA_GROUPS
NSA_WIDTH = NSA_HEADS * HEAD_DIM
CONV_WIDTH = 512
CMP_LEN = 32
CMP_STRIDE = 16
N_CMP = (SEQ - CMP_LEN) // CMP_STRIDE + 1
SEL_LEN = 64
N_SEL = SEQ // SEL_LEN
SEL_TOPN = 16
WINDOW = 512
FORCE_BONUS = 1.0e4
CONV_TAPS = 31
PEER_HEADS = 8
PEER_NKEYS = 128
PEER_EXPERTS = PEER_NKEYS * PEER_NKEYS
PEER_QDIM = 256
PEER_TOPK = 16
NORM_EPS = 1e-6
NEG_INF = -1e30
NEG_BIG = -3.0e38
IDX_BIG = 1.0e9

LANES = 128
VMEM_V7X = 64 * 1024 * 1024
VMEM_LIMIT = VMEM_V7X * 3 // 4
VMEM_LIMIT_PEER = VMEM_V7X * 15 // 16

TM_PROJ = 512
TQ = 256
TK = 256
WIN_KEYS = WINDOW + TQ
V_ROWS = HEAD_DIM + 16
TS_CONV = 64
CONV_PAD = 32
TL_ROUTE = 256
TT_PEER = 512
ET_PEER = 1024
ES_PEER = 256
ROUTE_SPLIT = TT_PEER // TL_ROUTE
G_ROW_STRIDE = PEER_NKEYS + 8

_NT = (((1,), (1,)), ((), ()))


def _cparams(sem):
    return pltpu.CompilerParams(dimension_semantics=sem, vmem_limit_bytes=VMEM_LIMIT)


def _ada_body(c_ref, w_ref, b_ref, o_ref):
    c = c_ref[...]
    sc = (c * jax.nn.sigmoid(c)).astype(BF16)
    o_ref[...] = jnp.dot(sc, w_ref[...].astype(BF16), preferred_element_type=F32) + b_ref[...]


def _ada(c, w, b):
    n = w.shape[1]
    tn = 1536
    return pl.pallas_call(
        _ada_body,
        grid=(n // tn,),
        in_specs=[pl.BlockSpec((BATCH, D_MODEL), lambda j: (0, 0)),
                  pl.BlockSpec((D_MODEL, tn), lambda j: (0, j)),
                  pl.BlockSpec((1, tn), lambda j: (0, j))],
        out_specs=pl.BlockSpec((BATCH, tn), lambda j: (0, j)),
        out_shape=jax.ShapeDtypeStruct((BATCH, n), F32),
        compiler_params=_cparams(("arbitrary",)),
        name="ada_mod",
    )(c, w, b)


def _norm_mod(x, g, shift, scale):
    ms = jnp.mean(x * x, axis=-1, keepdims=True)
    y = x * lax.rsqrt(ms + NORM_EPS) * g
    return y * (1.0 + scale) + shift


def _inproj_body(x_ref, mod_ref, g_ref, wq_ref, wkc_ref, wks_ref, wg_ref, wglu_ref,
                 q_ref, kc_ref, ks_ref, gl_ref, glu_ref):
    h = _norm_mod(x_ref[...], g_ref[...], mod_ref[0, 0:1, :], mod_ref[0, 1:2, :]).astype(BF16)
    for w_ref, o_ref in ((wq_ref, q_ref), (wkc_ref, kc_ref), (wks_ref, ks_ref),
                         (wg_ref, gl_ref), (wglu_ref, glu_ref)):
        o_ref[...] = jnp.dot(h, w_ref[...], preferred_element_type=F32)


def _inproj(xf, mod3, g, wq, wkc, wks, wg, wglu):
    tiles_per_batch = SEQ // TM_PROJ
    ws = (wq, wkc, wks, wg, wglu)
    row = lambda i: (i, 0)
    return pl.pallas_call(
        _inproj_body,
        grid=(TOKENS // TM_PROJ,),
        in_specs=[pl.BlockSpec((TM_PROJ, D_MODEL), row),
                  pl.BlockSpec((1, 6, D_MODEL), lambda i: (i // tiles_per_batch, 0, 0)),
                  pl.BlockSpec((1, D_MODEL), lambda i: (0, 0))]
                 + [pl.BlockSpec(w.shape, lambda i: (0, 0)) for w in ws],
        out_specs=[pl.BlockSpec((TM_PROJ, w.shape[1]), row) for w in ws],
        out_shape=[jax.ShapeDtypeStruct((TOKENS, w.shape[1]), F32) for w in ws],
        compiler_params=_cparams(("parallel",)),
        name="in_proj",
    )(xf, mod3, g, *ws)


def _rms_pair(x, gdup, lo):
    x2 = x * x
    s_lo = jnp.sum(jnp.where(lo, x2, 0.0), axis=-1, keepdims=True)
    s_hi = jnp.sum(jnp.where(lo, 0.0, x2), axis=-1, keepdims=True)
    rs = jnp.where(lo, lax.rsqrt(s_lo * (1.0 / HEAD_DIM) + NORM_EPS),
                   lax.rsqrt(s_hi * (1.0 / HEAD_DIM) + NORM_EPS))
    return x * rs * gdup


def _key_ext(kn, lo, lane, pos):
    ext = jnp.where(lane == HEAD_DIM, (pos >> 6).astype(F32),
                    jnp.where(lane == HEAD_DIM + 1, (pos & (SEL_LEN - 1)).astype(F32),
                              jnp.where(lane == HEAD_DIM + 2, 1.0, 0.0)))
    return jnp.where(lo, kn, ext), jnp.where(lo, pltpu.roll(kn, HEAD_DIM, 1), ext)


def _kvprep_body(r_ref, kvs_ref, wa_ref, wb_ref, pea_ref, peb_ref, g1_ref, g2_ref, g3_ref,
                 kce_ref, vct_ref, kse_ref, vst_ref, kwe_ref, vwt_ref):
    lane = lax.broadcasted_iota(jnp.int32, (1, LANES), 1)
    lo = lane < HEAD_DIM
    nrow = SEQ // CMP_STRIDE
    r = r_ref[0]
    a = jnp.dot((r + pea_ref[...]).astype(BF16), wa_ref[...], preferred_element_type=F32)
    b = jnp.dot((r + peb_ref[...]).astype(BF16), wb_ref[...], preferred_element_type=F32)
    c = a + pltpu.roll(b, nrow - 1, 0)
    end = lax.broadcasted_iota(jnp.int32, (nrow, 1), 0) * CMP_STRIDE + (CMP_LEN - 1)
    kc0, kc1 = _key_ext(_rms_pair(c[:, :LANES], g1_ref[...], lo), lo, lane, end)
    kce_ref[0, 0] = kc0.astype(BF16)
    kce_ref[0, 1] = kc1.astype(BF16)
    vct = c[:, LANES:].T.astype(BF16)
    vct_ref[0, 0] = vct[:HEAD_DIM]
    vct_ref[0, 1] = vct[HEAD_DIM:]

    rows = 256
    ones = jnp.ones((V_ROWS - HEAD_DIM, rows), F32)

    def chunk(i, carry):
        r0 = pl.multiple_of(i * rows, rows)
        blk = kvs_ref[pl.ds(r0, rows), :]
        pos = r0 + lax.broadcasted_iota(jnp.int32, (rows, 1), 0)
        for k_ref, v_ref, gain, off in ((kse_ref, vst_ref, g2_ref, 0), (kwe_ref, vwt_ref, g3_ref, 2 * LANES)):
            k0, k1 = _key_ext(_rms_pair(blk[:, off:off + LANES], gain[...], lo), lo, lane, pos)
            k_ref[0, 0, pl.ds(r0, rows), :] = k0.astype(BF16)
            k_ref[0, 1, pl.ds(r0, rows), :] = k1.astype(BF16)
            vt = blk[:, off + LANES:off + 2 * LANES].T
            for g in range(NSA_GROUPS):
                v_ref[0, g, :, pl.ds(r0, rows)] = jnp.concatenate(
                    [vt[g * HEAD_DIM:(g + 1) * HEAD_DIM], ones], axis=0).astype(BF16)
        return carry

    lax.fori_loop(0, SEQ // rows, chunk, 0)


def _kvprep(rmat, kvs, wa, wb, pea, peb, g1, g2, g3):
    nrow = SEQ // CMP_STRIDE
    const2 = lambda b: (0, 0)
    per_b = lambda b: (b, 0, 0, 0)
    shapes = [(nrow, LANES), (HEAD_DIM, nrow), (SEQ, LANES), (V_ROWS, SEQ), (SEQ, LANES), (V_ROWS, SEQ)]
    return pl.pallas_call(
        _kvprep_body,
        grid=(BATCH,),
        in_specs=[pl.BlockSpec((1, nrow, rmat.shape[2]), lambda b: (b, 0, 0)),
                  pl.BlockSpec((SEQ, kvs.shape[1]), lambda b: (b, 0)),
                  pl.BlockSpec(wa.shape, const2), pl.BlockSpec(wb.shape, const2),
                  pl.BlockSpec(pea.shape, const2), pl.BlockSpec(peb.shape, const2),
                  pl.BlockSpec((1, LANES), const2), pl.BlockSpec((1, LANES), const2),
                  pl.BlockSpec((1, LANES), const2)],
        out_specs=[pl.BlockSpec((1, NSA_GROUPS) + s, per_b) for s in shapes],
        out_shape=[jax.ShapeDtypeStruct((BATCH, NSA_GROUPS) + s, BF16) for s in shapes],
        compiler_params=_cparams(("parallel",)),
        name="kv_prep",
    )(rmat, kvs, wa, wb, pea, peb, g1, g2, g3)


def _split3(x):
    p1 = x.astype(BF16)
    r1 = x - p1.astype(F32)
    p2 = r1.astype(BF16)
    p3 = (r1 - p2.astype(F32)).astype(BF16)
    return p1, p2, p3


def _nsa_body(q_ref, gl_ref, kce_ref, vct_ref, kse_ref, vst_ref, kwe_ref, vwt_ref,
              g0_ref, ovt_ref, et_ref, o_ref, s_scr):
    qi = pl.program_id(1)
    q0 = qi * TQ
    lane = lax.broadcasted_iota(jnp.int32, (1, LANES), 1)
    lo = lane < HEAD_DIM
    t_idx = q0 + lax.broadcasted_iota(jnp.int32, (1, TQ), 1)
    q0f = q0.astype(F32)

    n_sub = lax.broadcasted_iota(jnp.int32, (SEQ // CMP_STRIDE, 1), 0)
    cmask = (t_idx >= n_sub * CMP_STRIDE + (CMP_LEN - 1)) & (n_sub < N_CMP)
    j = lax.broadcasted_iota(jnp.int32, (N_SEL, 1), 0)
    tb = t_idx >> 6
    bonus = jnp.where((j == 0) | (j == tb) | (j == tb - 1), FORCE_BONUS, 0.0)

    qe, o_cmp, sels = [], [], []
    for g in range(NSA_GROUPS):
        for p in range(GQA_REP // 2):
            c0 = (g * GQA_REP // 2 + p) * LANES
            qn = _rms_pair(q_ref[:, c0:c0 + LANES], g0_ref[...], lo) * (HEAD_DIM ** -0.5)
            for half, base in ((0, qn), (1, pltpu.roll(qn, HEAD_DIM, 1))):
                slope = 2.0 ** -(g * GQA_REP + 2 * p + half + 1)
                ext = jnp.where(lane == HEAD_DIM, SEL_LEN * slope,
                                jnp.where(lane == HEAD_DIM + 1, slope,
                                          jnp.where(lane == HEAD_DIM + 2, -slope * q0f, 0.0)))
                qe.append(jnp.where(lo, base, ext).astype(BF16))
        heads = range(g * GQA_REP, (g + 1) * GQA_REP)

        kce = kce_ref[0, g]
        vct = vct_ref[0, g]
        psum = jnp.zeros((SEQ // CMP_STRIDE, TQ), F32)
        for h in heads:
            s = jnp.where(cmask, lax.dot_general(kce, qe[h], _NT, preferred_element_type=F32), NEG_INF)
            e = jnp.where(cmask, jnp.exp(s - jnp.max(s, axis=0, keepdims=True)), 0.0)
            l = jnp.sum(e, axis=0, keepdims=True)
            p = e / jnp.where(l > 0.0, l, 1.0)
            psum = psum + p
            o_cmp.append(jnp.dot(vct, p.astype(BF16), preferred_element_type=F32))

        imp = jnp.zeros((LANES, TQ), F32)
        for part in _split3(psum):
            imp = imp + jnp.dot(ovt_ref[...], part, preferred_element_type=F32)
        imp = jnp.where(j <= tb, imp[:N_SEL] + bonus, -1.0)
        rank = jnp.zeros((N_SEL, TQ), F32)
        for i in range(N_SEL):
            ci = imp[i:i + 1, :]
            ahead = (ci > imp) | ((ci == imp) & (j > i))
            rank = rank + jnp.where(ahead, 1.0, 0.0)
        sels.append(jnp.concatenate([jnp.where(rank < float(SEL_TOPN), 1.0, 0.0),
                                     jnp.zeros((LANES - N_SEL, TQ), F32)], axis=0).astype(BF16))

    key_sub = lax.broadcasted_iota(jnp.int32, (TK, 1), 0)
    neg_row = tuple(jnp.full((1, TQ), NEG_INF, F32) for _ in range(NSA_HEADS))
    zero_acc = tuple(jnp.zeros((V_ROWS, TQ), F32) for _ in range(NSA_HEADS))

    def two_pass(n_tiles, first_key, k_ref, v_ref, bias_fn):
        def score_step(i, ms):
            k0 = pl.multiple_of(first_key + i * TK, LANES)
            r0 = pl.multiple_of(i * TK, TK)
            out = []
            for g in range(NSA_GROUPS):
                kblk = k_ref[0, g, pl.ds(k0, TK), :]
                bias = bias_fn(g, k0)
                for h in range(g * GQA_REP, (g + 1) * GQA_REP):
                    s = lax.dot_general(kblk, qe[h], _NT, preferred_element_type=F32) + bias
                    s_scr[h, pl.ds(r0, TK), :] = s
                    out.append(jnp.maximum(ms[h], jnp.max(s, axis=0, keepdims=True)))
            return tuple(out)

        ms = lax.fori_loop(0, n_tiles, score_step, neg_row)

        def value_step(i, accs):
            k0 = pl.multiple_of(first_key + i * TK, LANES)
            r0 = pl.multiple_of(i * TK, TK)
            out = []
            for g in range(NSA_GROUPS):
                vt = v_ref[0, g, :, pl.ds(k0, TK)]
                for h in range(g * GQA_REP, (g + 1) * GQA_REP):
                    e = jnp.exp(s_scr[h, pl.ds(r0, TK), :] - ms[h])
                    out.append(accs[h] + jnp.dot(vt, e.astype(BF16), preferred_element_type=F32))
            return tuple(out)

        accs = lax.fori_loop(0, n_tiles, value_step, zero_acc)
        return [a[:HEAD_DIM] / a[HEAD_DIM:HEAD_DIM + 1] for a in accs]

    start = pl.multiple_of(jnp.maximum(q0 - WINDOW, 0), LANES)

    def window_bias(g, k0):
        wd = t_idx - (k0 + key_sub)
        return jnp.where((wd >= 0) & (wd < WINDOW), 0.0, NEG_INF)

    o_win = two_pass(WIN_KEYS // TK, start, kwe_ref, vwt_ref, window_bias)

    def selected_bias(g, k0):
        chosen = jnp.dot(et_ref[pl.ds(k0, TK), :], sels[g], preferred_element_type=F32) > 0.5
        return jnp.where(chosen & (k0 + key_sub <= t_idx), 0.0, NEG_INF)

    o_slc = two_pass((q0 + TQ - 1) // TK + 1, 0, kse_ref, vst_ref, selected_bias)

    for g in range(NSA_GROUPS):
        sg = jax.nn.sigmoid(gl_ref[:, g * LANES:(g + 1) * LANES].T[:4 * GQA_REP])
        for p in range(GQA_REP // 2):
            pair = []
            for r in (2 * p, 2 * p + 1):
                h = g * GQA_REP + r
                pair.append(sg[3 * r:3 * r + 1] * o_cmp[h] + sg[3 * r + 1:3 * r + 2] * o_slc[h]
                            + sg[3 * r + 2:3 * r + 3] * o_win[h])
            c0 = (g * GQA_REP // 2 + p) * LANES
            o_ref[:, c0:c0 + LANES] = jnp.concatenate(pair, axis=0).T.astype(o_ref.dtype)


def _nsa(q, gl, kce, vct, kse, vst, kwe, vwt, g0, ovt, et):
    nq = SEQ // TQ
    tile = lambda b, i: (b * nq + i, 0)
    per_b = lambda b, i: (b, 0, 0, 0)
    const2 = lambda b, i: (0, 0)
    return pl.pallas_call(
        _nsa_body,
        grid=(BATCH, nq),
        in_specs=[pl.BlockSpec((TQ, NSA_WIDTH), tile), pl.BlockSpec((TQ, NSA_GROUPS * LANES), tile)]
                 + [pl.BlockSpec((1,) + a.shape[1:], per_b) for a in (kce, vct, kse, vst, kwe, vwt)]
                 + [pl.BlockSpec((1, LANES), const2), pl.BlockSpec(ovt.shape, const2),
                    pl.BlockSpec(et.shape, const2)],
        out_specs=pl.BlockSpec((TQ, NSA_WIDTH), tile),
        out_shape=jax.ShapeDtypeStruct((TOKENS, NSA_WIDTH), BF16),
        scratch_shapes=[pltpu.VMEM((NSA_HEADS, SEQ, TQ), F32)],
        compiler_params=_cparams(("parallel", "arbitrary")),
        name="nsa_attention",
    )(q, gl, kce, vct, kse, vst, kwe, vwt, g0, ovt, et)


def _conv_body(glu_ref, w_ref, b_ref, lg_ref, lb_ref, o_ref, u_scr, sh_scr):
    u_scr[0:CONV_PAD, :] = jnp.zeros((CONV_PAD, CONV_WIDTH), F32)
    rows = 256

    def fill(i, carry):
        r0 = pl.multiple_of(i * rows, rows)
        blk = glu_ref[pl.ds(r0, rows), :]
        u_scr[pl.ds(CONV_PAD + r0, rows), :] = blk[:, :CONV_WIDTH] * jax.nn.sigmoid(blk[:, CONV_WIDTH:])
        return carry

    lax.fori_loop(0, SEQ // rows, fill, 0)
    first = CONV_PAD - (CONV_TAPS - 1)

    def tile(i, carry):
        r0 = pl.multiple_of(i * TS_CONV, TS_CONV)
        win = u_scr[pl.ds(r0, TS_CONV + CONV_PAD), :]
        span = TS_CONV + CONV_PAD - 8
        for s in range(1, 8):
            sh_scr[s - 1] = win[s:s + span, :]
        acc = jnp.zeros((TS_CONV, CONV_WIDTH), F32) + b_ref[...]
        for k in range(CONV_TAPS):
            s = (first + k) % 8
            base = first + k - s
            tap = win[base:base + TS_CONV, :] if s == 0 else sh_scr[s - 1, base:base + TS_CONV, :]
            acc = acc + tap * w_ref[k:k + 1, :]
        mu = jnp.mean(acc, axis=-1, keepdims=True)
        d = acc - mu
        var = jnp.mean(d * d, axis=-1, keepdims=True)
        yn = d * lax.rsqrt(var + NORM_EPS) * lg_ref[...] + lb_ref[...]
        o_ref[pl.ds(r0, TS_CONV), :] = (yn * jax.nn.sigmoid(yn)).astype(o_ref.dtype)
        return carry

    lax.fori_loop(0, SEQ // TS_CONV, tile, 0)


def _conv(glu, w, b, lg, lb):
    const2 = lambda i: (0, 0)
    return pl.pallas_call(
        _conv_body,
        grid=(BATCH,),
        in_specs=[pl.BlockSpec((SEQ, 2 * CONV_WIDTH), lambda i: (i, 0)),
                  pl.BlockSpec(w.shape, const2), pl.BlockSpec(b.shape, const2),
                  pl.BlockSpec(lg.shape, const2), pl.BlockSpec(lb.shape, const2)],
        out_specs=pl.BlockSpec((SEQ, CONV_WIDTH), lambda i: (i, 0)),
        out_shape=jax.ShapeDtypeStruct((TOKENS, CONV_WIDTH), BF16),
        scratch_shapes=[pltpu.VMEM((CONV_PAD + SEQ, CONV_WIDTH), F32),
                        pltpu.VMEM((7, TS_CONV + CONV_PAD - 8, CONV_WIDTH), F32)],
        compiler_params=_cparams(("parallel",)),
        name="conv_mixer",
    )(glu, w, b, lg, lb)


def _outproj_body(a_ref, c_ref, x_ref, mod_ref, g_ref, wa_ref, wc_ref, x1_ref, h2_ref):
    mix = (jnp.dot(a_ref[...], wa_ref[...], preferred_element_type=F32)
           + jnp.dot(c_ref[...], wc_ref[...], preferred_element_type=F32))
    x1 = x_ref[...] + mod_ref[0, 2:3, :] * mix
    x1_ref[...] = x1
    h2_ref[...] = _norm_mod(x1, g_ref[...], mod_ref[0, 3:4, :], mod_ref[0, 4:5, :]).astype(BF16)


def _outproj(a, c, xf, mod3, g, wa, wc):
    tiles_per_batch = SEQ // TM_PROJ
    row = lambda i: (i, 0)
    const2 = lambda i: (0, 0)
    return pl.pallas_call(
        _outproj_body,
        grid=(TOKENS // TM_PROJ,),
        in_specs=[pl.BlockSpec((TM_PROJ, NSA_WIDTH), row),
                  pl.BlockSpec((TM_PROJ, CONV_WIDTH), row),
                  pl.BlockSpec((TM_PROJ, D_MODEL), row),
                  pl.BlockSpec((1, 6, D_MODEL), lambda i: (i // tiles_per_batch, 0, 0)),
                  pl.BlockSpec((1, D_MODEL), const2),
                  pl.BlockSpec(wa.shape, const2), pl.BlockSpec(wc.shape, const2)],
        out_specs=[pl.BlockSpec((TM_PROJ, D_MODEL), row), pl.BlockSpec((TM_PROJ, D_MODEL), row)],
        out_shape=[jax.ShapeDtypeStruct((TOKENS, D_MODEL), F32),
                   jax.ShapeDtypeStruct((TOKENS, D_MODEL), BF16)],
        compiler_params=_cparams(("parallel",)),
        name="out_proj",
    )(a, c, xf, mod3, g, wa, wc)


_SORT4 = ((0, 1), (2, 3), (0, 2), (1, 3), (1, 2))
POPS_PER_SLICE = 4


def _run(gen):
    try:
        while True:
            next(gen)
    except StopIteration as stop:
        return stop.value


def _interleave(*gens):
    live = list(gens)
    while live:
        for gen in list(live):
            try:
                next(gen)
            except StopIteration:
                live.remove(gen)


def _topk_rows(x, k):
    n, cols = x.shape
    q = n // 4
    row = lax.broadcasted_iota(jnp.int32, (q, cols), 0).astype(F32)
    vals = [x[i * q:(i + 1) * q] for i in range(4)]
    idxs = [row + float(i * q) for i in range(4)]
    for i, j in _SORT4:
        swap = (vals[j] > vals[i]) | ((vals[j] == vals[i]) & (idxs[j] < idxs[i]))
        vals[i], vals[j] = jnp.where(swap, vals[j], vals[i]), jnp.where(swap, vals[i], vals[j])
        idxs[i], idxs[j] = jnp.where(swap, idxs[j], idxs[i]), jnp.where(swap, idxs[i], idxs[j])
    yield
    slot = lax.broadcasted_iota(jnp.int32, (k, cols), 0)
    out_v = jnp.zeros((k, cols), F32)
    out_i = jnp.zeros((k, cols), F32)
    for it in range(k):
        m = jnp.max(vals[0], axis=0, keepdims=True)
        idx = jnp.min(jnp.where(vals[0] == m, idxs[0], IDX_BIG), axis=0, keepdims=True)
        hit = idxs[0] == idx
        for lvl in range(3):
            vals[lvl] = jnp.where(hit, vals[lvl + 1], vals[lvl])
            idxs[lvl] = jnp.where(hit, idxs[lvl + 1], idxs[lvl])
        vals[3] = jnp.where(hit, NEG_BIG, vals[3])
        out_v = jnp.where(slot == it, m, out_v)
        out_i = jnp.where(slot == it, idx, out_i)
        if it % POPS_PER_SLICE == POPS_PER_SLICE - 1:
            yield
    return out_v, out_i


def _pair_topk(v1, v2):
    k, cols = v1.shape
    half = k // 2
    alo = lax.broadcasted_iota(jnp.int32, (half, cols), 0).astype(F32)
    ahi = alo + float(half)
    levels = []
    for b in range(k):
        lvl = v1[:half] + jnp.broadcast_to(v2[b:b + 1, :], (half, cols))
        if k // (b + 1) < half:
            lvl = jnp.where(alo < float(k // (b + 1)), lvl, NEG_BIG)
        levels.append(lvl)
    top_hi = v1[half:] + jnp.broadcast_to(v2[0:1, :], (half, cols))
    depth = jnp.zeros((half, cols), F32)
    slot = lax.broadcasted_iota(jnp.int32, (k, cols), 0)
    tops = jnp.zeros((k, cols), F32)
    a_out = jnp.zeros((k, cols), F32)
    b_out = jnp.zeros((k, cols), F32)
    for it in range(k):
        m = jnp.max(jnp.maximum(levels[0], top_hi), axis=0, keepdims=True)
        a_sel = jnp.min(jnp.minimum(jnp.where(levels[0] == m, alo, IDX_BIG),
                                    jnp.where(top_hi == m, ahi, IDX_BIG)), axis=0, keepdims=True)
        hit = alo == a_sel
        b_sel = jnp.sum(jnp.where(hit, depth, 0.0), axis=0, keepdims=True)
        depth = jnp.where(hit, depth + 1.0, depth)
        for b in range(k - 1):
            levels[b] = jnp.where(hit, levels[b + 1], levels[b])
        levels[k - 1] = jnp.where(hit, NEG_BIG, levels[k - 1])
        top_hi = jnp.where(ahi == a_sel, NEG_BIG, top_hi)
        tops = jnp.where(slot == it, m, tops)
        a_out = jnp.where(slot == it, a_sel, a_out)
        b_out = jnp.where(slot == it, b_sel, b_out)
        if it % POPS_PER_SLICE == POPS_PER_SLICE - 1:
            yield
    return tops, a_out, b_out


def _route_head(qp, key_a, key_b):
    k = PEER_TOPK
    tv, ti = [], []
    for c, keys in enumerate((key_a, key_b)):
        qs = qp[:, c * LANES:(c + 1) * LANES].astype(BF16)
        st = lax.dot_general(keys, qs, _NT, preferred_element_type=F32)
        v, i = yield from _topk_rows(st, k)
        tv.append(v)
        ti.append(i)
    tops, a_sel, b_sel = yield from _pair_topk(tv[0], tv[1])
    i1s = jnp.zeros_like(tops)
    i2s = jnp.zeros_like(tops)
    for a in range(k):
        i1s = i1s + jnp.where(a_sel == float(a), jnp.broadcast_to(ti[0][a:a + 1, :], tops.shape), 0.0)
        i2s = i2s + jnp.where(b_sel == float(a), jnp.broadcast_to(ti[1][a:a + 1, :], tops.shape), 0.0)
    e = jnp.exp(tops - jnp.max(tops, axis=0, keepdims=True))
    return i1s, i2s, e / jnp.sum(e, axis=0, keepdims=True)


def _route_body(h_ref, wq_ref, keys_ref, sel_ref, qp_scr):
    qp_scr[...] = jnp.dot(h_ref[...], wq_ref[...], preferred_element_type=F32)
    k = PEER_TOPK

    def head(h, carry):
        c0 = pl.multiple_of(h * PEER_QDIM, PEER_QDIM)
        parts = _run(_route_head(qp_scr[:, pl.ds(c0, PEER_QDIM)], keys_ref[2 * h], keys_ref[2 * h + 1]))
        r0 = pl.multiple_of(h * k, k)
        for a, val in enumerate(parts):
            sel_ref[a, pl.ds(r0, k), :] = val
        return carry

    lax.fori_loop(0, PEER_HEADS, head, 0, unroll=4)


def _route(h2, wq, keys):
    n = TT_PEER
    nsel = PEER_HEADS * PEER_TOPK
    return pl.pallas_call(
        _route_body,
        grid=(n // TL_ROUTE,),
        in_specs=[pl.BlockSpec((TL_ROUTE, D_MODEL), lambda i: (i, 0)),
                  pl.BlockSpec(wq.shape, lambda i: (0, 0)),
                  pl.BlockSpec(keys.shape, lambda i: (0, 0, 0))],
        out_specs=pl.BlockSpec((3, nsel, TL_ROUTE), lambda i: (0, 0, i)),
        out_shape=jax.ShapeDtypeStruct((3, nsel, n), F32),
        scratch_shapes=[pltpu.VMEM((TL_ROUTE, PEER_HEADS * PEER_QDIM), F32)],
        compiler_params=_cparams(("parallel",)),
        name="peer_route_first",
    )(h2, wq, keys)


def _peer_body(h_ref, hn_ref, wq_ref, keys_ref, u_ref, v_ref, first_ref, x1_ref, mod_ref, o_ref,
               g_scr, cur_scr, nxt_scr):
    i = pl.program_id(0)
    e = pl.program_id(1)
    n = PEER_NKEYS

    @pl.when(e == 0)
    def _():
        @pl.when(i == 0)
        def _():
            for a in range(3):
                cur_scr[a] = first_ref[a].T

        @pl.when(i > 0)
        def _():
            for a in range(3):
                cur_scr[a] = nxt_scr[a].T

        o_ref[...] = jnp.zeros(o_ref.shape, F32)
        sub = lax.broadcasted_iota(jnp.int32, (n, n), 0).astype(F32)

        def tok(t, carry):
            w = cur_scr[2, pl.ds(t, 1), :]
            w_hi = w.astype(BF16).astype(F32)
            m1 = sub == cur_scr[0, pl.ds(t, 1), :]
            x1 = jnp.concatenate([jnp.where(m1, w_hi, 0.0), jnp.where(m1, w - w_hi, 0.0)], axis=1).astype(BF16)
            x2h = jnp.where(sub == cur_scr[1, pl.ds(t, 1), :], 1.0, 0.0)
            x2 = jnp.concatenate([x2h, x2h], axis=1).astype(BF16)
            g = lax.dot_general(x1, x2, _NT, preferred_element_type=F32)
            g_scr[pl.ds(pl.multiple_of(t * G_ROW_STRIDE, 8), n), :] = g
            return carry

        lax.fori_loop(0, TT_PEER, tok, 0, unroll=64)

    def route_task():
        t0 = pl.multiple_of((e % ROUTE_SPLIT) * TL_ROUTE, TL_ROUTE)
        qp = jnp.dot(hn_ref[pl.ds(t0, TL_ROUTE), :], wq_ref[...], preferred_element_type=F32)
        parts = yield from _route_head(qp, keys_ref[0], keys_ref[1])
        r0 = pl.multiple_of((e // ROUTE_SPLIT) * PEER_TOPK, PEER_TOPK)
        for a, val in enumerate(parts):
            nxt_scr[a, pl.ds(r0, PEER_TOPK), pl.ds(t0, TL_ROUTE)] = val

    def expert_task():
        h = h_ref[...]
        parts = []
        for s in range(ET_PEER // ES_PEER):
            z = lax.dot_general(h, u_ref[s * ES_PEER:(s + 1) * ES_PEER, :], _NT,
                                preferred_element_type=F32)
            yield
            act = 0.5 * z * (1.0 + lax.erf(z * (2.0 ** -0.5)))
            for c in range(ES_PEER // n):
                i1 = e * (ET_PEER // n) + s * (ES_PEER // n) + c
                gc = g_scr[pl.ds(i1, TT_PEER, stride=G_ROW_STRIDE), :]
                parts.append((act[:, c * n:(c + 1) * n] * gc).astype(BF16))
                yield
        o_ref[...] += jnp.dot(jnp.concatenate(parts, axis=1), v_ref[...], preferred_element_type=F32)

    _interleave(expert_task(), route_task())

    @pl.when(e == pl.num_programs(1) - 1)
    def _():
        o_ref[...] = x1_ref[...] + mod_ref[0, 5:6, :] * o_ref[...]


def _peer(h2, wq, keys, u, v, first, x1, mod3):
    n_tiles = TOKENS // TT_PEER
    n_steps = PEER_EXPERTS // ET_PEER
    assert n_steps == PEER_HEADS * ROUTE_SPLIT
    tiles_per_batch = SEQ // TT_PEER
    nsel = PEER_HEADS * PEER_TOPK
    tok = lambda i, e: (i, 0)
    exp = lambda i, e: (e, 0)
    once = pl.Buffered(1)
    return pl.pallas_call(
        _peer_body,
        grid=(n_tiles, n_steps),
        in_specs=[pl.BlockSpec((TT_PEER, D_MODEL), tok, pipeline_mode=once),
                  pl.BlockSpec((TT_PEER, D_MODEL), lambda i, e: (jnp.minimum(i + 1, n_tiles - 1), 0),
                               pipeline_mode=once),
                  pl.BlockSpec((D_MODEL, PEER_QDIM), lambda i, e: (0, e // ROUTE_SPLIT)),
                  pl.BlockSpec((2, PEER_NKEYS, PEER_QDIM // 2), lambda i, e: (e // ROUTE_SPLIT, 0, 0)),
                  pl.BlockSpec((ET_PEER, D_MODEL), exp),
                  pl.BlockSpec((ET_PEER, D_MODEL), exp),
                  pl.BlockSpec((3, nsel, TT_PEER), lambda i, e: (0, 0, 0), pipeline_mode=once),
                  pl.BlockSpec((TT_PEER, D_MODEL), tok, pipeline_mode=once),
                  pl.BlockSpec((1, 6, D_MODEL), lambda i, e: (i // tiles_per_batch, 0, 0))],
        out_specs=pl.BlockSpec((TT_PEER, D_MODEL), tok),
        out_shape=jax.ShapeDtypeStruct((TOKENS, D_MODEL), F32),
        scratch_shapes=[pltpu.VMEM((TT_PEER * G_ROW_STRIDE, PEER_NKEYS), F32),
                        pltpu.VMEM((3, TT_PEER, nsel), F32),
                        pltpu.VMEM((3, nsel, TT_PEER), F32)],
        compiler_params=pltpu.CompilerParams(dimension_semantics=("arbitrary", "arbitrary"),
                                             vmem_limit_bytes=VMEM_LIMIT_PEER),
        name="peer_experts",
    )(h2, h2, wq, keys, u, v, first, x1, mod3)


def _overlap_matrix():
    start = np.arange(N_CMP)[None, :] * CMP_STRIDE
    sel = np.arange(N_SEL)[:, None] * SEL_LEN
    ov = np.clip(np.minimum(start + CMP_LEN, sel + SEL_LEN) - np.maximum(start, sel), 0, None) / CMP_LEN
    out = np.zeros((LANES, LANES), np.float32)
    out[:N_SEL, :N_CMP] = ov
    return out


def _block_expand_matrix():
    out = np.zeros((SEQ, LANES), np.float32)
    out[np.arange(SEQ), np.arange(SEQ) // SEL_LEN] = 1.0
    return out


def _cmp_weights(wk, wv, first):
    width = 2 * NSA_GROUPS * HEAD_DIM
    blocks = [w[first:first + CMP_STRIDE].astype(BF16) for w in (wk, wv) for _ in range(NSA_GROUPS)]
    zero = jnp.zeros_like(blocks[0])
    rows = [jnp.concatenate([blk if i == j else zero for j in range(len(blocks))], axis=2)
            for i, blk in enumerate(blocks)]
    return jnp.concatenate(rows, axis=1).reshape(CMP_STRIDE * width, width)


def _dup(v):
    return jnp.concatenate([v, v])[None, :]


def kernel(x, c, w_ada, b_ada, norm_g, w_in, w_out, cmp_pe_k, cmp_pe_v, w_cmp_k, w_cmp_v, qk_norm_g,
           dw_w, dw_b, conv_ln_g, conv_ln_b, peer_wq, peer_sub_keys, peer_u, peer_v):
    assert x.shape == (BATCH, SEQ, D_MODEL) and w_ada.shape[0] == DEPTH
    ovt = jnp.asarray(_overlap_matrix(), BF16)
    emat = jnp.asarray(_block_expand_matrix(), BF16)
    o_kv = NSA_WIDTH
    o_gate = o_kv + 6 * NSA_GROUPS * HEAD_DIM
    o_glu = o_gate + 3 * NSA_HEADS
    n_cmp_cols = 2 * NSA_GROUPS * HEAD_DIM
    xf = x.reshape(TOKENS, D_MODEL)
    for l in range(DEPTH):
        mod3 = _ada(c, w_ada[l], b_ada[l][None, :]).reshape(BATCH, 6, D_MODEL)
        wi = w_in[l]
        gate_pad = jnp.zeros((D_MODEL, LANES - 3 * GQA_REP), F32)
        wg = jnp.concatenate(
            [part for g in range(NSA_GROUPS)
             for part in (wi[:, o_gate + 3 * GQA_REP * g:o_gate + 3 * GQA_REP * (g + 1)], gate_pad)], axis=1)
        q, kvc, kvs, gl, glu = _inproj(
            xf, mod3, norm_g[l, 0][None, :],
            wi[:, :o_kv].astype(BF16), wi[:, o_kv:o_kv + n_cmp_cols].astype(BF16),
            wi[:, o_kv + n_cmp_cols:o_gate].astype(BF16), wg.astype(BF16), wi[:, o_glu:].astype(BF16))
        pe = jnp.concatenate([cmp_pe_k[l], cmp_pe_k[l], cmp_pe_v[l], cmp_pe_v[l]], axis=1)
        kv_parts = _kvprep(
            kvc.reshape(BATCH, SEQ // CMP_STRIDE, CMP_STRIDE * n_cmp_cols), kvs,
            _cmp_weights(w_cmp_k[l], w_cmp_v[l], 0), _cmp_weights(w_cmp_k[l], w_cmp_v[l], CMP_STRIDE),
            pe[:CMP_STRIDE].reshape(1, -1), pe[CMP_STRIDE:].reshape(1, -1),
            _dup(qk_norm_g[l, 1]), _dup(qk_norm_g[l, 2]), _dup(qk_norm_g[l, 3]))
        a_out = _nsa(q, gl, *kv_parts, _dup(qk_norm_g[l, 0]), ovt, emat)
        c_out = _conv(glu, dw_w[l], dw_b[l][None, :], conv_ln_g[l][None, :], conv_ln_b[l][None, :])
        x1, h2 = _outproj(a_out, c_out, xf, mod3, norm_g[l, 1][None, :],
                          w_out[l, :NSA_WIDTH].astype(BF16), w_out[l, NSA_WIDTH:].astype(BF16))
        wq = peer_wq[l].astype(BF16)
        keys = peer_sub_keys[l].reshape(2 * PEER_HEADS, PEER_NKEYS, PEER_QDIM // 2).astype(BF16)
        xf = _peer(h2, wq, keys, peer_u[l].astype(BF16), peer_v[l].astype(BF16), _route(h2, wq, keys), x1, mod3)
    return xf.reshape(BATCH, SEQ, D_MODEL)
```

```python
import functools

import numpy as np
import jax
import jax.numpy as jnp
from jax import lax
from jax.experimental import pallas as pl
from jax.experimental.pallas import tpu as pltpu

F32 = jnp.float32
BF16 = jnp.bfloat16

D_MODEL = 1024
BATCH = 8
SEQ = 2048
DEPTH = 1
TOKENS = BATCH * SEQ

HEAD_DIM = 64
NSA_HEADS = 8
NSA_GROUPS = 2
GQA_REP = NSA_HEADS // NSA_GROUPS
NSA_WIDTH = NSA_HEADS * HEAD_DIM
CONV_WIDTH = 512
CMP_LEN = 32
CMP_STRIDE = 16
N_CMP = (SEQ - CMP_LEN) // CMP_STRIDE + 1
SEL_LEN = 64
N_SEL = SEQ // SEL_LEN
SEL_TOPN = 16
WINDOW = 512
FORCE_BONUS = 1.0e4
CONV_TAPS = 31
PEER_HEADS = 8
PEER_NKEYS = 128
PEER_EXPERTS = PEER_NKEYS * PEER_NKEYS
PEER_QDIM = 256
PEER_TOPK = 16
NORM_EPS = 1e-6
NEG_INF = -1e30
NEG_BIG = -3.0e38
IDX_BIG = 1.0e9

LANES = 128
VMEM_V7X = 64 * 1024 * 1024
VMEM_LIMIT = VMEM_V7X * 3 // 4
VMEM_LIMIT_PEER = VMEM_V7X * 15 // 16

TM_PROJ = 512
TQ = 256
TK = 256
WIN_KEYS = WINDOW + TQ
V_ROWS = HEAD_DIM + 16
TS_CONV = 64
CONV_PAD = 32
TL_ROUTE = 256
TT_PEER = 512
ET_PEER = 1024
ES_PEER = 256
ROUTE_SPLIT = TT_PEER // TL_ROUTE
G_ROW_STRIDE = PEER_NKEYS + 8

_NT = (((1,), (1,)), ((), ()))


def _cparams(sem):
    return pltpu.CompilerParams(dimension_semantics=sem, vmem_limit_bytes=VMEM_LIMIT)


def _ada_body(c_ref, w_ref, b_ref, o_ref):
    c = c_ref[...]
    sc = (c * jax.nn.sigmoid(c)).astype(BF16)
    o_ref[...] = jnp.dot(sc, w_ref[...].astype(BF16), preferred_element_type=F32) + b_ref[...]


def _ada(c, w, b):
    n = w.shape[1]
    tn = 1536
    return pl.pallas_call(
        _ada_body,
        grid=(n // tn,),
        in_specs=[pl.BlockSpec((BATCH, D_MODEL), lambda j: (0, 0)),
                  pl.BlockSpec((D_MODEL, tn), lambda j: (0, j)),
                  pl.BlockSpec((1, tn), lambda j: (0, j))],
        out_specs=pl.BlockSpec((BATCH, tn), lambda j: (0, j)),
        out_shape=jax.ShapeDtypeStruct((BATCH, n), F32),
        compiler_params=_cparams(("arbitrary",)),
        name="ada_mod",
    )(c, w, b)


def _norm_mod(x, g, shift, scale):
    ms = jnp.mean(x * x, axis=-1, keepdims=True)
    y = x * lax.rsqrt(ms + NORM_EPS) * g
    return y * (1.0 + scale) + shift


def _inproj_body(x_ref, mod_ref, g_ref, wq_ref, wkc_ref, wks_ref, wg_ref, wglu_ref,
                 q_ref, kc_ref, ks_ref, gl_ref, glu_ref):
    h = _norm_mod(x_ref[...], g_ref[...], mod_ref[0, 0:1, :], mod_ref[0, 1:2, :]).astype(BF16)
    for w_ref, o_ref in ((wq_ref, q_ref), (wkc_ref, kc_ref), (wks_ref, ks_ref),
                         (wg_ref, gl_ref), (wglu_ref, glu_ref)):
        o_ref[...] = jnp.dot(h, w_ref[...], preferred_element_type=F32)


def _inproj(xf, mod3, g, wq, wkc, wks, wg, wglu):
    tiles_per_batch = SEQ // TM_PROJ
    ws = (wq, wkc, wks, wg, wglu)
    row = lambda i: (i, 0)
    return pl.pallas_call(
        _inproj_body,
        grid=(TOKENS // TM_PROJ,),
        in_specs=[pl.BlockSpec((TM_PROJ, D_MODEL), row),
                  pl.BlockSpec((1, 6, D_MODEL), lambda i: (i // tiles_per_batch, 0, 0)),
                  pl.BlockSpec((1, D_MODEL), lambda i: (0, 0))]
                 + [pl.BlockSpec(w.shape, lambda i: (0, 0)) for w in ws],
        out_specs=[pl.BlockSpec((TM_PROJ, w.shape[1]), row) for w in ws],
        out_shape=[jax.ShapeDtypeStruct((TOKENS, w.shape[1]), F32) for w in ws],
        compiler_params=_cparams(("parallel",)),
        name="in_proj",
    )(xf, mod3, g, *ws)


def _rms_pair(x, gdup, lo):
    x2 = x * x
    s_lo = jnp.sum(jnp.where(lo, x2, 0.0), axis=-1, keepdims=True)
    s_hi = jnp.sum(jnp.where(lo, 0.0, x2), axis=-1, keepdims=True)
    rs = jnp.where(lo, lax.rsqrt(s_lo * (1.0 / HEAD_DIM) + NORM_EPS),
                   lax.rsqrt(s_hi * (1.0 / HEAD_DIM) + NORM_EPS))
    return x * rs * gdup


def _key_ext(kn, lo, lane, pos):
    ext = jnp.where(lane == HEAD_DIM, (pos >> 6).astype(F32),
                    jnp.where(lane == HEAD_DIM + 1, (pos & (SEL_LEN - 1)).astype(F32),
                              jnp.where(lane == HEAD_DIM + 2, 1.0, 0.0)))
    return jnp.where(lo, kn, ext), jnp.where(lo, pltpu.roll(kn, HEAD_DIM, 1), ext)


def _kvprep_body(r_ref, kvs_ref, wa_ref, wb_ref, pea_ref, peb_ref, g1_ref, g2_ref, g3_ref,
                 kce_ref, vct_ref, kse_ref, vst_ref, kwe_ref, vwt_ref):
    lane = lax.broadcasted_iota(jnp.int32, (1, LANES), 1)
    lo = lane < HEAD_DIM
    nrow = SEQ // CMP_STRIDE
    r = r_ref[0]
    a = jnp.dot((r + pea_ref[...]).astype(BF16), wa_ref[...], preferred_element_type=F32)
    b = jnp.dot((r + peb_ref[...]).astype(BF16), wb_ref[...], preferred_element_type=F32)
    c = a + pltpu.roll(b, nrow - 1, 0)
    end = lax.broadcasted_iota(jnp.int32, (nrow, 1), 0) * CMP_STRIDE + (CMP_LEN - 1)
    kc0, kc1 = _key_ext(_rms_pair(c[:, :LANES], g1_ref[...], lo), lo, lane, end)
    kce_ref[0, 0] = kc0.astype(BF16)
    kce_ref[0, 1] = kc1.astype(BF16)
    vct = c[:, LANES:].T.astype(BF16)
    vct_ref[0, 0] = vct[:HEAD_DIM]
    vct_ref[0, 1] = vct[HEAD_DIM:]

    rows = 256
    ones = jnp.ones((V_ROWS - HEAD_DIM, rows), F32)

    def chunk(i, carry):
        r0 = pl.multiple_of(i * rows, rows)
        blk = kvs_ref[pl.ds(r0, rows), :]
        pos = r0 + lax.broadcasted_iota(jnp.int32, (rows, 1), 0)
        for k_ref, v_ref, gain, off in ((kse_ref, vst_ref, g2_ref, 0), (kwe_ref, vwt_ref, g3_ref, 2 * LANES)):
            k0, k1 = _key_ext(_rms_pair(blk[:, off:off + LANES], gain[...], lo), lo, lane, pos)
            k_ref[0, 0, pl.ds(r0, rows), :] = k0.astype(BF16)
            k_ref[0, 1, pl.ds(r0, rows), :] = k1.astype(BF16)
            vt = blk[:, off + LANES:off + 2 * LANES].T
            for g in range(NSA_GROUPS):
                v_ref[0, g, :, pl.ds(r0, rows)] = jnp.concatenate(
                    [vt[g * HEAD_DIM:(g + 1) * HEAD_DIM], ones], axis=0).astype(BF16)
        return carry

    lax.fori_loop(0, SEQ // rows, chunk, 0)


def _kvprep(rmat, kvs, wa, wb, pea, peb, g1, g2, g3):
    nrow = SEQ // CMP_STRIDE
    const2 = lambda b: (0, 0)
    per_b = lambda b: (b, 0, 0, 0)
    shapes = [(nrow, LANES), (HEAD_DIM, nrow), (SEQ, LANES), (V_ROWS, SEQ), (SEQ, LANES), (V_ROWS, SEQ)]
    return pl.pallas_call(
        _kvprep_body,
        grid=(BATCH,),
        in_specs=[pl.BlockSpec((1, nrow, rmat.shape[2]), lambda b: (b, 0, 0)),
                  pl.BlockSpec((SEQ, kvs.shape[1]), lambda b: (b, 0)),
                  pl.BlockSpec(wa.shape, const2), pl.BlockSpec(wb.shape, const2),
                  pl.BlockSpec(pea.shape, const2), pl.BlockSpec(peb.shape, const2),
                  pl.BlockSpec((1, LANES), const2), pl.BlockSpec((1, LANES), const2),
                  pl.BlockSpec((1, LANES), const2)],
        out_specs=[pl.BlockSpec((1, NSA_GROUPS) + s, per_b) for s in shapes],
        out_shape=[jax.ShapeDtypeStruct((BATCH, NSA_GROUPS) + s, BF16) for s in shapes],
        compiler_params=_cparams(("parallel",)),
        name="kv_prep",
    )(rmat, kvs, wa, wb, pea, peb, g1, g2, g3)


def _loop(trips, body, init):
    if isinstance(trips, int):
        for i in range(trips):
            init = body(i, init)
        return init
    return lax.fori_loop(0, trips, body, init)


def _split3(x):
    p1 = x.astype(BF16)
    r1 = x - p1.astype(F32)
    p2 = r1.astype(BF16)
    p3 = (r1 - p2.astype(F32)).astype(BF16)
    return p1, p2, p3


def _nsa_body(q_ref, gl_ref, kce_ref, vct_ref, kse_ref, vst_ref, kwe_ref, vwt_ref,
              g0_ref, ovt_ref, et_ref, o_ref, s_scr):
    qi = pl.program_id(1)
    q0 = qi * TQ
    lane = lax.broadcasted_iota(jnp.int32, (1, LANES), 1)
    lo = lane < HEAD_DIM
    t_idx = q0 + lax.broadcasted_iota(jnp.int32, (1, TQ), 1)
    q0f = q0.astype(F32)

    n_sub = lax.broadcasted_iota(jnp.int32, (SEQ // CMP_STRIDE, 1), 0)
    cmask = (t_idx >= n_sub * CMP_STRIDE + (CMP_LEN - 1)) & (n_sub < N_CMP)
    j = lax.broadcasted_iota(jnp.int32, (N_SEL, 1), 0)
    tb = t_idx >> 6
    bonus = jnp.where((j == 0) | (j == tb) | (j == tb - 1), FORCE_BONUS, 0.0)

    qe, o_cmp, sels = [], [], []
    for g in range(NSA_GROUPS):
        for p in range(GQA_REP // 2):
            c0 = (g * GQA_REP // 2 + p) * LANES
            qn = _rms_pair(q_ref[:, c0:c0 + LANES], g0_ref[...], lo) * (HEAD_DIM ** -0.5)
            for half, base in ((0, qn), (1, pltpu.roll(qn, HEAD_DIM, 1))):
                slope = 2.0 ** -(g * GQA_REP + 2 * p + half + 1)
                ext = jnp.where(lane == HEAD_DIM, SEL_LEN * slope,
                                jnp.where(lane == HEAD_DIM + 1, slope,
                                          jnp.where(lane == HEAD_DIM + 2, -slope * q0f, 0.0)))
                qe.append(jnp.where(lo, base, ext).astype(BF16))
        heads = range(g * GQA_REP, (g + 1) * GQA_REP)

        kce = kce_ref[0, g]
        vct = vct_ref[0, g]
        psum = jnp.zeros((SEQ // CMP_STRIDE, TQ), F32)
        for h in heads:
            s = jnp.where(cmask, lax.dot_general(kce, qe[h], _NT, preferred_element_type=F32), NEG_INF)
            e = jnp.where(cmask, jnp.exp(s - jnp.max(s, axis=0, keepdims=True)), 0.0)
            l = jnp.sum(e, axis=0, keepdims=True)
            p = e / jnp.where(l > 0.0, l, 1.0)
            psum = psum + p
            o_cmp.append(jnp.dot(vct, p.astype(BF16), preferred_element_type=F32))

        imp = jnp.zeros((LANES, TQ), F32)
        for part in _split3(psum):
            imp = imp + jnp.dot(ovt_ref[...], part, preferred_element_type=F32)
        imp = jnp.where(j <= tb, imp[:N_SEL] + bonus, -1.0)
        rank = jnp.zeros((N_SEL, TQ), F32)
        for i in range(N_SEL):
            ci = imp[i:i + 1, :]
            ahead = (ci > imp) | ((ci == imp) & (j > i))
            rank = rank + jnp.where(ahead, 1.0, 0.0)
        sels.append(jnp.concatenate([jnp.where(rank < float(SEL_TOPN), 1.0, 0.0),
                                     jnp.zeros((LANES - N_SEL, TQ), F32)], axis=0).astype(BF16))

    key_sub = lax.broadcasted_iota(jnp.int32, (TK, 1), 0)
    neg_row = tuple(jnp.full((1, TQ), NEG_INF, F32) for _ in range(NSA_HEADS))
    zero_acc = tuple(jnp.zeros((V_ROWS, TQ), F32) for _ in range(NSA_HEADS))

    def two_pass(n_tiles, first_key, k_ref, v_ref, bias_fn):
        def score_step(i, ms):
            k0 = pl.multiple_of(first_key + i * TK, LANES)
            r0 = pl.multiple_of(i * TK, TK)
            out = []
            for g in range(NSA_GROUPS):
                kblk = k_ref[0, g, pl.ds(k0, TK), :]
                bias = bias_fn(g, k0)
                for h in range(g * GQA_REP, (g + 1) * GQA_REP):
                    s = lax.dot_general(kblk, qe[h], _NT, preferred_element_type=F32) + bias
                    s_scr[h, pl.ds(r0, TK), :] = s
                    out.append(jnp.maximum(ms[h], jnp.max(s, axis=0, keepdims=True)))
            return tuple(out)

        ms = _loop(n_tiles, score_step, neg_row)

        def value_step(i, accs):
            k0 = pl.multiple_of(first_key + i * TK, LANES)
            r0 = pl.multiple_of(i * TK, TK)
            out = []
            for g in range(NSA_GROUPS):
                vt = v_ref[0, g, :, pl.ds(k0, TK)]
                for h in range(g * GQA_REP, (g + 1) * GQA_REP):
                    e = jnp.exp(s_scr[h, pl.ds(r0, TK), :] - ms[h])
                    out.append(accs[h] + jnp.dot(vt, e.astype(BF16), preferred_element_type=F32))
            return tuple(out)

        accs = _loop(n_tiles, value_step, zero_acc)
        return [a[:HEAD_DIM] / a[HEAD_DIM:HEAD_DIM + 1] for a in accs]

    start = pl.multiple_of(jnp.maximum(q0 - WINDOW, 0), LANES)

    def window_bias(g, k0):
        wd = t_idx - (k0 + key_sub)
        return jnp.where((wd >= 0) & (wd < WINDOW), 0.0, NEG_INF)

    o_win = two_pass(WIN_KEYS // TK, start, kwe_ref, vwt_ref, window_bias)

    def selected_bias(g, k0):
        chosen = jnp.dot(et_ref[pl.ds(k0, TK), :], sels[g], preferred_element_type=F32) > 0.5
        return jnp.where(chosen & (k0 + key_sub <= t_idx), 0.0, NEG_INF)

    o_slc = two_pass((q0 + TQ - 1) // TK + 1, 0, kse_ref, vst_ref, selected_bias)

    for g in range(NSA_GROUPS):
        sg = jax.nn.sigmoid(gl_ref[:, g * LANES:(g + 1) * LANES].T[:4 * GQA_REP])
        for p in range(GQA_REP // 2):
            pair = []
            for r in (2 * p, 2 * p + 1):
                h = g * GQA_REP + r
                pair.append(sg[3 * r:3 * r + 1] * o_cmp[h] + sg[3 * r + 1:3 * r + 2] * o_slc[h]
                            + sg[3 * r + 2:3 * r + 3] * o_win[h])
            c0 = (g * GQA_REP // 2 + p) * LANES
            o_ref[:, c0:c0 + LANES] = jnp.concatenate(pair, axis=0).T.astype(o_ref.dtype)


def _nsa(q, gl, kce, vct, kse, vst, kwe, vwt, g0, ovt, et):
    nq = SEQ // TQ
    tile = lambda b, i: (b * nq + i, 0)
    per_b = lambda b, i: (b, 0, 0, 0)
    const2 = lambda b, i: (0, 0)
    return pl.pallas_call(
        _nsa_body,
        grid=(BATCH, nq),
        in_specs=[pl.BlockSpec((TQ, NSA_WIDTH), tile), pl.BlockSpec((TQ, NSA_GROUPS * LANES), tile)]
                 + [pl.BlockSpec((1,) + a.shape[1:], per_b) for a in (kce, vct, kse, vst, kwe, vwt)]
                 + [pl.BlockSpec((1, LANES), const2), pl.BlockSpec(ovt.shape, const2),
                    pl.BlockSpec(et.shape, const2)],
        out_specs=pl.BlockSpec((TQ, NSA_WIDTH), tile),
        out_shape=jax.ShapeDtypeStruct((TOKENS, NSA_WIDTH), BF16),
        scratch_shapes=[pltpu.VMEM((NSA_HEADS, SEQ, TQ), F32)],
        compiler_params=_cparams(("parallel", "arbitrary")),
        name="nsa_attention",
    )(q, gl, kce, vct, kse, vst, kwe, vwt, g0, ovt, et)


def _conv_body(glu_ref, w_ref, b_ref, lg_ref, lb_ref, o_ref, u_scr, sh_scr):
    u_scr[0:CONV_PAD, :] = jnp.zeros((CONV_PAD, CONV_WIDTH), F32)
    rows = 256

    def fill(i, carry):
        r0 = pl.multiple_of(i * rows, rows)
        blk = glu_ref[pl.ds(r0, rows), :]
        u_scr[pl.ds(CONV_PAD + r0, rows), :] = blk[:, :CONV_WIDTH] * jax.nn.sigmoid(blk[:, CONV_WIDTH:])
        return carry

    lax.fori_loop(0, SEQ // rows, fill, 0)
    first = CONV_PAD - (CONV_TAPS - 1)

    def tile(i, carry):
        r0 = pl.multiple_of(i * TS_CONV, TS_CONV)
        win = u_scr[pl.ds(r0, TS_CONV + CONV_PAD), :]
        span = TS_CONV + CONV_PAD - 8
        for s in range(1, 8):
            sh_scr[s - 1] = win[s:s + span, :]
        acc = jnp.zeros((TS_CONV, CONV_WIDTH), F32) + b_ref[...]
        for k in range(CONV_TAPS):
            s = (first + k) % 8
            base = first + k - s
            tap = win[base:base + TS_CONV, :] if s == 0 else sh_scr[s - 1, base:base + TS_CONV, :]
            acc = acc + tap * w_ref[k:k + 1, :]
        mu = jnp.mean(acc, axis=-1, keepdims=True)
        d = acc - mu
        var = jnp.mean(d * d, axis=-1, keepdims=True)
        yn = d * lax.rsqrt(var + NORM_EPS) * lg_ref[...] + lb_ref[...]
        o_ref[pl.ds(r0, TS_CONV), :] = (yn * jax.nn.sigmoid(yn)).astype(o_ref.dtype)
        return carry

    lax.fori_loop(0, SEQ // TS_CONV, tile, 0)


def _conv(glu, w, b, lg, lb):
    const2 = lambda i: (0, 0)
    return pl.pallas_call(
        _conv_body,
        grid=(BATCH,),
        in_specs=[pl.BlockSpec((SEQ, 2 * CONV_WIDTH), lambda i: (i, 0)),
                  pl.BlockSpec(w.shape, const2), pl.BlockSpec(b.shape, const2),
                  pl.BlockSpec(lg.shape, const2), pl.BlockSpec(lb.shape, const2)],
        out_specs=pl.BlockSpec((SEQ, CONV_WIDTH), lambda i: (i, 0)),
        out_shape=jax.ShapeDtypeStruct((TOKENS, CONV_WIDTH), BF16),
        scratch_shapes=[pltpu.VMEM((CONV_PAD + SEQ, CONV_WIDTH), F32),
                        pltpu.VMEM((7, TS_CONV + CONV_PAD - 8, CONV_WIDTH), F32)],
        compiler_params=_cparams(("parallel",)),
        name="conv_mixer",
    )(glu, w, b, lg, lb)


def _outproj_body(a_ref, c_ref, x_ref, mod_ref, g_ref, wa_ref, wc_ref, x1_ref, h2_ref):
    mix = (jnp.dot(a_ref[...], wa_ref[...], preferred_element_type=F32)
           + jnp.dot(c_ref[...], wc_ref[...], preferred_element_type=F32))
    x1 = x_ref[...] + mod_ref[0, 2:3, :] * mix
    x1_ref[...] = x1
    h2_ref[...] = _norm_mod(x1, g_ref[...], mod_ref[0, 3:4, :], mod_ref[0, 4:5, :]).astype(BF16)


def _outproj(a, c, xf, mod3, g, wa, wc):
    tiles_per_batch = SEQ // TM_PROJ
    row = lambda i: (i, 0)
    const2 = lambda i: (0, 0)
    return pl.pallas_call(
        _outproj_body,
        grid=(TOKENS // TM_PROJ,),
        in_specs=[pl.BlockSpec((TM_PROJ, NSA_WIDTH), row),
                  pl.BlockSpec((TM_PROJ, CONV_WIDTH), row),
                  pl.BlockSpec((TM_PROJ, D_MODEL), row),
                  pl.BlockSpec((1, 6, D_MODEL), lambda i: (i // tiles_per_batch, 0, 0)),
                  pl.BlockSpec((1, D_MODEL), const2),
                  pl.BlockSpec(wa.shape, const2), pl.BlockSpec(wc.shape, const2)],
        out_specs=[pl.BlockSpec((TM_PROJ, D_MODEL), row), pl.BlockSpec((TM_PROJ, D_MODEL), row)],
        out_shape=[jax.ShapeDtypeStruct((TOKENS, D_MODEL), F32),
                   jax.ShapeDtypeStruct((TOKENS, D_MODEL), BF16)],
        compiler_params=_cparams(("parallel",)),
        name="out_proj",
    )(a, c, xf, mod3, g, wa, wc)


_SORT4 = ((0, 1), (2, 3), (0, 2), (1, 3), (1, 2))
POPS_PER_SLICE = 4


def _run(gen):
    try:
        while True:
            next(gen)
    except StopIteration as stop:
        return stop.value


def _interleave(*gens):
    live = list(gens)
    while live:
        for gen in list(live):
            try:
                next(gen)
            except StopIteration:
                live.remove(gen)


def _topk_rows(x, k):
    n, cols = x.shape
    q = n // 4
    row = lax.broadcasted_iota(jnp.int32, (q, cols), 0).astype(F32)
    vals = [x[i * q:(i + 1) * q] for i in range(4)]
    idxs = [row + float(i * q) for i in range(4)]
    for i, j in _SORT4:
        swap = (vals[j] > vals[i]) | ((vals[j] == vals[i]) & (idxs[j] < idxs[i]))
        vals[i], vals[j] = jnp.where(swap, vals[j], vals[i]), jnp.where(swap, vals[i], vals[j])
        idxs[i], idxs[j] = jnp.where(swap, idxs[j], idxs[i]), jnp.where(swap, idxs[i], idxs[j])
    yield
    slot = lax.broadcasted_iota(jnp.int32, (k, cols), 0)
    out_v = jnp.zeros((k, cols), F32)
    out_i = jnp.zeros((k, cols), F32)
    for it in range(k):
        m = jnp.max(vals[0], axis=0, keepdims=True)
        idx = jnp.min(jnp.where(vals[0] == m, idxs[0], IDX_BIG), axis=0, keepdims=True)
        hit = idxs[0] == idx
        for lvl in range(3):
            vals[lvl] = jnp.where(hit, vals[lvl + 1], vals[lvl])
            idxs[lvl] = jnp.where(hit, idxs[lvl + 1], idxs[lvl])
        vals[3] = jnp.where(hit, NEG_BIG, vals[3])
        out_v = jnp.where(slot == it, m, out_v)
        out_i = jnp.where(slot == it, idx, out_i)
        if it % POPS_PER_SLICE == POPS_PER_SLICE - 1:
            yield
    return out_v, out_i


def _pair_topk(v1, v2):
    k, cols = v1.shape
    half = k // 2
    alo = lax.broadcasted_iota(jnp.int32, (half, cols), 0).astype(F32)
    ahi = alo + float(half)
    levels = []
    for b in range(k):
        lvl = v1[:half] + jnp.broadcast_to(v2[b:b + 1, :], (half, cols))
        if k // (b + 1) < half:
            lvl = jnp.where(alo < float(k // (b + 1)), lvl, NEG_BIG)
        levels.append(lvl)
    top_hi = v1[half:] + jnp.broadcast_to(v2[0:1, :], (half, cols))
    depth = jnp.zeros((half, cols), F32)
    slot = lax.broadcasted_iota(jnp.int32, (k, cols), 0)
    tops = jnp.zeros((k, cols), F32)
    a_out = jnp.zeros((k, cols), F32)
    b_out = jnp.zeros((k, cols), F32)
    for it in range(k):
        m = jnp.max(jnp.maximum(levels[0], top_hi), axis=0, keepdims=True)
        a_sel = jnp.min(jnp.minimum(jnp.where(levels[0] == m, alo, IDX_BIG),
                                    jnp.where(top_hi == m, ahi, IDX_BIG)), axis=0, keepdims=True)
        hit = alo == a_sel
        b_sel = jnp.sum(jnp.where(hit, depth, 0.0), axis=0, keepdims=True)
        depth = jnp.where(hit, depth + 1.0, depth)
        for b in range(k - 1):
            levels[b] = jnp.where(hit, levels[b + 1], levels[b])
        levels[k - 1] = jnp.where(hit, NEG_BIG, levels[k - 1])
        top_hi = jnp.where(ahi == a_sel, NEG_BIG, top_hi)
        tops = jnp.where(slot == it, m, tops)
        a_out = jnp.where(slot == it, a_sel, a_out)
        b_out = jnp.where(slot == it, b_sel, b_out)
        if it % POPS_PER_SLICE == POPS_PER_SLICE - 1:
            yield
    return tops, a_out, b_out


def _route_head(qp, key_a, key_b):
    k = PEER_TOPK
    tv, ti = [], []
    for c, keys in enumerate((key_a, key_b)):
        qs = qp[:, c * LANES:(c + 1) * LANES].astype(BF16)
        st = lax.dot_general(keys, qs, _NT, preferred_element_type=F32)
        v, i = yield from _topk_rows(st, k)
        tv.append(v)
        ti.append(i)
    tops, a_sel, b_sel = yield from _pair_topk(tv[0], tv[1])
    i1s = jnp.zeros_like(tops)
    i2s = jnp.zeros_like(tops)
    for a in range(k):
        i1s = i1s + jnp.where(a_sel == float(a), jnp.broadcast_to(ti[0][a:a + 1, :], tops.shape), 0.0)
        i2s = i2s + jnp.where(b_sel == float(a), jnp.broadcast_to(ti[1][a:a + 1, :], tops.shape), 0.0)
    e = jnp.exp(tops - jnp.max(tops, axis=0, keepdims=True))
    return i1s, i2s, e / jnp.sum(e, axis=0, keepdims=True)


def _route_body(h_ref, wq_ref, keys_ref, sel_ref, qp_scr):
    qp_scr[...] = jnp.dot(h_ref[...], wq_ref[...], preferred_element_type=F32)
    k = PEER_TOPK

    def head(h, carry):
        c0 = pl.multiple_of(h * PEER_QDIM, PEER_QDIM)
        parts = _run(_route_head(qp_scr[:, pl.ds(c0, PEER_QDIM)], keys_ref[2 * h], keys_ref[2 * h + 1]))
        r0 = pl.multiple_of(h * k, k)
        for a, val in enumerate(parts):
            sel_ref[a, pl.ds(r0, k), :] = val
        return carry

    lax.fori_loop(0, PEER_HEADS, head, 0, unroll=4)


def _route(h2, wq, keys):
    n = TT_PEER
    nsel = PEER_HEADS * PEER_TOPK
    return pl.pallas_call(
        _route_body,
        grid=(n // TL_ROUTE,),
        in_specs=[pl.BlockSpec((TL_ROUTE, D_MODEL), lambda i: (i, 0)),
                  pl.BlockSpec(wq.shape, lambda i: (0, 0)),
                  pl.BlockSpec(keys.shape, lambda i: (0, 0, 0))],
        out_specs=pl.BlockSpec((3, nsel, TL_ROUTE), lambda i: (0, 0, i)),
        out_shape=jax.ShapeDtypeStruct((3, nsel, n), F32),
        scratch_shapes=[pltpu.VMEM((TL_ROUTE, PEER_HEADS * PEER_QDIM), F32)],
        compiler_params=_cparams(("parallel",)),
        name="peer_route_first",
    )(h2, wq, keys)


def _peer_body(h_ref, hn_ref, wq_ref, keys_ref, u_ref, v_ref, first_ref, x1_ref, mod_ref, o_ref,
               g_scr, cur_scr, nxt_scr):
    i = pl.program_id(0)
    e = pl.program_id(1)
    n = PEER_NKEYS

    @pl.when(e == 0)
    def _():
        @pl.when(i == 0)
        def _():
            for a in range(3):
                cur_scr[a] = first_ref[a].T

        @pl.when(i > 0)
        def _():
            for a in range(3):
                cur_scr[a] = nxt_scr[a].T

        o_ref[...] = jnp.zeros(o_ref.shape, F32)
        sub = lax.broadcasted_iota(jnp.int32, (n, n), 0).astype(F32)

        def tok(t, carry):
            w = cur_scr[2, pl.ds(t, 1), :]
            w_hi = w.astype(BF16).astype(F32)
            m1 = sub == cur_scr[0, pl.ds(t, 1), :]
            x1 = jnp.concatenate([jnp.where(m1, w_hi, 0.0), jnp.where(m1, w - w_hi, 0.0)], axis=1).astype(BF16)
            x2h = jnp.where(sub == cur_scr[1, pl.ds(t, 1), :], 1.0, 0.0)
            x2 = jnp.concatenate([x2h, x2h], axis=1).astype(BF16)
            g = lax.dot_general(x1, x2, _NT, preferred_element_type=F32)
            g_scr[pl.ds(pl.multiple_of(t * G_ROW_STRIDE, 8), n), :] = g
            return carry

        lax.fori_loop(0, TT_PEER, tok, 0, unroll=64)

    def route_task():
        t0 = pl.multiple_of((e % ROUTE_SPLIT) * TL_ROUTE, TL_ROUTE)
        qp = jnp.dot(hn_ref[pl.ds(t0, TL_ROUTE), :], wq_ref[...], preferred_element_type=F32)
        parts = yield from _route_head(qp, keys_ref[0], keys_ref[1])
        r0 = pl.multiple_of((e // ROUTE_SPLIT) * PEER_TOPK, PEER_TOPK)
        for a, val in enumerate(parts):
            nxt_scr[a, pl.ds(r0, PEER_TOPK), pl.ds(t0, TL_ROUTE)] = val

    def expert_task():
        h = h_ref[...]
        parts = []
        for s in range(ET_PEER // ES_PEER):
            z = lax.dot_general(h, u_ref[s * ES_PEER:(s + 1) * ES_PEER, :], _NT,
                                preferred_element_type=F32)
            yield
            act = 0.5 * z * (1.0 + lax.erf(z * (2.0 ** -0.5)))
            for c in range(ES_PEER // n):
                i1 = e * (ET_PEER // n) + s * (ES_PEER // n) + c
                gc = g_scr[pl.ds(i1, TT_PEER, stride=G_ROW_STRIDE), :]
                parts.append((act[:, c * n:(c + 1) * n] * gc).astype(BF16))
                yield
        o_ref[...] += jnp.dot(jnp.concatenate(parts, axis=1), v_ref[...], preferred_element_type=F32)

    _interleave(expert_task(), route_task())

    @pl.when(e == pl.num_programs(1) - 1)
    def _():
        o_ref[...] = x1_ref[...] + mod_ref[0, 5:6, :] * o_ref[...]


def _peer(h2, wq, keys, u, v, first, x1, mod3):
    n_tiles = TOKENS // TT_PEER
    n_steps = PEER_EXPERTS // ET_PEER
    assert n_steps == PEER_HEADS * ROUTE_SPLIT
    tiles_per_batch = SEQ // TT_PEER
    nsel = PEER_HEADS * PEER_TOPK
    tok = lambda i, e: (i, 0)
    exp = lambda i, e: (e, 0)
    once = pl.Buffered(1)
    return pl.pallas_call(
        _peer_body,
        grid=(n_tiles, n_steps),
        in_specs=[pl.BlockSpec((TT_PEER, D_MODEL), tok, pipeline_mode=once),
                  pl.BlockSpec((TT_PEER, D_MODEL), lambda i, e: (jnp.minimum(i + 1, n_tiles - 1), 0),
                               pipeline_mode=once),
                  pl.BlockSpec((D_MODEL, PEER_QDIM), lambda i, e: (0, e // ROUTE_SPLIT)),
                  pl.BlockSpec((2, PEER_NKEYS, PEER_QDIM // 2), lambda i, e: (e // ROUTE_SPLIT, 0, 0)),
                  pl.BlockSpec((ET_PEER, D_MODEL), exp),
                  pl.BlockSpec((ET_PEER, D_MODEL), exp),
                  pl.BlockSpec((3, nsel, TT_PEER), lambda i, e: (0, 0, 0), pipeline_mode=once),
                  pl.BlockSpec((TT_PEER, D_MODEL), tok, pipeline_mode=once),
                  pl.BlockSpec((1, 6, D_MODEL), lambda i, e: (i // tiles_per_batch, 0, 0))],
        out_specs=pl.BlockSpec((TT_PEER, D_MODEL), tok),
        out_shape=jax.ShapeDtypeStruct((TOKENS, D_MODEL), F32),
        scratch_shapes=[pltpu.VMEM((TT_PEER * G_ROW_STRIDE, PEER_NKEYS), F32),
                        pltpu.VMEM((3, TT_PEER, nsel), F32),
                        pltpu.VMEM((3, nsel, TT_PEER), F32)],
        compiler_params=pltpu.CompilerParams(dimension_semantics=("arbitrary", "arbitrary"),
                                             vmem_limit_bytes=VMEM_LIMIT_PEER),
        name="peer_experts",
    )(h2, h2, wq, keys, u, v, first, x1, mod3)


def _overlap_matrix():
    start = np.arange(N_CMP)[None, :] * CMP_STRIDE
    sel = np.arange(N_SEL)[:, None] * SEL_LEN
    ov = np.clip(np.minimum(start + CMP_LEN, sel + SEL_LEN) - np.maximum(start, sel), 0, None) / CMP_LEN
    out = np.zeros((LANES, LANES), np.float32)
    out[:N_SEL, :N_CMP] = ov
    return out


def _block_expand_matrix():
    out = np.zeros((SEQ, LANES), np.float32)
    out[np.arange(SEQ), np.arange(SEQ) // SEL_LEN] = 1.0
    return out


def _cmp_weights(wk, wv, first):
    width = 2 * NSA_GROUPS * HEAD_DIM
    blocks = [w[first:first + CMP_STRIDE].astype(BF16) for w in (wk, wv) for _ in range(NSA_GROUPS)]
    zero = jnp.zeros_like(blocks[0])
    rows = [jnp.concatenate([blk if i == j else zero for j in range(len(blocks))], axis=2)
            for i, blk in enumerate(blocks)]
    return jnp.concatenate(rows, axis=1).reshape(CMP_STRIDE * width, width)


def _dup(v):
    return jnp.concatenate([v, v])[None, :]


def kernel(x, c, w_ada, b_ada, norm_g, w_in, w_out, cmp_pe_k, cmp_pe_v, w_cmp_k, w_cmp_v, qk_norm_g,
           dw_w, dw_b, conv_ln_g, conv_ln_b, peer_wq, peer_sub_keys, peer_u, peer_v):
    assert x.shape == (BATCH, SEQ, D_MODEL) and w_ada.shape[0] == DEPTH
    ovt = jnp.asarray(_overlap_matrix(), BF16)
    emat = jnp.asarray(_block_expand_matrix(), BF16)
    o_kv = NSA_WIDTH
    o_gate = o_kv + 6 * NSA_GROUPS * HEAD_DIM
    o_glu = o_gate + 3 * NSA_HEADS
    n_cmp_cols = 2 * NSA_GROUPS * HEAD_DIM
    xf = x.reshape(TOKENS, D_MODEL)
    for l in range(DEPTH):
        mod3 = _ada(c, w_ada[l], b_ada[l][None, :]).reshape(BATCH, 6, D_MODEL)
        wi = w_in[l]
        gate_pad = jnp.zeros((D_MODEL, LANES - 3 * GQA_REP), F32)
        wg = jnp.concatenate(
            [part for g in range(NSA_GROUPS)
             for part in (wi[:, o_gate + 3 * GQA_REP * g:o_gate + 3 * GQA_REP * (g + 1)], gate_pad)], axis=1)
        q, kvc, kvs, gl, glu = _inproj(
            xf, mod3, norm_g[l, 0][None, :],
            wi[:, :o_kv].astype(BF16), wi[:, o_kv:o_kv + n_cmp_cols].astype(BF16),
            wi[:, o_kv + n_cmp_cols:o_gate].astype(BF16), wg.astype(BF16), wi[:, o_glu:].astype(BF16))
        pe = jnp.concatenate([cmp_pe_k[l], cmp_pe_k[l], cmp_pe_v[l], cmp_pe_v[l]], axis=1)
        kv_parts = _kvprep(
            kvc.reshape(BATCH, SEQ // CMP_STRIDE, CMP_STRIDE * n_cmp_cols), kvs,
            _cmp_weights(w_cmp_k[l], w_cmp_v[l], 0), _cmp_weights(w_cmp_k[l], w_cmp_v[l], CMP_STRIDE),
            pe[:CMP_STRIDE].reshape(1, -1), pe[CMP_STRIDE:].reshape(1, -1),
            _dup(qk_norm_g[l, 1]), _dup(qk_norm_g[l, 2]), _dup(qk_norm_g[l, 3]))
        a_out = _nsa(q, gl, *kv_parts, _dup(qk_norm_g[l, 0]), ovt, emat)
        c_out = _conv(glu, dw_w[l], dw_b[l][None, :], conv_ln_g[l][None, :], conv_ln_b[l][None, :])
        x1, h2 = _outproj(a_out, c_out, xf, mod3, norm_g[l, 1][None, :],
                          w_out[l, :NSA_WIDTH].astype(BF16), w_out[l, NSA_WIDTH:].astype(BF16))
        wq = peer_wq[l].astype(BF16)
        keys = peer_sub_keys[l].reshape(2 * PEER_HEADS, PEER_NKEYS, PEER_QDIM // 2).astype(BF16)
        xf = _peer(h2, wq, keys, peer_u[l].astype(BF16), peer_v[l].astype(BF16), _route(h2, wq, keys), x1, mod3)
    return xf.reshape(BATCH, SEQ, D_MODEL)
```

```python
import functools

import numpy as np
import jax
import jax.numpy as jnp
from jax import lax
from jax.experimental import pallas as pl
from jax.experimental.pallas import tpu as pltpu

F32 = jnp.float32
BF16 = jnp.bfloat16

D_MODEL = 1024
BATCH = 8
SEQ = 2048
DEPTH = 1
TOKENS = BATCH * SEQ

HEAD_DIM = 64
NSA_HEADS = 8
NSA_GROUPS = 2
GQA_REP = NSA_HEADS // NSA_GROUPS
NSA_WIDTH = NSA_HEADS * HEAD_DIM
CONV_WIDTH = 512
CMP_LEN = 32
CMP_STRIDE = 16
N_CMP = (SEQ - CMP_LEN) // CMP_STRIDE + 1
SEL_LEN = 64
N_SEL = SEQ // SEL_LEN
SEL_TOPN = 16
WINDOW = 512
FORCE_BONUS = 1.0e4
CONV_TAPS = 31
PEER_HEADS = 8
PEER_NKEYS = 128
PEER_EXPERTS = PEER_NKEYS * PEER_NKEYS
PEER_QDIM = 256
PEER_TOPK = 16
NORM_EPS = 1e-6
NEG_INF = -1e30
NEG_BIG = -3.0e38
IDX_BIG = 1.0e9

LANES = 128
VMEM_V7X = 64 * 1024 * 1024
VMEM_LIMIT = VMEM_V7X * 3 // 4
VMEM_LIMIT_PEER = VMEM_V7X * 15 // 16

TM_PROJ = 512
TQ = 256
TK = 256
WIN_KEYS = WINDOW + TQ
V_ROWS = HEAD_DIM + 16
TS_CONV = 64
CONV_PAD = 32
TL_ROUTE = 256
TT_PEER = 512
ET_PEER = 1024
ES_PEER = 256
ROUTE_SPLIT = TT_PEER // TL_ROUTE
G_ROW_STRIDE = PEER_NKEYS + 8

_NT = (((1,), (1,)), ((), ()))


def _cparams(sem):
    return pltpu.CompilerParams(dimension_semantics=sem, vmem_limit_bytes=VMEM_LIMIT)


def _ada_body(c_ref, w_ref, b_ref, o_ref):
    c = c_ref[...]
    sc = (c * jax.nn.sigmoid(c)).astype(BF16)
    o_ref[...] = jnp.dot(sc, w_ref[...].astype(BF16), preferred_element_type=F32) + b_ref[...]


def _ada(c, w, b):
    n = w.shape[1]
    tn = 1536
    return pl.pallas_call(
        _ada_body,
        grid=(n // tn,),
        in_specs=[pl.BlockSpec((BATCH, D_MODEL), lambda j: (0, 0)),
                  pl.BlockSpec((D_MODEL, tn), lambda j: (0, j)),
                  pl.BlockSpec((1, tn), lambda j: (0, j))],
        out_specs=pl.BlockSpec((BATCH, tn), lambda j: (0, j)),
        out_shape=jax.ShapeDtypeStruct((BATCH, n), F32),
        compiler_params=_cparams(("arbitrary",)),
        name="ada_mod",
    )(c, w, b)


def _norm_mod(x, g, shift, scale):
    ms = jnp.mean(x * x, axis=-1, keepdims=True)
    y = x * lax.rsqrt(ms + NORM_EPS) * g
    return y * (1.0 + scale) + shift


def _inproj_body(x_ref, mod_ref, g_ref, wq_ref, wkc_ref, wks_ref, wg_ref, wglu_ref,
                 q_ref, kc_ref, ks_ref, gl_ref, glu_ref):
    h = _norm_mod(x_ref[...], g_ref[...], mod_ref[0, 0:1, :], mod_ref[0, 1:2, :]).astype(BF16)
    for w_ref, o_ref in ((wq_ref, q_ref), (wkc_ref, kc_ref), (wks_ref, ks_ref),
                         (wg_ref, gl_ref), (wglu_ref, glu_ref)):
        o_ref[...] = jnp.dot(h, w_ref[...], preferred_element_type=F32)


def _inproj(xf, mod3, g, wq, wkc, wks, wg, wglu):
    tiles_per_batch = SEQ // TM_PROJ
    ws = (wq, wkc, wks, wg, wglu)
    row = lambda i: (i, 0)
    return pl.pallas_call(
        _inproj_body,
        grid=(TOKENS // TM_PROJ,),
        in_specs=[pl.BlockSpec((TM_PROJ, D_MODEL), row),
                  pl.BlockSpec((1, 6, D_MODEL), lambda i: (i // tiles_per_batch, 0, 0)),
                  pl.BlockSpec((1, D_MODEL), lambda i: (0, 0))]
                 + [pl.BlockSpec(w.shape, lambda i: (0, 0)) for w in ws],
        out_specs=[pl.BlockSpec((TM_PROJ, w.shape[1]), row) for w in ws],
        out_shape=[jax.ShapeDtypeStruct((TOKENS, w.shape[1]), F32) for w in ws],
        compiler_params=_cparams(("parallel",)),
        name="in_proj",
    )(xf, mod3, g, *ws)


def _rms_pair(x, gdup, lo):
    x2 = x * x
    s_lo = jnp.sum(jnp.where(lo, x2, 0.0), axis=-1, keepdims=True)
    s_hi = jnp.sum(jnp.where(lo, 0.0, x2), axis=-1, keepdims=True)
    rs = jnp.where(lo, lax.rsqrt(s_lo * (1.0 / HEAD_DIM) + NORM_EPS),
                   lax.rsqrt(s_hi * (1.0 / HEAD_DIM) + NORM_EPS))
    return x * rs * gdup


def _key_ext(kn, lo, lane, pos):
    ext = jnp.where(lane == HEAD_DIM, (pos >> 6).astype(F32),
                    jnp.where(lane == HEAD_DIM + 1, (pos & (SEL_LEN - 1)).astype(F32),
                              jnp.where(lane == HEAD_DIM + 2, 1.0, 0.0)))
    return jnp.where(lo, kn, ext), jnp.where(lo, pltpu.roll(kn, HEAD_DIM, 1), ext)


def _kvprep_body(r_ref, kvs_ref, wa_ref, wb_ref, pea_ref, peb_ref, g1_ref, g2_ref, g3_ref,
                 kce_ref, vct_ref, kse_ref, vst_ref, kwe_ref, vwt_ref):
    lane = lax.broadcasted_iota(jnp.int32, (1, LANES), 1)
    lo = lane < HEAD_DIM
    nrow = SEQ // CMP_STRIDE
    r = r_ref[0]
    a = jnp.dot((r + pea_ref[...]).astype(BF16), wa_ref[...], preferred_element_type=F32)
    b = jnp.dot((r + peb_ref[...]).astype(BF16), wb_ref[...], preferred_element_type=F32)
    c = a + pltpu.roll(b, nrow - 1, 0)
    end = lax.broadcasted_iota(jnp.int32, (nrow, 1), 0) * CMP_STRIDE + (CMP_LEN - 1)
    kc0, kc1 = _key_ext(_rms_pair(c[:, :LANES], g1_ref[...], lo), lo, lane, end)
    kce_ref[0, 0] = kc0.astype(BF16)
    kce_ref[0, 1] = kc1.astype(BF16)
    vct = c[:, LANES:].T.astype(BF16)
    vct_ref[0, 0] = vct[:HEAD_DIM]
    vct_ref[0, 1] = vct[HEAD_DIM:]

    rows = 256
    ones = jnp.ones((V_ROWS - HEAD_DIM, rows), F32)

    def chunk(i, carry):
        r0 = pl.multiple_of(i * rows, rows)
        blk = kvs_ref[pl.ds(r0, rows), :]
        pos = r0 + lax.broadcasted_iota(jnp.int32, (rows, 1), 0)
        for k_ref, v_ref, gain, off in ((kse_ref, vst_ref, g2_ref, 0), (kwe_ref, vwt_ref, g3_ref, 2 * LANES)):
            k0, k1 = _key_ext(_rms_pair(blk[:, off:off + LANES], gain[...], lo), lo, lane, pos)
            k_ref[0, 0, pl.ds(r0, rows), :] = k0.astype(BF16)
            k_ref[0, 1, pl.ds(r0, rows), :] = k1.astype(BF16)
            vt = blk[:, off + LANES:off + 2 * LANES].T
            for g in range(NSA_GROUPS):
                v_ref[0, g, :, pl.ds(r0, rows)] = jnp.concatenate(
                    [vt[g * HEAD_DIM:(g + 1) * HEAD_DIM], ones], axis=0).astype(BF16)
        return carry

    lax.fori_loop(0, SEQ // rows, chunk, 0)


def _kvprep(rmat, kvs, wa, wb, pea, peb, g1, g2, g3):
    nrow = SEQ // CMP_STRIDE
    const2 = lambda b: (0, 0)
    per_b = lambda b: (b, 0, 0, 0)
    shapes = [(nrow, LANES), (HEAD_DIM, nrow), (SEQ, LANES), (V_ROWS, SEQ), (SEQ, LANES), (V_ROWS, SEQ)]
    return pl.pallas_call(
        _kvprep_body,
        grid=(BATCH,),
        in_specs=[pl.BlockSpec((1, nrow, rmat.shape[2]), lambda b: (b, 0, 0)),
                  pl.BlockSpec((SEQ, kvs.shape[1]), lambda b: (b, 0)),
                  pl.BlockSpec(wa.shape, const2), pl.BlockSpec(wb.shape, const2),
                  pl.BlockSpec(pea.shape, const2), pl.BlockSpec(peb.shape, const2),
                  pl.BlockSpec((1, LANES), const2), pl.BlockSpec((1, LANES), const2),
                  pl.BlockSpec((1, LANES), const2)],
        out_specs=[pl.BlockSpec((1, NSA_GROUPS) + s, per_b) for s in shapes],
        out_shape=[jax.ShapeDtypeStruct((BATCH, NSA_GROUPS) + s, BF16) for s in shapes],
        compiler_params=_cparams(("parallel",)),
        name="kv_prep",
    )(rmat, kvs, wa, wb, pea, peb, g1, g2, g3)


def _loop(trips, body, init):
    if isinstance(trips, int):
        for i in range(trips):
            init = body(i, init)
        return init
    return lax.fori_loop(0, trips, body, init)


def _split3(x):
    p1 = x.astype(BF16)
    r1 = x - p1.astype(F32)
    p2 = r1.astype(BF16)
    p3 = (r1 - p2.astype(F32)).astype(BF16)
    return p1, p2, p3


def _nsa_body(q_ref, gl_ref, kce_ref, vct_ref, kse_ref, vst_ref, kwe_ref, vwt_ref,
              g0_ref, ovt_ref, et_ref, o_ref, s_scr):
    qi = pl.program_id(1)
    q0 = qi * TQ
    lane = lax.broadcasted_iota(jnp.int32, (1, LANES), 1)
    lo = lane < HEAD_DIM
    t_idx = q0 + lax.broadcasted_iota(jnp.int32, (1, TQ), 1)
    q0f = q0.astype(F32)

    n_sub = lax.broadcasted_iota(jnp.int32, (SEQ // CMP_STRIDE, 1), 0)
    cmask = (t_idx >= n_sub * CMP_STRIDE + (CMP_LEN - 1)) & (n_sub < N_CMP)
    j = lax.broadcasted_iota(jnp.int32, (N_SEL, 1), 0)
    tb = t_idx >> 6
    bonus = jnp.where((j == 0) | (j == tb) | (j == tb - 1), FORCE_BONUS, 0.0)

    qe, o_cmp, sels = [], [], []
    for g in range(NSA_GROUPS):
        for p in range(GQA_REP // 2):
            c0 = (g * GQA_REP // 2 + p) * LANES
            qn = _rms_pair(q_ref[:, c0:c0 + LANES], g0_ref[...], lo) * (HEAD_DIM ** -0.5)
            for half, base in ((0, qn), (1, pltpu.roll(qn, HEAD_DIM, 1))):
                slope = 2.0 ** -(g * GQA_REP + 2 * p + half + 1)
                ext = jnp.where(lane == HEAD_DIM, SEL_LEN * slope,
                                jnp.where(lane == HEAD_DIM + 1, slope,
                                          jnp.where(lane == HEAD_DIM + 2, -slope * q0f, 0.0)))
                qe.append(jnp.where(lo, base, ext).astype(BF16))
        heads = range(g * GQA_REP, (g + 1) * GQA_REP)

        kce = kce_ref[0, g]
        vct = vct_ref[0, g]
        psum = jnp.zeros((SEQ // CMP_STRIDE, TQ), F32)
        for h in heads:
            s = jnp.where(cmask, lax.dot_general(kce, qe[h], _NT, preferred_element_type=F32), NEG_INF)
            e = jnp.where(cmask, jnp.exp(s - jnp.max(s, axis=0, keepdims=True)), 0.0)
            l = jnp.sum(e, axis=0, keepdims=True)
            p = e / jnp.where(l > 0.0, l, 1.0)
            psum = psum + p
            o_cmp.append(jnp.dot(vct, p.astype(BF16), preferred_element_type=F32))

        imp = jnp.zeros((LANES, TQ), F32)
        for part in _split3(psum):
            imp = imp + jnp.dot(ovt_ref[...], part, preferred_element_type=F32)
        imp = jnp.where(j <= tb, imp[:N_SEL] + bonus, -1.0)
        rank = jnp.zeros((N_SEL, TQ), F32)
        for i in range(N_SEL):
            ci = imp[i:i + 1, :]
            ahead = (ci > imp) | ((ci == imp) & (j > i))
            rank = rank + jnp.where(ahead, 1.0, 0.0)
        sels.append(jnp.concatenate([jnp.where(rank < float(SEL_TOPN), 1.0, 0.0),
                                     jnp.zeros((LANES - N_SEL, TQ), F32)], axis=0).astype(BF16))

    key_sub = lax.broadcasted_iota(jnp.int32, (TK, 1), 0)
    neg_row = tuple(jnp.full((1, TQ), NEG_INF, F32) for _ in range(NSA_HEADS))
    zero_acc = tuple(jnp.zeros((V_ROWS, TQ), F32) for _ in range(NSA_HEADS))

    def two_pass(n_tiles, first_key, k_ref, v_ref, bias_fn, per_trip=1):
        def score_step(i, ms):
            ms = list(ms)
            for sub in range(per_trip):
                k0 = pl.multiple_of(first_key + (i * per_trip + sub) * TK, LANES)
                r0 = pl.multiple_of((i * per_trip + sub) * TK, TK)
                for g in range(NSA_GROUPS):
                    kblk = k_ref[0, g, pl.ds(k0, TK), :]
                    bias = bias_fn(g, k0)
                    for h in range(g * GQA_REP, (g + 1) * GQA_REP):
                        s = lax.dot_general(kblk, qe[h], _NT, preferred_element_type=F32) + bias
                        s_scr[h, pl.ds(r0, TK), :] = s
                        ms[h] = jnp.maximum(ms[h], jnp.max(s, axis=0, keepdims=True))
            return tuple(ms)

        ms = _loop(n_tiles, score_step, neg_row)

        def value_step(i, accs):
            accs = list(accs)
            for sub in range(per_trip):
                k0 = pl.multiple_of(first_key + (i * per_trip + sub) * TK, LANES)
                r0 = pl.multiple_of((i * per_trip + sub) * TK, TK)
                for g in range(NSA_GROUPS):
                    vt = v_ref[0, g, :, pl.ds(k0, TK)]
                    for h in range(g * GQA_REP, (g + 1) * GQA_REP):
                        e = jnp.exp(s_scr[h, pl.ds(r0, TK), :] - ms[h])
                        accs[h] = accs[h] + jnp.dot(vt, e.astype(BF16), preferred_element_type=F32)
            return tuple(accs)

        accs = _loop(n_tiles, value_step, zero_acc)
        return [a[:HEAD_DIM] / a[HEAD_DIM:HEAD_DIM + 1] for a in accs]

    start = pl.multiple_of(jnp.maximum(q0 - WINDOW, 0), LANES)

    def window_bias(g, k0):
        wd = t_idx - (k0 + key_sub)
        return jnp.where((wd >= 0) & (wd < WINDOW), 0.0, NEG_INF)

    o_win = two_pass(WIN_KEYS // TK, start, kwe_ref, vwt_ref, window_bias)

    def selected_bias(g, k0):
        chosen = jnp.dot(et_ref[pl.ds(k0, TK), :], sels[g], preferred_element_type=F32) > 0.5
        return jnp.where(chosen & (k0 + key_sub <= t_idx), 0.0, NEG_INF)

    o_slc = two_pass(((q0 + TQ - 1) // TK + 2) // 2, 0, kse_ref, vst_ref, selected_bias, per_trip=2)

    for g in range(NSA_GROUPS):
        sg = jax.nn.sigmoid(gl_ref[:, g * LANES:(g + 1) * LANES].T[:4 * GQA_REP])
        for p in range(GQA_REP // 2):
            pair = []
            for r in (2 * p, 2 * p + 1):
                h = g * GQA_REP + r
                pair.append(sg[3 * r:3 * r + 1] * o_cmp[h] + sg[3 * r + 1:3 * r + 2] * o_slc[h]
                            + sg[3 * r + 2:3 * r + 3] * o_win[h])
            c0 = (g * GQA_REP // 2 + p) * LANES
            o_ref[:, c0:c0 + LANES] = jnp.concatenate(pair, axis=0).T.astype(o_ref.dtype)


def _nsa(q, gl, kce, vct, kse, vst, kwe, vwt, g0, ovt, et):
    nq = SEQ // TQ
    tile = lambda b, i: (b * nq + i, 0)
    per_b = lambda b, i: (b, 0, 0, 0)
    const2 = lambda b, i: (0, 0)
    return pl.pallas_call(
        _nsa_body,
        grid=(BATCH, nq),
        in_specs=[pl.BlockSpec((TQ, NSA_WIDTH), tile), pl.BlockSpec((TQ, NSA_GROUPS * LANES), tile)]
                 + [pl.BlockSpec((1,) + a.shape[1:], per_b) for a in (kce, vct, kse, vst, kwe, vwt)]
                 + [pl.BlockSpec((1, LANES), const2), pl.BlockSpec(ovt.shape, const2),
                    pl.BlockSpec(et.shape, const2)],
        out_specs=pl.BlockSpec((TQ, NSA_WIDTH), tile),
        out_shape=jax.ShapeDtypeStruct((TOKENS, NSA_WIDTH), BF16),
        scratch_shapes=[pltpu.VMEM((NSA_HEADS, SEQ, TQ), F32)],
        compiler_params=_cparams(("parallel", "arbitrary")),
        name="nsa_attention",
    )(q, gl, kce, vct, kse, vst, kwe, vwt, g0, ovt, et)


def _conv_body(glu_ref, w_ref, b_ref, lg_ref, lb_ref, o_ref, u_scr, sh_scr):
    u_scr[0:CONV_PAD, :] = jnp.zeros((CONV_PAD, CONV_WIDTH), F32)
    rows = 256

    def fill(i, carry):
        r0 = pl.multiple_of(i * rows, rows)
        blk = glu_ref[pl.ds(r0, rows), :]
        u_scr[pl.ds(CONV_PAD + r0, rows), :] = blk[:, :CONV_WIDTH] * jax.nn.sigmoid(blk[:, CONV_WIDTH:])
        return carry

    lax.fori_loop(0, SEQ // rows, fill, 0)
    first = CONV_PAD - (CONV_TAPS - 1)

    def tile(i, carry):
        r0 = pl.multiple_of(i * TS_CONV, TS_CONV)
        win = u_scr[pl.ds(r0, TS_CONV + CONV_PAD), :]
        span = TS_CONV + CONV_PAD - 8
        for s in range(1, 8):
            sh_scr[s - 1] = win[s:s + span, :]
        acc = jnp.zeros((TS_CONV, CONV_WIDTH), F32) + b_ref[...]
        for k in range(CONV_TAPS):
            s = (first + k) % 8
            base = first + k - s
            tap = win[base:base + TS_CONV, :] if s == 0 else sh_scr[s - 1, base:base + TS_CONV, :]
            acc = acc + tap * w_ref[k:k + 1, :]
        mu = jnp.mean(acc, axis=-1, keepdims=True)
        d = acc - mu
        var = jnp.mean(d * d, axis=-1, keepdims=True)
        yn = d * lax.rsqrt(var + NORM_EPS) * lg_ref[...] + lb_ref[...]
        o_ref[pl.ds(r0, TS_CONV), :] = (yn * jax.nn.sigmoid(yn)).astype(o_ref.dtype)
        return carry

    lax.fori_loop(0, SEQ // TS_CONV, tile, 0)


def _conv(glu, w, b, lg, lb):
    const2 = lambda i: (0, 0)
    return pl.pallas_call(
        _conv_body,
        grid=(BATCH,),
        in_specs=[pl.BlockSpec((SEQ, 2 * CONV_WIDTH), lambda i: (i, 0)),
                  pl.BlockSpec(w.shape, const2), pl.BlockSpec(b.shape, const2),
                  pl.BlockSpec(lg.shape, const2), pl.BlockSpec(lb.shape, const2)],
        out_specs=pl.BlockSpec((SEQ, CONV_WIDTH), lambda i: (i, 0)),
        out_shape=jax.ShapeDtypeStruct((TOKENS, CONV_WIDTH), BF16),
        scratch_shapes=[pltpu.VMEM((CONV_PAD + SEQ, CONV_WIDTH), F32),
                        pltpu.VMEM((7, TS_CONV + CONV_PAD - 8, CONV_WIDTH), F32)],
        compiler_params=_cparams(("parallel",)),
        name="conv_mixer",
    )(glu, w, b, lg, lb)


def _outproj_body(a_ref, c_ref, x_ref, mod_ref, g_ref, wa_ref, wc_ref, x1_ref, h2_ref):
    mix = (jnp.dot(a_ref[...], wa_ref[...], preferred_element_type=F32)
           + jnp.dot(c_ref[...], wc_ref[...], preferred_element_type=F32))
    x1 = x_ref[...] + mod_ref[0, 2:3, :] * mix
    x1_ref[...] = x1
    h2_ref[...] = _norm_mod(x1, g_ref[...], mod_ref[0, 3:4, :], mod_ref[0, 4:5, :]).astype(BF16)


def _outproj(a, c, xf, mod3, g, wa, wc):
    tiles_per_batch = SEQ // TM_PROJ
    row = lambda i: (i, 0)
    const2 = lambda i: (0, 0)
    return pl.pallas_call(
        _outproj_body,
        grid=(TOKENS // TM_PROJ,),
        in_specs=[pl.BlockSpec((TM_PROJ, NSA_WIDTH), row),
                  pl.BlockSpec((TM_PROJ, CONV_WIDTH), row),
                  pl.BlockSpec((TM_PROJ, D_MODEL), row),
                  pl.BlockSpec((1, 6, D_MODEL), lambda i: (i // tiles_per_batch, 0, 0)),
                  pl.BlockSpec((1, D_MODEL), const2),
                  pl.BlockSpec(wa.shape, const2), pl.BlockSpec(wc.shape, const2)],
        out_specs=[pl.BlockSpec((TM_PROJ, D_MODEL), row), pl.BlockSpec((TM_PROJ, D_MODEL), row)],
        out_shape=[jax.ShapeDtypeStruct((TOKENS, D_MODEL), F32),
                   jax.ShapeDtypeStruct((TOKENS, D_MODEL), BF16)],
        compiler_params=_cparams(("parallel",)),
        name="out_proj",
    )(a, c, xf, mod3, g, wa, wc)


_SORT4 = ((0, 1), (2, 3), (0, 2), (1, 3), (1, 2))
POPS_PER_SLICE = 4


def _run(gen):
    try:
        while True:
            next(gen)
    except StopIteration as stop:
        return stop.value


def _interleave(*gens):
    live = list(gens)
    while live:
        for gen in list(live):
            try:
                next(gen)
            except StopIteration:
                live.remove(gen)


def _topk_rows(x, k):
    n, cols = x.shape
    q = n // 4
    row = lax.broadcasted_iota(jnp.int32, (q, cols), 0).astype(F32)
    vals = [x[i * q:(i + 1) * q] for i in range(4)]
    idxs = [row + float(i * q) for i in range(4)]
    for i, j in _SORT4:
        swap = (vals[j] > vals[i]) | ((vals[j] == vals[i]) & (idxs[j] < idxs[i]))
        vals[i], vals[j] = jnp.where(swap, vals[j], vals[i]), jnp.where(swap, vals[i], vals[j])
        idxs[i], idxs[j] = jnp.where(swap, idxs[j], idxs[i]), jnp.where(swap, idxs[i], idxs[j])
    yield
    slot = lax.broadcasted_iota(jnp.int32, (k, cols), 0)
    out_v = jnp.zeros((k, cols), F32)
    out_i = jnp.zeros((k, cols), F32)
    for it in range(k):
        m = jnp.max(vals[0], axis=0, keepdims=True)
        idx = jnp.min(jnp.where(vals[0] == m, idxs[0], IDX_BIG), axis=0, keepdims=True)
        hit = idxs[0] == idx
        for lvl in range(3):
            vals[lvl] = jnp.where(hit, vals[lvl + 1], vals[lvl])
            idxs[lvl] = jnp.where(hit, idxs[lvl + 1], idxs[lvl])
        vals[3] = jnp.where(hit, NEG_BIG, vals[3])
        out_v = jnp.where(slot == it, m, out_v)
        out_i = jnp.where(slot == it, idx, out_i)
        if it % POPS_PER_SLICE == POPS_PER_SLICE - 1:
            yield
    return out_v, out_i


def _pair_topk(v1, v2):
    k, cols = v1.shape
    half = k // 2
    alo = lax.broadcasted_iota(jnp.int32, (half, cols), 0).astype(F32)
    ahi = alo + float(half)
    levels = []
    for b in range(k):
        lvl = v1[:half] + jnp.broadcast_to(v2[b:b + 1, :], (half, cols))
        if k // (b + 1) < half:
            lvl = jnp.where(alo < float(k // (b + 1)), lvl, NEG_BIG)
        levels.append(lvl)
    top_hi = v1[half:] + jnp.broadcast_to(v2[0:1, :], (half, cols))
    depth = jnp.zeros((half, cols), F32)
    slot = lax.broadcasted_iota(jnp.int32, (k, cols), 0)
    tops = jnp.zeros((k, cols), F32)
    a_out = jnp.zeros((k, cols), F32)
    b_out = jnp.zeros((k, cols), F32)
    for it in range(k):
        m = jnp.max(jnp.maximum(levels[0], top_hi), axis=0, keepdims=True)
        a_sel = jnp.min(jnp.minimum(jnp.where(levels[0] == m, alo, IDX_BIG),
                                    jnp.where(top_hi == m, ahi, IDX_BIG)), axis=0, keepdims=True)
        hit = alo == a_sel
        b_sel = jnp.sum(jnp.where(hit, depth, 0.0), axis=0, keepdims=True)
        depth = jnp.where(hit, depth + 1.0, depth)
        for b in range(k - 1):
            levels[b] = jnp.where(hit, levels[b + 1], levels[b])
        levels[k - 1] = jnp.where(hit, NEG_BIG, levels[k - 1])
        top_hi = jnp.where(ahi == a_sel, NEG_BIG, top_hi)
        tops = jnp.where(slot == it, m, tops)
        a_out = jnp.where(slot == it, a_sel, a_out)
        b_out = jnp.where(slot == it, b_sel, b_out)
        if it % POPS_PER_SLICE == POPS_PER_SLICE - 1:
            yield
    return tops, a_out, b_out


def _route_head(qp, key_a, key_b):
    k = PEER_TOPK
    tv, ti = [], []
    for c, keys in enumerate((key_a, key_b)):
        qs = qp[:, c * LANES:(c + 1) * LANES].astype(BF16)
        st = lax.dot_general(keys, qs, _NT, preferred_element_type=F32)
        v, i = yield from _topk_rows(st, k)
        tv.append(v)
        ti.append(i)
    tops, a_sel, b_sel = yield from _pair_topk(tv[0], tv[1])
    i1s = jnp.zeros_like(tops)
    i2s = jnp.zeros_like(tops)
    for a in range(k):
        i1s = i1s + jnp.where(a_sel == float(a), jnp.broadcast_to(ti[0][a:a + 1, :], tops.shape), 0.0)
        i2s = i2s + jnp.where(b_sel == float(a), jnp.broadcast_to(ti[1][a:a + 1, :], tops.shape), 0.0)
    e = jnp.exp(tops - jnp.max(tops, axis=0, keepdims=True))
    return i1s, i2s, e / jnp.sum(e, axis=0, keepdims=True)


def _route_body(h_ref, wq_ref, keys_ref, sel_ref, qp_scr):
    qp_scr[...] = jnp.dot(h_ref[...], wq_ref[...], preferred_element_type=F32)
    k = PEER_TOPK

    def head(h, carry):
        c0 = pl.multiple_of(h * PEER_QDIM, PEER_QDIM)
        parts = _run(_route_head(qp_scr[:, pl.ds(c0, PEER_QDIM)], keys_ref[2 * h], keys_ref[2 * h + 1]))
        r0 = pl.multiple_of(h * k, k)
        for a, val in enumerate(parts):
            sel_ref[a, pl.ds(r0, k), :] = val
        return carry

    lax.fori_loop(0, PEER_HEADS, head, 0, unroll=4)


def _route(h2, wq, keys):
    n = TT_PEER
    nsel = PEER_HEADS * PEER_TOPK
    return pl.pallas_call(
        _route_body,
        grid=(n // TL_ROUTE,),
        in_specs=[pl.BlockSpec((TL_ROUTE, D_MODEL), lambda i: (i, 0)),
                  pl.BlockSpec(wq.shape, lambda i: (0, 0)),
                  pl.BlockSpec(keys.shape, lambda i: (0, 0, 0))],
        out_specs=pl.BlockSpec((3, nsel, TL_ROUTE), lambda i: (0, 0, i)),
        out_shape=jax.ShapeDtypeStruct((3, nsel, n), F32),
        scratch_shapes=[pltpu.VMEM((TL_ROUTE, PEER_HEADS * PEER_QDIM), F32)],
        compiler_params=_cparams(("parallel",)),
        name="peer_route_first",
    )(h2, wq, keys)


def _peer_body(h_ref, hn_ref, wq_ref, keys_ref, u_ref, v_ref, first_ref, x1_ref, mod_ref, o_ref,
               g_scr, cur_scr, nxt_scr):
    i = pl.program_id(0)
    e = pl.program_id(1)
    n = PEER_NKEYS

    @pl.when(e == 0)
    def _():
        @pl.when(i == 0)
        def _():
            for a in range(3):
                cur_scr[a] = first_ref[a].T

        @pl.when(i > 0)
        def _():
            for a in range(3):
                cur_scr[a] = nxt_scr[a].T

        o_ref[...] = jnp.zeros(o_ref.shape, F32)
        sub = lax.broadcasted_iota(jnp.int32, (n, n), 0).astype(F32)

        def tok(t, carry):
            w = cur_scr[2, pl.ds(t, 1), :]
            w_hi = w.astype(BF16).astype(F32)
            m1 = sub == cur_scr[0, pl.ds(t, 1), :]
            x1 = jnp.concatenate([jnp.where(m1, w_hi, 0.0), jnp.where(m1, w - w_hi, 0.0)], axis=1).astype(BF16)
            x2h = jnp.where(sub == cur_scr[1, pl.ds(t, 1), :], 1.0, 0.0)
            x2 = jnp.concatenate([x2h, x2h], axis=1).astype(BF16)
            g = lax.dot_general(x1, x2, _NT, preferred_element_type=F32)
            g_scr[pl.ds(pl.multiple_of(t * G_ROW_STRIDE, 8), n), :] = g
            return carry

        lax.fori_loop(0, TT_PEER, tok, 0, unroll=64)

    def route_task():
        t0 = pl.multiple_of((e % ROUTE_SPLIT) * TL_ROUTE, TL_ROUTE)
        qp = jnp.dot(hn_ref[pl.ds(t0, TL_ROUTE), :], wq_ref[...], preferred_element_type=F32)
        parts = yield from _route_head(qp, keys_ref[0], keys_ref[1])
        r0 = pl.multiple_of((e // ROUTE_SPLIT) * PEER_TOPK, PEER_TOPK)
        for a, val in enumerate(parts):
            nxt_scr[a, pl.ds(r0, PEER_TOPK), pl.ds(t0, TL_ROUTE)] = val

    def expert_task():
        h = h_ref[...]
        parts = []
        for s in range(ET_PEER // ES_PEER):
            z = lax.dot_general(h, u_ref[s * ES_PEER:(s + 1) * ES_PEER, :], _NT,
                                preferred_element_type=F32)
            yield
            act = 0.5 * z * (1.0 + lax.erf(z * (2.0 ** -0.5)))
            for c in range(ES_PEER // n):
                i1 = e * (ET_PEER // n) + s * (ES_PEER // n) + c
                gc = g_scr[pl.ds(i1, TT_PEER, stride=G_ROW_STRIDE), :]
                parts.append((act[:, c * n:(c + 1) * n] * gc).astype(BF16))
                yield
        o_ref[...] += jnp.dot(jnp.concatenate(parts, axis=1), v_ref[...], preferred_element_type=F32)

    _interleave(expert_task(), route_task())

    @pl.when(e == pl.num_programs(1) - 1)
    def _():
        o_ref[...] = x1_ref[...] + mod_ref[0, 5:6, :] * o_ref[...]


def _peer(h2, wq, keys, u, v, first, x1, mod3):
    n_tiles = TOKENS // TT_PEER
    n_steps = PEER_EXPERTS // ET_PEER
    assert n_steps == PEER_HEADS * ROUTE_SPLIT
    tiles_per_batch = SEQ // TT_PEER
    nsel = PEER_HEADS * PEER_TOPK
    tok = lambda i, e: (i, 0)
    exp = lambda i, e: (e, 0)
    once = pl.Buffered(1)
    return pl.pallas_call(
        _peer_body,
        grid=(n_tiles, n_steps),
        in_specs=[pl.BlockSpec((TT_PEER, D_MODEL), tok, pipeline_mode=once),
                  pl.BlockSpec((TT_PEER, D_MODEL), lambda i, e: (jnp.minimum(i + 1, n_tiles - 1), 0),
                               pipeline_mode=once),
                  pl.BlockSpec((D_MODEL, PEER_QDIM), lambda i, e: (0, e // ROUTE_SPLIT)),
                  pl.BlockSpec((2, PEER_NKEYS, PEER_QDIM // 2), lambda i, e: (e // ROUTE_SPLIT, 0, 0)),
                  pl.BlockSpec((ET_PEER, D_MODEL), exp),
                  pl.BlockSpec((ET_PEER, D_MODEL), exp),
                  pl.BlockSpec((3, nsel, TT_PEER), lambda i, e: (0, 0, 0), pipeline_mode=once),
                  pl.BlockSpec((TT_PEER, D_MODEL), tok, pipeline_mode=once),
                  pl.BlockSpec((1, 6, D_MODEL), lambda i, e: (i // tiles_per_batch, 0, 0))],
        out_specs=pl.BlockSpec((TT_PEER, D_MODEL), tok),
        out_shape=jax.ShapeDtypeStruct((TOKENS, D_MODEL), F32),
        scratch_shapes=[pltpu.VMEM((TT_PEER * G_ROW_STRIDE, PEER_NKEYS), F32),
                        pltpu.VMEM((3, TT_PEER, nsel), F32),
                        pltpu.VMEM((3, nsel, TT_PEER), F32)],
        compiler_params=pltpu.CompilerParams(dimension_semantics=("arbitrary", "arbitrary"),
                                             vmem_limit_bytes=VMEM_LIMIT_PEER),
        name="peer_experts",
    )(h2, h2, wq, keys, u, v, first, x1, mod3)


def _overlap_matrix():
    start = np.arange(N_CMP)[None, :] * CMP_STRIDE
    sel = np.arange(N_SEL)[:, None] * SEL_LEN
    ov = np.clip(np.minimum(start + CMP_LEN, sel + SEL_LEN) - np.maximum(start, sel), 0, None) / CMP_LEN
    out = np.zeros((LANES, LANES), np.float32)
    out[:N_SEL, :N_CMP] = ov
    return out


def _block_expand_matrix():
    out = np.zeros((SEQ, LANES), np.float32)
    out[np.arange(SEQ), np.arange(SEQ) // SEL_LEN] = 1.0
    return out


def _cmp_weights(wk, wv, first):
    width = 2 * NSA_GROUPS * HEAD_DIM
    blocks = [w[first:first + CMP_STRIDE].astype(BF16) for w in (wk, wv) for _ in range(NSA_GROUPS)]
    zero = jnp.zeros_like(blocks[0])
    rows = [jnp.concatenate([blk if i == j else zero for j in range(len(blocks))], axis=2)
            for i, blk in enumerate(blocks)]
    return jnp.concatenate(rows, axis=1).reshape(CMP_STRIDE * width, width)


def _dup(v):
    return jnp.concatenate([v, v])[None, :]


def kernel(x, c, w_ada, b_ada, norm_g, w_in, w_out, cmp_pe_k, cmp_pe_v, w_cmp_k, w_cmp_v, qk_norm_g,
           dw_w, dw_b, conv_ln_g, conv_ln_b, peer_wq, peer_sub_keys, peer_u, peer_v):
    assert x.shape == (BATCH, SEQ, D_MODEL) and w_ada.shape[0] == DEPTH
    ovt = jnp.asarray(_overlap_matrix(), BF16)
    emat = jnp.asarray(_block_expand_matrix(), BF16)
    o_kv = NSA_WIDTH
    o_gate = o_kv + 6 * NSA_GROUPS * HEAD_DIM
    o_glu = o_gate + 3 * NSA_HEADS
    n_cmp_cols = 2 * NSA_GROUPS * HEAD_DIM
    xf = x.reshape(TOKENS, D_MODEL)
    for l in range(DEPTH):
        mod3 = _ada(c, w_ada[l], b_ada[l][None, :]).reshape(BATCH, 6, D_MODEL)
        wi = w_in[l]
        gate_pad = jnp.zeros((D_MODEL, LANES - 3 * GQA_REP), F32)
        wg = jnp.concatenate(
            [part for g in range(NSA_GROUPS)
             for part in (wi[:, o_gate + 3 * GQA_REP * g:o_gate + 3 * GQA_REP * (g + 1)], gate_pad)], axis=1)
        q, kvc, kvs, gl, glu = _inproj(
            xf, mod3, norm_g[l, 0][None, :],
            wi[:, :o_kv].astype(BF16), wi[:, o_kv:o_kv + n_cmp_cols].astype(BF16),
            wi[:, o_kv + n_cmp_cols:o_gate].astype(BF16), wg.astype(BF16), wi[:, o_glu:].astype(BF16))
        pe = jnp.concatenate([cmp_pe_k[l], cmp_pe_k[l], cmp_pe_v[l], cmp_pe_v[l]], axis=1)
        kv_parts = _kvprep(
            kvc.reshape(BATCH, SEQ // CMP_STRIDE, CMP_STRIDE * n_cmp_cols), kvs,
            _cmp_weights(w_cmp_k[l], w_cmp_v[l], 0), _cmp_weights(w_cmp_k[l], w_cmp_v[l], CMP_STRIDE),
            pe[:CMP_STRIDE].reshape(1, -1), pe[CMP_STRIDE:].reshape(1, -1),
            _dup(qk_norm_g[l, 1]), _dup(qk_norm_g[l, 2]), _dup(qk_norm_g[l, 3]))
        a_out = _nsa(q, gl, *kv_parts, _dup(qk_norm_g[l, 0]), ovt, emat)
        c_out = _conv(glu, dw_w[l], dw_b[l][None, :], conv_ln_g[l][None, :], conv_ln_b[l][None, :])
        x1, h2 = _outproj(a_out, c_out, xf, mod3, norm_g[l, 1][None, :],
                          w_out[l, :NSA_WIDTH].astype(BF16), w_out[l, NSA_WIDTH:].astype(BF16))
        wq = peer_wq[l].astype(BF16)
        keys = peer_sub_keys[l].reshape(2 * PEER_HEADS, PEER_NKEYS, PEER_QDIM // 2).astype(BF16)
        xf = _peer(h2, wq, keys, peer_u[l].astype(BF16), peer_v[l].astype(BF16), _route(h2, wq, keys), x1, mod3)
    return xf.reshape(BATCH, SEQ, D_MODEL)
```

```python
import functools

import numpy as np
import jax
import jax.numpy as jnp
from jax import lax
from jax.experimental import pallas as pl
from jax.experimental.pallas import tpu as pltpu

F32 = jnp.float32
BF16 = jnp.bfloat16

D_MODEL = 1024
BATCH = 8
SEQ = 2048
DEPTH = 1
TOKENS = BATCH * SEQ

HEAD_DIM = 64
NSA_HEADS = 8
NSA_GROUPS = 2
GQA_REP = NSA_HEADS // NSA_GROUPS
NSA_WIDTH = NSA_HEADS * HEAD_DIM
CONV_WIDTH = 512
CMP_LEN = 32
CMP_STRIDE = 16
N_CMP = (SEQ - CMP_LEN) // CMP_STRIDE + 1
SEL_LEN = 64
N_SEL = SEQ // SEL_LEN
SEL_TOPN = 16
WINDOW = 512
FORCE_BONUS = 1.0e4
CONV_TAPS = 31
PEER_HEADS = 8
PEER_NKEYS = 128
PEER_EXPERTS = PEER_NKEYS * PEER_NKEYS
PEER_QDIM = 256
PEER_TOPK = 16
NORM_EPS = 1e-6
NEG_INF = -1e30
NEG_BIG = -3.0e38
IDX_BIG = 1.0e9

LANES = 128
VMEM_V7X = 64 * 1024 * 1024
VMEM_LIMIT = VMEM_V7X * 3 // 4
VMEM_LIMIT_PEER = VMEM_V7X * 15 // 16

TM_PROJ = 512
TQ = 256
TK = 256
WIN_KEYS = WINDOW + TQ
V_ROWS = HEAD_DIM + 16
TS_CONV = 128
CONV_PAD = 32
TL_ROUTE = 256
TT_PEER = 512
ET_PEER = 1024
ES_PEER = 256
ROUTE_SPLIT = TT_PEER // TL_ROUTE
G_ROW_STRIDE = PEER_NKEYS + 8

_NT = (((1,), (1,)), ((), ()))


def _cparams(sem):
    return pltpu.CompilerParams(dimension_semantics=sem, vmem_limit_bytes=VMEM_LIMIT)


def _ada_body(c_ref, w_ref, b_ref, o_ref):
    c = c_ref[...]
    sc = (c * jax.nn.sigmoid(c)).astype(BF16)
    o_ref[...] = jnp.dot(sc, w_ref[...].astype(BF16), preferred_element_type=F32) + b_ref[...]


def _ada(c, w, b):
    n = w.shape[1]
    tn = 1536
    return pl.pallas_call(
        _ada_body,
        grid=(n // tn,),
        in_specs=[pl.BlockSpec((BATCH, D_MODEL), lambda j: (0, 0)),
                  pl.BlockSpec((D_MODEL, tn), lambda j: (0, j)),
                  pl.BlockSpec((1, tn), lambda j: (0, j))],
        out_specs=pl.BlockSpec((BATCH, tn), lambda j: (0, j)),
        out_shape=jax.ShapeDtypeStruct((BATCH, n), F32),
        compiler_params=_cparams(("arbitrary",)),
        name="ada_mod",
    )(c, w, b)


def _norm_mod(x, g, shift, scale):
    ms = jnp.mean(x * x, axis=-1, keepdims=True)
    y = x * lax.rsqrt(ms + NORM_EPS) * g
    return y * (1.0 + scale) + shift


def _inproj_body(x_ref, mod_ref, g_ref, wq_ref, wkc_ref, wks_ref, wg_ref, wglu_ref,
                 q_ref, kc_ref, ks_ref, gl_ref, glu_ref):
    h = _norm_mod(x_ref[...], g_ref[...], mod_ref[0, 0:1, :], mod_ref[0, 1:2, :]).astype(BF16)
    for w_ref, o_ref in ((wq_ref, q_ref), (wkc_ref, kc_ref), (wks_ref, ks_ref),
                         (wg_ref, gl_ref), (wglu_ref, glu_ref)):
        o_ref[...] = jnp.dot(h, w_ref[...], preferred_element_type=F32)


def _inproj(xf, mod3, g, wq, wkc, wks, wg, wglu):
    tiles_per_batch = SEQ // TM_PROJ
    ws = (wq, wkc, wks, wg, wglu)
    row = lambda i: (i, 0)
    return pl.pallas_call(
        _inproj_body,
        grid=(TOKENS // TM_PROJ,),
        in_specs=[pl.BlockSpec((TM_PROJ, D_MODEL), row),
                  pl.BlockSpec((1, 6, D_MODEL), lambda i: (i // tiles_per_batch, 0, 0)),
                  pl.BlockSpec((1, D_MODEL), lambda i: (0, 0))]
                 + [pl.BlockSpec(w.shape, lambda i: (0, 0)) for w in ws],
        out_specs=[pl.BlockSpec((TM_PROJ, w.shape[1]), row) for w in ws],
        out_shape=[jax.ShapeDtypeStruct((TOKENS, w.shape[1]), F32) for w in ws],
        compiler_params=_cparams(("parallel",)),
        name="in_proj",
    )(xf, mod3, g, *ws)


def _rms_pair(x, gdup, lo):
    x2 = x * x
    s_lo = jnp.sum(jnp.where(lo, x2, 0.0), axis=-1, keepdims=True)
    s_hi = jnp.sum(jnp.where(lo, 0.0, x2), axis=-1, keepdims=True)
    rs = jnp.where(lo, lax.rsqrt(s_lo * (1.0 / HEAD_DIM) + NORM_EPS),
                   lax.rsqrt(s_hi * (1.0 / HEAD_DIM) + NORM_EPS))
    return x * rs * gdup


def _key_ext(kn, lo, lane, pos):
    ext = jnp.where(lane == HEAD_DIM, (pos >> 6).astype(F32),
                    jnp.where(lane == HEAD_DIM + 1, (pos & (SEL_LEN - 1)).astype(F32),
                              jnp.where(lane == HEAD_DIM + 2, 1.0, 0.0)))
    return jnp.where(lo, kn, ext), jnp.where(lo, pltpu.roll(kn, HEAD_DIM, 1), ext)


def _kvprep_body(r_ref, kvs_ref, wa_ref, wb_ref, pea_ref, peb_ref, g1_ref, g2_ref, g3_ref,
                 kce_ref, vct_ref, kse_ref, vst_ref, kwe_ref, vwt_ref):
    lane = lax.broadcasted_iota(jnp.int32, (1, LANES), 1)
    lo = lane < HEAD_DIM
    nrow = SEQ // CMP_STRIDE
    r = r_ref[0]
    a = jnp.dot((r + pea_ref[...]).astype(BF16), wa_ref[...], preferred_element_type=F32)
    b = jnp.dot((r + peb_ref[...]).astype(BF16), wb_ref[...], preferred_element_type=F32)
    c = a + pltpu.roll(b, nrow - 1, 0)
    end = lax.broadcasted_iota(jnp.int32, (nrow, 1), 0) * CMP_STRIDE + (CMP_LEN - 1)
    kc0, kc1 = _key_ext(_rms_pair(c[:, :LANES], g1_ref[...], lo), lo, lane, end)
    kce_ref[0, 0] = kc0.astype(BF16)
    kce_ref[0, 1] = kc1.astype(BF16)
    vct = c[:, LANES:].T.astype(BF16)
    vct_ref[0, 0] = vct[:HEAD_DIM]
    vct_ref[0, 1] = vct[HEAD_DIM:]

    rows = 256
    ones = jnp.ones((V_ROWS - HEAD_DIM, rows), F32)

    def chunk(i, carry):
        r0 = pl.multiple_of(i * rows, rows)
        blk = kvs_ref[pl.ds(r0, rows), :]
        pos = r0 + lax.broadcasted_iota(jnp.int32, (rows, 1), 0)
        for k_ref, v_ref, gain, off in ((kse_ref, vst_ref, g2_ref, 0), (kwe_ref, vwt_ref, g3_ref, 2 * LANES)):
            k0, k1 = _key_ext(_rms_pair(blk[:, off:off + LANES], gain[...], lo), lo, lane, pos)
            k_ref[0, 0, pl.ds(r0, rows), :] = k0.astype(BF16)
            k_ref[0, 1, pl.ds(r0, rows), :] = k1.astype(BF16)
            vt = blk[:, off + LANES:off + 2 * LANES].T
            for g in range(NSA_GROUPS):
                v_ref[0, g, :, pl.ds(r0, rows)] = jnp.concatenate(
                    [vt[g * HEAD_DIM:(g + 1) * HEAD_DIM], ones], axis=0).astype(BF16)
        return carry

    lax.fori_loop(0, SEQ // rows, chunk, 0)


def _kvprep(rmat, kvs, wa, wb, pea, peb, g1, g2, g3):
    nrow = SEQ // CMP_STRIDE
    const2 = lambda b: (0, 0)
    per_b = lambda b: (b, 0, 0, 0)
    shapes = [(nrow, LANES), (HEAD_DIM, nrow), (SEQ, LANES), (V_ROWS, SEQ), (SEQ, LANES), (V_ROWS, SEQ)]
    return pl.pallas_call(
        _kvprep_body,
        grid=(BATCH,),
        in_specs=[pl.BlockSpec((1, nrow, rmat.shape[2]), lambda b: (b, 0, 0)),
                  pl.BlockSpec((SEQ, kvs.shape[1]), lambda b: (b, 0)),
                  pl.BlockSpec(wa.shape, const2), pl.BlockSpec(wb.shape, const2),
                  pl.BlockSpec(pea.shape, const2), pl.BlockSpec(peb.shape, const2),
                  pl.BlockSpec((1, LANES), const2), pl.BlockSpec((1, LANES), const2),
                  pl.BlockSpec((1, LANES), const2)],
        out_specs=[pl.BlockSpec((1, NSA_GROUPS) + s, per_b) for s in shapes],
        out_shape=[jax.ShapeDtypeStruct((BATCH, NSA_GROUPS) + s, BF16) for s in shapes],
        compiler_params=_cparams(("parallel",)),
        name="kv_prep",
    )(rmat, kvs, wa, wb, pea, peb, g1, g2, g3)


def _loop(trips, body, init):
    if isinstance(trips, int):
        for i in range(trips):
            init = body(i, init)
        return init
    return lax.fori_loop(0, trips, body, init)


def _split3(x):
    p1 = x.astype(BF16)
    r1 = x - p1.astype(F32)
    p2 = r1.astype(BF16)
    p3 = (r1 - p2.astype(F32)).astype(BF16)
    return p1, p2, p3


def _nsa_body(q_ref, gl_ref, kce_ref, vct_ref, kse_ref, vst_ref, kwe_ref, vwt_ref,
              g0_ref, ovt_ref, et_ref, o_ref, s_scr):
    qi = pl.program_id(1)
    q0 = qi * TQ
    lane = lax.broadcasted_iota(jnp.int32, (1, LANES), 1)
    lo = lane < HEAD_DIM
    t_idx = q0 + lax.broadcasted_iota(jnp.int32, (1, TQ), 1)
    q0f = q0.astype(F32)

    n_sub = lax.broadcasted_iota(jnp.int32, (SEQ // CMP_STRIDE, 1), 0)
    cmask = (t_idx >= n_sub * CMP_STRIDE + (CMP_LEN - 1)) & (n_sub < N_CMP)
    j = lax.broadcasted_iota(jnp.int32, (N_SEL, 1), 0)
    tb = t_idx >> 6
    bonus = jnp.where((j == 0) | (j == tb) | (j == tb - 1), FORCE_BONUS, 0.0)

    qe, o_cmp, sels = [], [], []
    for g in range(NSA_GROUPS):
        for p in range(GQA_REP // 2):
            c0 = (g * GQA_REP // 2 + p) * LANES
            qn = _rms_pair(q_ref[:, c0:c0 + LANES], g0_ref[...], lo) * (HEAD_DIM ** -0.5)
            for half, base in ((0, qn), (1, pltpu.roll(qn, HEAD_DIM, 1))):
                slope = 2.0 ** -(g * GQA_REP + 2 * p + half + 1)
                ext = jnp.where(lane == HEAD_DIM, SEL_LEN * slope,
                                jnp.where(lane == HEAD_DIM + 1, slope,
                                          jnp.where(lane == HEAD_DIM + 2, -slope * q0f, 0.0)))
                qe.append(jnp.where(lo, base, ext).astype(BF16))
        heads = range(g * GQA_REP, (g + 1) * GQA_REP)

        kce = kce_ref[0, g]
        vct = vct_ref[0, g]
        psum = jnp.zeros((SEQ // CMP_STRIDE, TQ), F32)
        for h in heads:
            s = jnp.where(cmask, lax.dot_general(kce, qe[h], _NT, preferred_element_type=F32), NEG_INF)
            e = jnp.where(cmask, jnp.exp(s - jnp.max(s, axis=0, keepdims=True)), 0.0)
            l = jnp.sum(e, axis=0, keepdims=True)
            p = e / jnp.where(l > 0.0, l, 1.0)
            psum = psum + p
            o_cmp.append(jnp.dot(vct, p.astype(BF16), preferred_element_type=F32))

        imp = jnp.zeros((LANES, TQ), F32)
        for part in _split3(psum):
            imp = imp + jnp.dot(ovt_ref[...], part, preferred_element_type=F32)
        imp = jnp.where(j <= tb, imp[:N_SEL] + bonus, -1.0)
        rank = jnp.zeros((N_SEL, TQ), F32)
        for i in range(N_SEL):
            ci = imp[i:i + 1, :]
            ahead = (ci > imp) | ((ci == imp) & (j > i))
            rank = rank + jnp.where(ahead, 1.0, 0.0)
        sels.append(jnp.concatenate([jnp.where(rank < float(SEL_TOPN), 1.0, 0.0),
                                     jnp.zeros((LANES - N_SEL, TQ), F32)], axis=0).astype(BF16))

    key_sub = lax.broadcasted_iota(jnp.int32, (TK, 1), 0)
    neg_row = tuple(jnp.full((1, TQ), NEG_INF, F32) for _ in range(NSA_HEADS))
    zero_acc = tuple(jnp.zeros((V_ROWS, TQ), F32) for _ in range(NSA_HEADS))

    def two_pass(n_tiles, first_key, k_ref, v_ref, bias_fn, per_trip=1):
        def score_step(i, ms):
            ms = list(ms)
            for sub in range(per_trip):
                k0 = pl.multiple_of(first_key + (i * per_trip + sub) * TK, LANES)
                r0 = pl.multiple_of((i * per_trip + sub) * TK, TK)
                for g in range(NSA_GROUPS):
                    kblk = k_ref[0, g, pl.ds(k0, TK), :]
                    bias = bias_fn(g, k0)
                    for h in range(g * GQA_REP, (g + 1) * GQA_REP):
                        s = lax.dot_general(kblk, qe[h], _NT, preferred_element_type=F32) + bias
                        s_scr[h, pl.ds(r0, TK), :] = s
                        ms[h] = jnp.maximum(ms[h], jnp.max(s, axis=0, keepdims=True))
            return tuple(ms)

        ms = _loop(n_tiles, score_step, neg_row)

        def value_step(i, accs):
            accs = list(accs)
            for sub in range(per_trip):
                k0 = pl.multiple_of(first_key + (i * per_trip + sub) * TK, LANES)
                r0 = pl.multiple_of((i * per_trip + sub) * TK, TK)
                for g in range(NSA_GROUPS):
                    vt = v_ref[0, g, :, pl.ds(k0, TK)]
                    for h in range(g * GQA_REP, (g + 1) * GQA_REP):
                        e = jnp.exp(s_scr[h, pl.ds(r0, TK), :] - ms[h])
                        accs[h] = accs[h] + jnp.dot(vt, e.astype(BF16), preferred_element_type=F32)
            return tuple(accs)

        accs = _loop(n_tiles, value_step, zero_acc)
        return [a[:HEAD_DIM] / a[HEAD_DIM:HEAD_DIM + 1] for a in accs]

    start = pl.multiple_of(jnp.maximum(q0 - WINDOW, 0), LANES)

    def window_bias(g, k0):
        wd = t_idx - (k0 + key_sub)
        return jnp.where((wd >= 0) & (wd < WINDOW), 0.0, NEG_INF)

    o_win = two_pass(WIN_KEYS // TK, start, kwe_ref, vwt_ref, window_bias)

    def selected_bias(g, k0):
        chosen = jnp.dot(et_ref[pl.ds(k0, TK), :], sels[g], preferred_element_type=F32) > 0.5
        return jnp.where(chosen & (k0 + key_sub <= t_idx), 0.0, NEG_INF)

    o_slc = two_pass(((q0 + TQ - 1) // TK + 2) // 2, 0, kse_ref, vst_ref, selected_bias, per_trip=2)

    for g in range(NSA_GROUPS):
        sg = jax.nn.sigmoid(gl_ref[:, g * LANES:(g + 1) * LANES].T[:4 * GQA_REP])
        for p in range(GQA_REP // 2):
            pair = []
            for r in (2 * p, 2 * p + 1):
                h = g * GQA_REP + r
                pair.append(sg[3 * r:3 * r + 1] * o_cmp[h] + sg[3 * r + 1:3 * r + 2] * o_slc[h]
                            + sg[3 * r + 2:3 * r + 3] * o_win[h])
            c0 = (g * GQA_REP // 2 + p) * LANES
            o_ref[:, c0:c0 + LANES] = jnp.concatenate(pair, axis=0).T.astype(o_ref.dtype)


def _nsa(q, gl, kce, vct, kse, vst, kwe, vwt, g0, ovt, et):
    nq = SEQ // TQ
    tile = lambda b, i: (b * nq + i, 0)
    per_b = lambda b, i: (b, 0, 0, 0)
    const2 = lambda b, i: (0, 0)
    return pl.pallas_call(
        _nsa_body,
        grid=(BATCH, nq),
        in_specs=[pl.BlockSpec((TQ, NSA_WIDTH), tile), pl.BlockSpec((TQ, NSA_GROUPS * LANES), tile)]
                 + [pl.BlockSpec((1,) + a.shape[1:], per_b) for a in (kce, vct, kse, vst, kwe, vwt)]
                 + [pl.BlockSpec((1, LANES), const2), pl.BlockSpec(ovt.shape, const2),
                    pl.BlockSpec(et.shape, const2)],
        out_specs=pl.BlockSpec((TQ, NSA_WIDTH), tile),
        out_shape=jax.ShapeDtypeStruct((TOKENS, NSA_WIDTH), BF16),
        scratch_shapes=[pltpu.VMEM((NSA_HEADS, SEQ, TQ), F32)],
        compiler_params=_cparams(("parallel", "arbitrary")),
        name="nsa_attention",
    )(q, gl, kce, vct, kse, vst, kwe, vwt, g0, ovt, et)


def _conv_body(glu_ref, w_ref, b_ref, lg_ref, lb_ref, o_ref, u_scr, sh_scr):
    u_scr[0:CONV_PAD, :] = jnp.zeros((CONV_PAD, CONV_WIDTH), F32)
    rows = 256

    def fill(i, carry):
        r0 = pl.multiple_of(i * rows, rows)
        blk = glu_ref[pl.ds(r0, rows), :]
        u_scr[pl.ds(CONV_PAD + r0, rows), :] = blk[:, :CONV_WIDTH] * jax.nn.sigmoid(blk[:, CONV_WIDTH:])
        return carry

    lax.fori_loop(0, SEQ // rows, fill, 0)
    first = CONV_PAD - (CONV_TAPS - 1)

    def tile(i, carry):
        r0 = pl.multiple_of(i * TS_CONV, TS_CONV)
        win = u_scr[pl.ds(r0, TS_CONV + CONV_PAD), :]
        span = TS_CONV + CONV_PAD - 8
        for s in range(1, 8):
            sh_scr[s - 1] = win[s:s + span, :]
        acc = jnp.zeros((TS_CONV, CONV_WIDTH), F32) + b_ref[...]
        for k in range(CONV_TAPS):
            s = (first + k) % 8
            base = first + k - s
            tap = win[base:base + TS_CONV, :] if s == 0 else sh_scr[s - 1, base:base + TS_CONV, :]
            acc = acc + tap * w_ref[k:k + 1, :]
        mu = jnp.mean(acc, axis=-1, keepdims=True)
        d = acc - mu
        var = jnp.mean(d * d, axis=-1, keepdims=True)
        yn = d * lax.rsqrt(var + NORM_EPS) * lg_ref[...] + lb_ref[...]
        o_ref[pl.ds(r0, TS_CONV), :] = (yn * jax.nn.sigmoid(yn)).astype(o_ref.dtype)
        return carry

    lax.fori_loop(0, SEQ // TS_CONV, tile, 0)


def _conv(glu, w, b, lg, lb):
    const2 = lambda i: (0, 0)
    return pl.pallas_call(
        _conv_body,
        grid=(BATCH,),
        in_specs=[pl.BlockSpec((SEQ, 2 * CONV_WIDTH), lambda i: (i, 0)),
                  pl.BlockSpec(w.shape, const2), pl.BlockSpec(b.shape, const2),
                  pl.BlockSpec(lg.shape, const2), pl.BlockSpec(lb.shape, const2)],
        out_specs=pl.BlockSpec((SEQ, CONV_WIDTH), lambda i: (i, 0)),
        out_shape=jax.ShapeDtypeStruct((TOKENS, CONV_WIDTH), BF16),
        scratch_shapes=[pltpu.VMEM((CONV_PAD + SEQ, CONV_WIDTH), F32),
                        pltpu.VMEM((7, TS_CONV + CONV_PAD - 8, CONV_WIDTH), F32)],
        compiler_params=_cparams(("parallel",)),
        name="conv_mixer",
    )(glu, w, b, lg, lb)


def _outproj_body(a_ref, c_ref, x_ref, mod_ref, g_ref, wa_ref, wc_ref, x1_ref, h2_ref):
    mix = (jnp.dot(a_ref[...], wa_ref[...], preferred_element_type=F32)
           + jnp.dot(c_ref[...], wc_ref[...], preferred_element_type=F32))
    x1 = x_ref[...] + mod_ref[0, 2:3, :] * mix
    x1_ref[...] = x1
    h2_ref[...] = _norm_mod(x1, g_ref[...], mod_ref[0, 3:4, :], mod_ref[0, 4:5, :]).astype(BF16)


def _outproj(a, c, xf, mod3, g, wa, wc):
    tiles_per_batch = SEQ // TM_PROJ
    row = lambda i: (i, 0)
    const2 = lambda i: (0, 0)
    return pl.pallas_call(
        _outproj_body,
        grid=(TOKENS // TM_PROJ,),
        in_specs=[pl.BlockSpec((TM_PROJ, NSA_WIDTH), row),
                  pl.BlockSpec((TM_PROJ, CONV_WIDTH), row),
                  pl.BlockSpec((TM_PROJ, D_MODEL), row),
                  pl.BlockSpec((1, 6, D_MODEL), lambda i: (i // tiles_per_batch, 0, 0)),
                  pl.BlockSpec((1, D_MODEL), const2),
                  pl.BlockSpec(wa.shape, const2), pl.BlockSpec(wc.shape, const2)],
        out_specs=[pl.BlockSpec((TM_PROJ, D_MODEL), row), pl.BlockSpec((TM_PROJ, D_MODEL), row)],
        out_shape=[jax.ShapeDtypeStruct((TOKENS, D_MODEL), F32),
                   jax.ShapeDtypeStruct((TOKENS, D_MODEL), BF16)],
        compiler_params=_cparams(("parallel",)),
        name="out_proj",
    )(a, c, xf, mod3, g, wa, wc)


_SORT4 = ((0, 1), (2, 3), (0, 2), (1, 3), (1, 2))
POPS_PER_SLICE = 4


def _run(gen):
    try:
        while True:
            next(gen)
    except StopIteration as stop:
        return stop.value


def _interleave(*gens):
    live = list(gens)
    while live:
        for gen in list(live):
            try:
                next(gen)
            except StopIteration:
                live.remove(gen)


def _topk_rows(x, k):
    n, cols = x.shape
    q = n // 4
    row = lax.broadcasted_iota(jnp.int32, (q, cols), 0).astype(F32)
    vals = [x[i * q:(i + 1) * q] for i in range(4)]
    idxs = [row + float(i * q) for i in range(4)]
    for i, j in _SORT4:
        swap = (vals[j] > vals[i]) | ((vals[j] == vals[i]) & (idxs[j] < idxs[i]))
        vals[i], vals[j] = jnp.where(swap, vals[j], vals[i]), jnp.where(swap, vals[i], vals[j])
        idxs[i], idxs[j] = jnp.where(swap, idxs[j], idxs[i]), jnp.where(swap, idxs[i], idxs[j])
    yield
    slot = lax.broadcasted_iota(jnp.int32, (k, cols), 0)
    out_v = jnp.zeros((k, cols), F32)
    out_i = jnp.zeros((k, cols), F32)
    for it in range(k):
        m = jnp.max(vals[0], axis=0, keepdims=True)
        idx = jnp.min(jnp.where(vals[0] == m, idxs[0], IDX_BIG), axis=0, keepdims=True)
        hit = idxs[0] == idx
        for lvl in range(3):
            vals[lvl] = jnp.where(hit, vals[lvl + 1], vals[lvl])
            idxs[lvl] = jnp.where(hit, idxs[lvl + 1], idxs[lvl])
        vals[3] = jnp.where(hit, NEG_BIG, vals[3])
        out_v = jnp.where(slot == it, m, out_v)
        out_i = jnp.where(slot == it, idx, out_i)
        if it % POPS_PER_SLICE == POPS_PER_SLICE - 1:
            yield
    return out_v, out_i


def _pair_topk(v1, v2):
    k, cols = v1.shape
    half = k // 2
    alo = lax.broadcasted_iota(jnp.int32, (half, cols), 0).astype(F32)
    ahi = alo + float(half)
    levels = []
    for b in range(k):
        lvl = v1[:half] + jnp.broadcast_to(v2[b:b + 1, :], (half, cols))
        if k // (b + 1) < half:
            lvl = jnp.where(alo < float(k // (b + 1)), lvl, NEG_BIG)
        levels.append(lvl)
    top_hi = v1[half:] + jnp.broadcast_to(v2[0:1, :], (half, cols))
    depth = jnp.zeros((half, cols), F32)
    slot = lax.broadcasted_iota(jnp.int32, (k, cols), 0)
    tops = jnp.zeros((k, cols), F32)
    a_out = jnp.zeros((k, cols), F32)
    b_out = jnp.zeros((k, cols), F32)
    for it in range(k):
        m = jnp.max(jnp.maximum(levels[0], top_hi), axis=0, keepdims=True)
        a_sel = jnp.min(jnp.minimum(jnp.where(levels[0] == m, alo, IDX_BIG),
                                    jnp.where(top_hi == m, ahi, IDX_BIG)), axis=0, keepdims=True)
        hit = alo == a_sel
        b_sel = jnp.sum(jnp.where(hit, depth, 0.0), axis=0, keepdims=True)
        depth = jnp.where(hit, depth + 1.0, depth)
        for b in range(k - 1):
            levels[b] = jnp.where(hit, levels[b + 1], levels[b])
        levels[k - 1] = jnp.where(hit, NEG_BIG, levels[k - 1])
        top_hi = jnp.where(ahi == a_sel, NEG_BIG, top_hi)
        tops = jnp.where(slot == it, m, tops)
        a_out = jnp.where(slot == it, a_sel, a_out)
        b_out = jnp.where(slot == it, b_sel, b_out)
        if it % POPS_PER_SLICE == POPS_PER_SLICE - 1:
            yield
    return tops, a_out, b_out


def _route_head(qp, key_a, key_b):
    k = PEER_TOPK
    tv, ti = [], []
    for c, keys in enumerate((key_a, key_b)):
        qs = qp[:, c * LANES:(c + 1) * LANES].astype(BF16)
        st = lax.dot_general(keys, qs, _NT, preferred_element_type=F32)
        v, i = yield from _topk_rows(st, k)
        tv.append(v)
        ti.append(i)
    tops, a_sel, b_sel = yield from _pair_topk(tv[0], tv[1])
    i1s = jnp.zeros_like(tops)
    i2s = jnp.zeros_like(tops)
    for a in range(k):
        i1s = i1s + jnp.where(a_sel == float(a), jnp.broadcast_to(ti[0][a:a + 1, :], tops.shape), 0.0)
        i2s = i2s + jnp.where(b_sel == float(a), jnp.broadcast_to(ti[1][a:a + 1, :], tops.shape), 0.0)
    e = jnp.exp(tops - jnp.max(tops, axis=0, keepdims=True))
    return i1s, i2s, e / jnp.sum(e, axis=0, keepdims=True)


def _route_body(h_ref, wq_ref, keys_ref, sel_ref, qp_scr):
    qp_scr[...] = jnp.dot(h_ref[...], wq_ref[...], preferred_element_type=F32)
    k = PEER_TOPK

    def head(h, carry):
        c0 = pl.multiple_of(h * PEER_QDIM, PEER_QDIM)
        parts = _run(_route_head(qp_scr[:, pl.ds(c0, PEER_QDIM)], keys_ref[2 * h], keys_ref[2 * h + 1]))
        r0 = pl.multiple_of(h * k, k)
        for a, val in enumerate(parts):
            sel_ref[a, pl.ds(r0, k), :] = val
        return carry

    lax.fori_loop(0, PEER_HEADS, head, 0, unroll=4)


def _route(h2, wq, keys):
    n = TT_PEER
    nsel = PEER_HEADS * PEER_TOPK
    return pl.pallas_call(
        _route_body,
        grid=(n // TL_ROUTE,),
        in_specs=[pl.BlockSpec((TL_ROUTE, D_MODEL), lambda i: (i, 0)),
                  pl.BlockSpec(wq.shape, lambda i: (0, 0)),
                  pl.BlockSpec(keys.shape, lambda i: (0, 0, 0))],
        out_specs=pl.BlockSpec((3, nsel, TL_ROUTE), lambda i: (0, 0, i)),
        out_shape=jax.ShapeDtypeStruct((3, nsel, n), F32),
        scratch_shapes=[pltpu.VMEM((TL_ROUTE, PEER_HEADS * PEER_QDIM), F32)],
        compiler_params=_cparams(("parallel",)),
        name="peer_route_first",
    )(h2, wq, keys)


def _peer_body(h_ref, hn_ref, wq_ref, keys_ref, u_ref, v_ref, first_ref, x1_ref, mod_ref, o_ref,
               g_scr, cur_scr, nxt_scr):
    i = pl.program_id(0)
    e = pl.program_id(1)
    n = PEER_NKEYS

    @pl.when(e == 0)
    def _():
        @pl.when(i == 0)
        def _():
            for a in range(3):
                cur_scr[a] = first_ref[a].T

        @pl.when(i > 0)
        def _():
            for a in range(3):
                cur_scr[a] = nxt_scr[a].T

        o_ref[...] = jnp.zeros(o_ref.shape, F32)
        sub = lax.broadcasted_iota(jnp.int32, (n, n), 0).astype(F32)

        def tok(t, carry):
            w = cur_scr[2, pl.ds(t, 1), :]
            w_hi = w.astype(BF16).astype(F32)
            m1 = sub == cur_scr[0, pl.ds(t, 1), :]
            x1 = jnp.concatenate([jnp.where(m1, w_hi, 0.0), jnp.where(m1, w - w_hi, 0.0)], axis=1).astype(BF16)
            x2h = jnp.where(sub == cur_scr[1, pl.ds(t, 1), :], 1.0, 0.0)
            x2 = jnp.concatenate([x2h, x2h], axis=1).astype(BF16)
            g = lax.dot_general(x1, x2, _NT, preferred_element_type=F32)
            g_scr[pl.ds(pl.multiple_of(t * G_ROW_STRIDE, 8), n), :] = g
            return carry

        lax.fori_loop(0, TT_PEER, tok, 0, unroll=64)

    def route_task():
        t0 = pl.multiple_of((e % ROUTE_SPLIT) * TL_ROUTE, TL_ROUTE)
        qp = jnp.dot(hn_ref[pl.ds(t0, TL_ROUTE), :], wq_ref[...], preferred_element_type=F32)
        parts = yield from _route_head(qp, keys_ref[0], keys_ref[1])
        r0 = pl.multiple_of((e // ROUTE_SPLIT) * PEER_TOPK, PEER_TOPK)
        for a, val in enumerate(parts):
            nxt_scr[a, pl.ds(r0, PEER_TOPK), pl.ds(t0, TL_ROUTE)] = val

    def expert_task():
        h = h_ref[...]
        parts = []
        for s in range(ET_PEER // ES_PEER):
            z = lax.dot_general(h, u_ref[s * ES_PEER:(s + 1) * ES_PEER, :], _NT,
                                preferred_element_type=F32)
            yield
            act = 0.5 * z * (1.0 + lax.erf(z * (2.0 ** -0.5)))
            for c in range(ES_PEER // n):
                i1 = e * (ET_PEER // n) + s * (ES_PEER // n) + c
                gc = g_scr[pl.ds(i1, TT_PEER, stride=G_ROW_STRIDE), :]
                parts.append((act[:, c * n:(c + 1) * n] * gc).astype(BF16))
                yield
        o_ref[...] += jnp.dot(jnp.concatenate(parts, axis=1), v_ref[...], preferred_element_type=F32)

    _interleave(expert_task(), route_task())

    @pl.when(e == pl.num_programs(1) - 1)
    def _():
        o_ref[...] = x1_ref[...] + mod_ref[0, 5:6, :] * o_ref[...]


def _peer(h2, wq, keys, u, v, first, x1, mod3):
    n_tiles = TOKENS // TT_PEER
    n_steps = PEER_EXPERTS // ET_PEER
    assert n_steps == PEER_HEADS * ROUTE_SPLIT
    tiles_per_batch = SEQ // TT_PEER
    nsel = PEER_HEADS * PEER_TOPK
    tok = lambda i, e: (i, 0)
    exp = lambda i, e: (e, 0)
    once = pl.Buffered(1)
    return pl.pallas_call(
        _peer_body,
        grid=(n_tiles, n_steps),
        in_specs=[pl.BlockSpec((TT_PEER, D_MODEL), tok, pipeline_mode=once),
                  pl.BlockSpec((TT_PEER, D_MODEL), lambda i, e: (jnp.minimum(i + 1, n_tiles - 1), 0),
                               pipeline_mode=once),
                  pl.BlockSpec((D_MODEL, PEER_QDIM), lambda i, e: (0, e // ROUTE_SPLIT)),
                  pl.BlockSpec((2, PEER_NKEYS, PEER_QDIM // 2), lambda i, e: (e // ROUTE_SPLIT, 0, 0)),
                  pl.BlockSpec((ET_PEER, D_MODEL), exp),
                  pl.BlockSpec((ET_PEER, D_MODEL), exp),
                  pl.BlockSpec((3, nsel, TT_PEER), lambda i, e: (0, 0, 0), pipeline_mode=once),
                  pl.BlockSpec((TT_PEER, D_MODEL), tok, pipeline_mode=once),
                  pl.BlockSpec((1, 6, D_MODEL), lambda i, e: (i // tiles_per_batch, 0, 0))],
        out_specs=pl.BlockSpec((TT_PEER, D_MODEL), tok),
        out_shape=jax.ShapeDtypeStruct((TOKENS, D_MODEL), F32),
        scratch_shapes=[pltpu.VMEM((TT_PEER * G_ROW_STRIDE, PEER_NKEYS), F32),
                        pltpu.VMEM((3, TT_PEER, nsel), F32),
                        pltpu.VMEM((3, nsel, TT_PEER), F32)],
        compiler_params=pltpu.CompilerParams(dimension_semantics=("arbitrary", "arbitrary"),
                                             vmem_limit_bytes=VMEM_LIMIT_PEER),
        name="peer_experts",
    )(h2, h2, wq, keys, u, v, first, x1, mod3)


def _overlap_matrix():
    start = np.arange(N_CMP)[None, :] * CMP_STRIDE
    sel = np.arange(N_SEL)[:, None] * SEL_LEN
    ov = np.clip(np.minimum(start + CMP_LEN, sel + SEL_LEN) - np.maximum(start, sel), 0, None) / CMP_LEN
    out = np.zeros((LANES, LANES), np.float32)
    out[:N_SEL, :N_CMP] = ov
    return out


def _block_expand_matrix():
    out = np.zeros((SEQ, LANES), np.float32)
    out[np.arange(SEQ), np.arange(SEQ) // SEL_LEN] = 1.0
    return out


def _cmp_weights(wk, wv, first):
    width = 2 * NSA_GROUPS * HEAD_DIM
    blocks = [w[first:first + CMP_STRIDE].astype(BF16) for w in (wk, wv) for _ in range(NSA_GROUPS)]
    zero = jnp.zeros_like(blocks[0])
    rows = [jnp.concatenate([blk if i == j else zero for j in range(len(blocks))], axis=2)
            for i, blk in enumerate(blocks)]
    return jnp.concatenate(rows, axis=1).reshape(CMP_STRIDE * width, width)


def _dup(v):
    return jnp.concatenate([v, v])[None, :]


def kernel(x, c, w_ada, b_ada, norm_g, w_in, w_out, cmp_pe_k, cmp_pe_v, w_cmp_k, w_cmp_v, qk_norm_g,
           dw_w, dw_b, conv_ln_g, conv_ln_b, peer_wq, peer_sub_keys, peer_u, peer_v):
    assert x.shape == (BATCH, SEQ, D_MODEL) and w_ada.shape[0] == DEPTH
    ovt = jnp.asarray(_overlap_matrix(), BF16)
    emat = jnp.asarray(_block_expand_matrix(), BF16)
    o_kv = NSA_WIDTH
    o_gate = o_kv + 6 * NSA_GROUPS * HEAD_DIM
    o_glu = o_gate + 3 * NSA_HEADS
    n_cmp_cols = 2 * NSA_GROUPS * HEAD_DIM
    xf = x.reshape(TOKENS, D_MODEL)
    for l in range(DEPTH):
        mod3 = _ada(c, w_ada[l], b_ada[l][None, :]).reshape(BATCH, 6, D_MODEL)
        wi = w_in[l]
        gate_pad = jnp.zeros((D_MODEL, LANES - 3 * GQA_REP), F32)
        wg = jnp.concatenate(
            [part for g in range(NSA_GROUPS)
             for part in (wi[:, o_gate + 3 * GQA_REP * g:o_gate + 3 * GQA_REP * (g + 1)], gate_pad)], axis=1)
        q, kvc, kvs, gl, glu = _inproj(
            xf, mod3, norm_g[l, 0][None, :],
            wi[:, :o_kv].astype(BF16), wi[:, o_kv:o_kv + n_cmp_cols].astype(BF16),
            wi[:, o_kv + n_cmp_cols:o_gate].astype(BF16), wg.astype(BF16), wi[:, o_glu:].astype(BF16))
        pe = jnp.concatenate([cmp_pe_k[l], cmp_pe_k[l], cmp_pe_v[l], cmp_pe_v[l]], axis=1)
        kv_parts = _kvprep(
            kvc.reshape(BATCH, SEQ // CMP_STRIDE, CMP_STRIDE * n_cmp_cols), kvs,
            _cmp_weights(w_cmp_k[l], w_cmp_v[l], 0), _cmp_weights(w_cmp_k[l], w_cmp_v[l], CMP_STRIDE),
            pe[:CMP_STRIDE].reshape(1, -1), pe[CMP_STRIDE:].reshape(1, -1),
            _dup(qk_norm_g[l, 1]), _dup(qk_norm_g[l, 2]), _dup(qk_norm_g[l, 3]))
        a_out = _nsa(q, gl, *kv_parts, _dup(qk_norm_g[l, 0]), ovt, emat)
        c_out = _conv(glu, dw_w[l], dw_b[l][None, :], conv_ln_g[l][None, :], conv_ln_b[l][None, :])
        x1, h2 = _outproj(a_out, c_out, xf, mod3, norm_g[l, 1][None, :],
                          w_out[l, :NSA_WIDTH].astype(BF16), w_out[l, NSA_WIDTH:].astype(BF16))
        wq = peer_wq[l].astype(BF16)
        keys = peer_sub_keys[l].reshape(2 * PEER_HEADS, PEER_NKEYS, PEER_QDIM // 2).astype(BF16)
        xf = _peer(h2, wq, keys, peer_u[l].astype(BF16), peer_v[l].astype(BF16), _route(h2, wq, keys), x1, mod3)
    return xf.reshape(BATCH, SEQ, D_MODEL)
```
